```python
import jax, jax.numpy as jnp
from jax import lax
import numpy as np

D_MODEL = 1024
BATCH = 4
SEQ = 4096
DEPTH = 4

HEAD_DIM = 64
ROPE_THETA = 10000.0
NORM_EPS = 1e-6
GMLP_GROUPS = 4
GMLP_WIDTH = GMLP_GROUPS * HEAD_DIM
GMLP_CHUNK = 128
DIL_CONFIGS = ((128, 1), (512, 4), (2048, 16))
DIL_HEADS_PER_GROUP = 4
DIL_HEADS = DIL_HEADS_PER_GROUP * len(DIL_CONFIGS)
DIL_WIDTH = DIL_HEADS * HEAD_DIM
DIL_OUT_WIDTH = DIL_HEADS_PER_GROUP * HEAD_DIM
EVEN_IN = 2 * GMLP_WIDTH + 3 * DIL_WIDTH
EVEN_OUT = GMLP_WIDTH + DIL_OUT_WIDTH
MOBA_HEADS = 12
MOBA_WIDTH = MOBA_HEADS * HEAD_DIM
MOBA_BLOCK = 256
MOBA_TOPK = 3
MOBA_Q_CHUNK = 16
CONV_WIDTH = 256
CONV_TAPS = 3
ODD_IN = 3 * MOBA_WIDTH + 3 * CONV_WIDTH
ODD_OUT = MOBA_WIDTH + CONV_WIDTH
D_FF = 2816
N_EXPERTS = 8
TOP_K = 2
D_FF_EXPERT = 3584
N_EVEN = (DEPTH + 1) // 2
N_ODD = DEPTH // 2

kernel_name = "hybrid_gmlp_dilated_moba_shortconv_moe"


def rms_norm(x, g):
    xf = x.astype(jnp.float32)
    y = xf * lax.rsqrt(jnp.mean(xf * xf, axis=-1, keepdims=True) + NORM_EPS)
    return (y * g.astype(jnp.float32)).astype(x.dtype)


def rope_tables(seq):
    inv = 1.0 / (ROPE_THETA ** (jnp.arange(0, HEAD_DIM, 2, dtype=jnp.float32) / HEAD_DIM))
    ang = jnp.arange(seq, dtype=jnp.float32)[:, None] * inv[None, :]
    ang = jnp.concatenate([ang, ang], axis=-1)
    return jnp.cos(ang)[:, None, :], jnp.sin(ang)[:, None, :]


def apply_rope(x, cos, sin):
    xf = x.astype(jnp.float32)
    half = HEAD_DIM // 2
    rot = jnp.concatenate([-xf[..., half:], xf[..., :half]], axis=-1)
    return (xf * cos + rot * sin).astype(x.dtype)


def prep_qk(q, k, gq, gk, cos, sin):
    return apply_rope(rms_norm(q, gq), cos, sin), apply_rope(rms_norm(k, gk), cos, sin)


def swiglu(h, wg, wu, wd):
    return (jax.nn.silu(h @ wg) * (h @ wu)) @ wd


def gmlp_spatial_gating(z, v_gain, ws, bs):
    b, s, _ = z.shape
    z = jax.nn.gelu(z)
    u, v = z[..., :GMLP_WIDTH], z[..., GMLP_WIDTH:]
    v = rms_norm(v.reshape(b, s, GMLP_GROUPS, HEAD_DIM), v_gain.reshape(GMLP_GROUPS, HEAD_DIM))
    v = v.reshape(b, s // GMLP_CHUNK, GMLP_CHUNK, GMLP_GROUPS, HEAD_DIM)
    causal = jnp.tril(jnp.ones((GMLP_CHUNK, GMLP_CHUNK), dtype=bool))
    w = jnp.where(causal[None], ws, jnp.zeros_like(ws))
    y = jnp.einsum('gij,bcjgd->bcigd', w, v) + jnp.swapaxes(bs, 0, 1)[:, :, None]
    return u * y.reshape(b, s, GMLP_WIDTH)


def banded_causal_attention(q, k, v, band):
    n, L, h, dh = q.shape
    nb = -(-L // band)
    pad = nb * band - L

    def blocks(t):
        return jnp.pad(t, ((0, 0), (0, pad), (0, 0), (0, 0))).reshape(n, nb, band, h, dh)

    def with_prev(t):
        prev = jnp.pad(t[:, :-1], ((0, 0), (1, 0), (0, 0), (0, 0), (0, 0)))
        return jnp.concatenate([prev, t], axis=2)

    qb, kw, vw = blocks(q), with_prev(blocks(k)), with_prev(blocks(v))
    sc = jnp.einsum('nbqhd,nbkhd->nbhqk', qb, kw, preferred_element_type=jnp.float32) * (HEAD_DIM ** -0.5)
    qi = jnp.arange(band)[:, None] + band
    kj = jnp.arange(2 * band)[None, :]
    dist = qi - kj
    blk = jnp.arange(nb)[:, None, None]
    mask = (dist >= 0) & (dist <= band) & (blk * band - band + kj >= 0)
    sc = jnp.where(mask[None, :, None], sc, -jnp.inf)
    m = jnp.max(sc, axis=-1, keepdims=True)
    p = jnp.exp(sc - m)
    l = jnp.sum(p, axis=-1, keepdims=True)
    o = jnp.einsum('nbhqk,nbkhd->nbhqd', p, vw.astype(jnp.float32)) / l
    lse = (m + jnp.log(l))[..., 0]
    o = o.transpose(0, 1, 3, 2, 4).reshape(n, nb * band, h, dh)[:, :L]
    lse = lse.transpose(0, 1, 3, 2).reshape(n, nb * band, h)[:, :L]
    return o, lse


def to_strided(t, dil):
    b, s, h, dh = t.shape
    return t.reshape(b, s // dil, dil, h, dh).transpose(0, 2, 1, 3, 4).reshape(b * dil, s // dil, h, dh)


def from_strided(t, b, dil):
    rest = t.shape[2:]
    L = t.shape[1]
    t = t.reshape((b, dil, L) + rest)
    t = jnp.moveaxis(t, 1, 2)
    return t.reshape((b, L * dil) + rest)


def dilated_attention(q, k, v):
    b = q.shape[0]
    outs, lses = [], []
    for g, (window, dil) in enumerate(DIL_CONFIGS):
        hs = slice(g * DIL_HEADS_PER_GROUP, (g + 1) * DIL_HEADS_PER_GROUP)
        o, lse = banded_causal_attention(to_strided(q[:, :, hs], dil), to_strided(k[:, :, hs], dil),
                                         to_strided(v[:, :, hs], dil), window // dil)
        outs.append(from_strided(o, b, dil))
        lses.append(from_strided(lse, b, dil))
    w = jax.nn.softmax(jnp.stack(lses, axis=0), axis=0)
    return jnp.sum(w[..., None] * jnp.stack(outs, axis=0), axis=0)


def moba_attention(q, k, v):
    b, s, h, dh = q.shape
    q, k, v = (t.transpose(0, 2, 1, 3) for t in (q, k, v))
    nblk = -(-s // MOBA_BLOCK)
    pad = nblk * MOBA_BLOCK - s
    kb = jnp.pad(k, ((0, 0), (0, 0), (0, pad), (0, 0))).reshape(b, h, nblk, MOBA_BLOCK, dh)
    vb = jnp.pad(v, ((0, 0), (0, 0), (0, pad), (0, 0))).reshape(b, h, nblk, MOBA_BLOCK, dh)
    kmean = jnp.mean(kb.astype(jnp.float32), axis=3)
    pos = jnp.arange(s)
    own = pos // MOBA_BLOCK
    gate = jnp.einsum('bhsd,bhnd->bhsn', q.astype(jnp.float32), kmean)
    fully_past = jnp.arange(nblk)[None, :] < own[:, None]
    gate = jnp.where(fully_past, gate, -jnp.inf)
    top_val, top_idx = lax.top_k(gate, min(MOBA_TOPK, nblk))
    idx = jnp.concatenate([top_idx.astype(jnp.int32),
                           jnp.broadcast_to(own[None, None, :, None], (b, h, s, 1)).astype(jnp.int32)], axis=-1)
    valid = jnp.concatenate([jnp.isfinite(top_val), jnp.ones((b, h, s, 1), dtype=bool)], axis=-1)
    nc = s // MOBA_Q_CHUNK

    def chunks(t):
        return jnp.moveaxis(t.reshape((b, h, nc, MOBA_Q_CHUNK) + t.shape[3:]), 2, 0)

    bi = jnp.arange(b)[:, None, None, None]
    hi = jnp.arange(h)[None, :, None, None]
    offs = jnp.arange(MOBA_BLOCK)

    def attend(args):
        qc, ic, vc, pc = args
        kg = kb[bi, hi, ic]
        vg = vb[bi, hi, ic]
        sc = jnp.einsum('bhqd,bhqjnd->bhqjn', qc, kg, preferred_element_type=jnp.float32) * (HEAD_DIM ** -0.5)
        kpos = ic[..., None] * MOBA_BLOCK + offs
        mask = vc[..., None] & (kpos <= pc[None, None, :, None, None])
        sc = jnp.where(mask, sc, -jnp.inf)
        shp = sc.shape
        p = jax.nn.softmax(sc.reshape(shp[:3] + (-1,)), axis=-1).reshape(shp)
        return jnp.einsum('bhqjn,bhqjnd->bhqd', p, vg.astype(jnp.float32))

    out = lax.map(attend, (chunks(q), chunks(idx), chunks(valid), pos.reshape(nc, MOBA_Q_CHUNK)))
    out = jnp.moveaxis(out, 0, 2).reshape(b, h, s, dh)
    return out.transpose(0, 2, 1, 3)


def short_conv_mixer(bg, cg, xz, w):
    z = cg * xz
    s = z.shape[1]
    zp = jnp.pad(z, ((0, 0), (CONV_TAPS - 1, 0), (0, 0)))
    y = sum(w[t] * zp[:, t:t + s] for t in range(CONV_TAPS))
    return bg * y


def moe_swiglu(h, router_w, wg, wu, wd):
    logits = jnp.einsum('bsd,de->bse', h, router_w, preferred_element_type=jnp.float32)
    top_val, top_idx = lax.top_k(logits, TOP_K)
    gates = jax.nn.softmax(top_val, axis=-1)
    gate = jnp.einsum('bske,bsk->bse', jax.nn.one_hot(top_idx, N_EXPERTS, dtype=jnp.float32), gates).astype(h.dtype)
    out = jnp.zeros_like(h)
    for e in range(N_EXPERTS):
        out = out + gate[..., e:e + 1] * swiglu(h, wg[e], wu[e], wd[e])
    return out


def setup_inputs(seed: int = 0) -> dict:
    key = jax.random.key(seed)
    ks = jax.random.split(key, 22)

    def nrm(k, shape, scale):
        return jax.random.normal(k, shape, jnp.float32) * scale

    def gain(k, shape):
        return 1.0 + 0.02 * jax.random.normal(k, shape, jnp.float32)

    ne, no = N_EVEN, N_ODD
    return {
        'x': nrm(ks[0], (BATCH, SEQ, D_MODEL), 1.0),
        'norm_mix': gain(ks[1], (DEPTH, D_MODEL)),
        'norm_ffn': gain(ks[2], (DEPTH, D_MODEL)),
        'w_in_ab': nrm(ks[3], (ne, D_MODEL, EVEN_IN), D_MODEL ** -0.5),
        'gmlp_v_gain': gain(ks[4], (ne, GMLP_WIDTH)),
        'gmlp_ws': nrm(ks[5], (ne, GMLP_GROUPS, GMLP_CHUNK, GMLP_CHUNK), GMLP_CHUNK ** -0.5),
        'gmlp_bs': 1.0 + 0.1 * jax.random.normal(ks[6], (ne, GMLP_GROUPS, GMLP_CHUNK), jnp.float32),
        'dil_q_gain': gain(ks[7], (ne, HEAD_DIM)),
        'dil_k_gain': gain(ks[8], (ne, HEAD_DIM)),
        'w_out_ab': nrm(ks[9], (ne, EVEN_OUT, D_MODEL), EVEN_OUT ** -0.5),
        'ffn_w_gate': nrm(ks[10], (ne, D_MODEL, D_FF), D_MODEL ** -0.5),
        'ffn_w_up': nrm(ks[11], (ne, D_MODEL, D_FF), D_MODEL ** -0.5),
        'ffn_w_down': nrm(ks[12], (ne, D_FF, D_MODEL), D_FF ** -0.5),
        'w_in_cd': nrm(ks[13], (no, D_MODEL, ODD_IN), D_MODEL ** -0.5),
        'moba_q_gain': gain(ks[14], (no, HEAD_DIM)),
        'moba_k_gain': gain(ks[15], (no, HEAD_DIM)),
        'conv_w': nrm(ks[16], (no, CONV_TAPS, CONV_WIDTH), CONV_TAPS ** -0.5),
        'w_out_cd': nrm(ks[17], (no, ODD_OUT, D_MODEL), ODD_OUT ** -0.5),
        'router_w': nrm(ks[18], (no, D_MODEL, N_EXPERTS), D_MODEL ** -0.5),
        'moe_w_gate': nrm(ks[19], (no, N_EXPERTS, D_MODEL, D_FF_EXPERT), D_MODEL ** -0.5),
        'moe_w_up': nrm(ks[20], (no, N_EXPERTS, D_MODEL, D_FF_EXPERT), D_MODEL ** -0.5),
        'moe_w_down': nrm(ks[21], (no, N_EXPERTS, D_FF_EXPERT, D_MODEL), D_FF_EXPERT ** -0.5),
    }


def reference(x, norm_mix, norm_ffn, w_in_ab, gmlp_v_gain, gmlp_ws, gmlp_bs, dil_q_gain, dil_k_gain,
              w_out_ab, ffn_w_gate, ffn_w_up, ffn_w_down, w_in_cd, moba_q_gain, moba_k_gain, conv_w,
              w_out_cd, router_w, moe_w_gate, moe_w_up, moe_w_down):
    b, s, _ = x.shape
    cos, sin = rope_tables(s)
    for layer in range(DEPTH):
        i = layer // 2
        h = rms_norm(x, norm_mix[layer])
        if layer % 2 == 0:
            p = h @ w_in_ab[i]
            o1 = 2 * GMLP_WIDTH
            o2 = o1 + DIL_WIDTH
            o3 = o2 + DIL_WIDTH
            a_out = gmlp_spatial_gating(p[..., :o1], gmlp_v_gain[i], gmlp_ws[i], gmlp_bs[i])
            q = p[..., o1:o2].reshape(b, s, DIL_HEADS, HEAD_DIM)
            k = p[..., o2:o3].reshape(b, s, DIL_HEADS, HEAD_DIM)
            v = p[..., o3:].reshape(b, s, DIL_HEADS, HEAD_DIM)
            q, k = prep_qk(q, k, dil_q_gain[i], dil_k_gain[i], cos, sin)
            b_out = dilated_attention(q, k, v).reshape(b, s, DIL_OUT_WIDTH).astype(h.dtype)
            x = x + jnp.concatenate([a_out, b_out], axis=-1) @ w_out_ab[i]
            x = x + swiglu(rms_norm(x, norm_ffn[layer]), ffn_w_gate[i], ffn_w_up[i], ffn_w_down[i])
        else:
            p = h @ w_in_cd[i]
            o1 = MOBA_WIDTH
            o2 = 2 * MOBA_WIDTH
            o3 = 3 * MOBA_WIDTH
            o4 = o3 + CONV_WIDTH
            o5 = o4 + CONV_WIDTH
            q = p[..., :o1].reshape(b, s, MOBA_HEADS, HEAD_DIM)
            k = p[..., o1:o2].reshape(b, s, MOBA_HEADS, HEAD_DIM)
            v = p[..., o2:o3].reshape(b, s, MOBA_HEADS, HEAD_DIM)
            q, k = prep_qk(q, k, moba_q_gain[i], moba_k_gain[i], cos, sin)
            c_out = moba_attention(q, k, v).reshape(b, s, MOBA_WIDTH).astype(h.dtype)
            d_out = short_conv_mixer(p[..., o3:o4], p[..., o4:o5], p[..., o5:], conv_w[i])
            x = x + jnp.concatenate([c_out, d_out], axis=-1) @ w_out_cd[i]
            x = x + moe_swiglu(rms_norm(x, norm_ffn[layer]), router_w[i], moe_w_gate[i], moe_w_up[i], moe_w_down[i])
    return x
```

```python
import functools

import jax
import jax.numpy as jnp
from jax import lax
from jax.experimental import pallas as pl
from jax.experimental.pallas import tpu as pltpu

F32 = jnp.float32
BF16 = jnp.bfloat16
I32 = jnp.int32

D_MODEL = 1024
HEAD_DIM = 64
ROPE_THETA = 10000.0
NORM_EPS = 1e-6
LANE_TILE = 256
HEADS_PER_TILE = LANE_TILE // HEAD_DIM
GMLP_WIDTH = 256
GMLP_CHUNK = 128
DIL_CONFIGS = ((128, 1), (512, 4), (2048, 16))
DIL_WIDTH = 768
BAND = 128
MOBA_WIDTH = 768
MOBA_BLOCK = 256
MOBA_TOPK = 3
CONV_WIDTH = 256
N_EXPERTS = 8
ROUTE_LANES = 128
NEG_INF = float("-inf")

_NT = (((1,), (1,)), ((), ()))


def _params(*sem):
    return pltpu.CompilerParams(dimension_semantics=tuple(sem), vmem_limit_bytes=56 * 1024 * 1024)


def _lane_ids():
    return lax.broadcasted_iota(I32, (1, LANE_TILE), 1)


def _head_block_diag():
    r = lax.broadcasted_iota(I32, (LANE_TILE, LANE_TILE), 0) // HEAD_DIM
    c = lax.broadcasted_iota(I32, (LANE_TILE, LANE_TILE), 1) // HEAD_DIM
    return jnp.where(r == c, 1.0, 0.0).astype(BF16)


def _head_mean_sq(x, bd):
    s = x * x
    hi = s.astype(BF16)
    lo = (s - hi.astype(F32)).astype(BF16)
    tot = jnp.dot(hi, bd, preferred_element_type=F32) + jnp.dot(lo, bd, preferred_element_type=F32)
    return tot * (1.0 / HEAD_DIM)


def _head_norm(x, gain, bd):
    return x * lax.rsqrt(_head_mean_sq(x, bd) + NORM_EPS) * gain


def _rope(x, cos, sin, lane):
    half = HEAD_DIM // 2
    x_up = pltpu.roll(x, LANE_TILE - half, 1)
    x_dn = pltpu.roll(x, half, 1)
    rot = jnp.where((lane % HEAD_DIM) < half, -x_up, x_dn)
    return x * cos + rot * sin


def _row_norm(x, gain):
    return x * lax.rsqrt(jnp.mean(x * x, axis=-1, keepdims=True) + NORM_EPS) * gain


def _norm_proj_body(x_ref, g_ref, w_ref, o_ref, *, tn):
    hb = _row_norm(x_ref[...], g_ref[...]).astype(BF16)
    for c in range(o_ref.shape[-1] // tn):
        sl = slice(c * tn, (c + 1) * tn)
        o_ref[:, sl] = jnp.dot(hb, w_ref[:, sl], preferred_element_type=F32).astype(o_ref.dtype)


def _norm_proj(x2, gain, w_bf16, *, tm=512, tn=256):
    t, d = x2.shape
    n = w_bf16.shape[1]
    return pl.pallas_call(
        functools.partial(_norm_proj_body, tn=tn),
        out_shape=jax.ShapeDtypeStruct((t, n), BF16),
        grid=(t // tm,),
        in_specs=[
            pl.BlockSpec((tm, d), lambda i: (i, 0)),
            pl.BlockSpec((1, d), lambda i: (0, 0)),
            pl.BlockSpec((d, n), lambda i: (0, 0)),
        ],
        out_specs=pl.BlockSpec((tm, n), lambda i: (i, 0)),
        compiler_params=_params("parallel"),
        name="norm_proj",
    )(x2, gain.reshape(1, d), w_bf16)


def _gmlp_body(u_ref, v_ref, gain_ref, ws_ref, bias_ref, o_ref):
    lane = _lane_ids()
    bd = _head_block_diag()
    r = lax.broadcasted_iota(I32, (GMLP_CHUNK, GMLP_CHUNK), 0)
    c = lax.broadcasted_iota(I32, (GMLP_CHUNK, GMLP_CHUNK), 1)
    w_tril = [jnp.where(r >= c, ws_ref[g], 0.0).astype(BF16) for g in range(HEADS_PER_TILE)]
    for ch in range(o_ref.shape[0] // GMLP_CHUNK):
        sl = slice(ch * GMLP_CHUNK, (ch + 1) * GMLP_CHUNK)
        u = jax.nn.gelu(u_ref[sl, :].astype(F32))
        v = jax.nn.gelu(v_ref[sl, :].astype(F32))
        vn = _head_norm(v, gain_ref[...], bd).astype(BF16)
        y = bias_ref[...]
        for g in range(HEADS_PER_TILE):
            yg = jnp.dot(w_tril[g], vn, preferred_element_type=F32)
            y = y + jnp.where(lane // HEAD_DIM == g, yg, 0.0)
        o_ref[sl, :] = (u * y).astype(o_ref.dtype)


def _gmlp(p3, v_gain, ws, bs, *, tc=512):
    b, s, _ = p3.shape
    groups, chunk = bs.shape
    bias = jnp.repeat(bs.T, HEAD_DIM, axis=1)
    return pl.pallas_call(
        _gmlp_body,
        out_shape=jax.ShapeDtypeStruct((b, s, GMLP_WIDTH), BF16),
        grid=(b, s // tc),
        in_specs=[
            pl.BlockSpec((None, tc, LANE_TILE), lambda bi, i: (bi, i, 0)),
            pl.BlockSpec((None, tc, LANE_TILE), lambda bi, i: (bi, i, 1)),
            pl.BlockSpec((1, GMLP_WIDTH), lambda bi, i: (0, 0)),
            pl.BlockSpec((groups, chunk, chunk), lambda bi, i: (0, 0, 0)),
            pl.BlockSpec((chunk, GMLP_WIDTH), lambda bi, i: (0, 0)),
        ],
        out_specs=pl.BlockSpec((None, tc, GMLP_WIDTH), lambda bi, i: (bi, i, 0)),
        compiler_params=_params("parallel", "parallel"),
        name="gmlp_gate",
    )(p3, p3, v_gain.reshape(1, GMLP_WIDTH), ws, bias)


def _band_attn_body(q_ref, k_ref, kp_ref, v_ref, vp_ref, cos_ref, sin_ref, cosp_ref, sinp_ref,
                    gq_ref, gk_ref, o_ref, lse_ref):
    i = pl.program_id(2)
    lb = q_ref.shape[0]
    lane = _lane_ids()
    bd = _head_block_diag()
    k_all = jnp.concatenate([kp_ref[...], k_ref[...]], axis=0).astype(F32)
    cos_all = jnp.concatenate([cosp_ref[...], cos_ref[...]], axis=0)
    sin_all = jnp.concatenate([sinp_ref[...], sin_ref[...]], axis=0)
    kn = _rope(_head_norm(k_all, gk_ref[...], bd), cos_all, sin_all, lane).astype(BF16)
    v_all = jnp.concatenate([vp_ref[...], v_ref[...]], axis=0)
    qn = _rope(_head_norm(q_ref[...].astype(F32), gq_ref[...], bd), cos_ref[...], sin_ref[...], lane)
    qn = (qn * (HEAD_DIM ** -0.5)).astype(BF16)
    qi = lax.broadcasted_iota(I32, (BAND, 2 * BAND), 0) + BAND
    kj = lax.broadcasted_iota(I32, (BAND, 2 * BAND), 1)
    in_band = (qi - kj >= 0) & (qi - kj <= BAND)
    first_ok = in_band & ((kj >= BAND) | (i > 0))
    for j in range(lb // BAND):
        qj = qn[j * BAND:(j + 1) * BAND]
        kw = kn[j * BAND:(j + 2) * BAND]
        vw = v_all[j * BAND:(j + 2) * BAND]
        valid = first_ok if j == 0 else in_band
        o = jnp.zeros((BAND, LANE_TILE), F32)
        lse = jnp.zeros((BAND, LANE_TILE), F32)
        for h in range(HEADS_PER_TILE):
            hm = lane // HEAD_DIM == h
            kh = jnp.where(hm, kw, jnp.zeros_like(kw))
            s = lax.dot_general(qj, kh, _NT, preferred_element_type=F32)
            s = jnp.where(valid, s, NEG_INF)
            m = jnp.max(s, axis=-1, keepdims=True)
            p = jnp.exp(s - m)
            l = jnp.sum(p, axis=-1, keepdims=True)
            oh = jnp.dot(p.astype(BF16), vw, preferred_element_type=F32)
            o = jnp.where(hm, oh / l, o)
            lse = jnp.where(hm, m + jnp.log(l), lse)
        o_ref[j * BAND:(j + 1) * BAND, :] = o
        lse_ref[j * BAND:(j + 1) * BAND, :] = lse


def _band_attn(p3, cos_t, sin_t, gq, gk, *, group, dil, n_cols):
    b, s, _ = p3.shape
    length = s // dil
    lb = min(length, 512)
    pv = p3.reshape(b, length, dil * n_cols * LANE_TILE)
    cs = cos_t.reshape(length, dil * LANE_TILE)
    sn = sin_t.reshape(length, dil * LANE_TILE)
    qc, kc, vc = 2 + group, 5 + group, 8 + group
    sub = lb // BAND

    def cur(col):
        return pl.BlockSpec((None, lb, LANE_TILE), lambda bi, r, i: (bi, i, r * n_cols + col))

    def prev(col):
        return pl.BlockSpec((None, BAND, LANE_TILE),
                            lambda bi, r, i: (bi, jnp.maximum(i * sub - 1, 0), r * n_cols + col))

    tab_cur = pl.BlockSpec((lb, LANE_TILE), lambda bi, r, i: (i, r))
    tab_prev = pl.BlockSpec((BAND, LANE_TILE), lambda bi, r, i: (jnp.maximum(i * sub - 1, 0), r))
    gain = pl.BlockSpec((1, LANE_TILE), lambda bi, r, i: (0, 0))
    out = pl.BlockSpec((None, lb, LANE_TILE), lambda bi, r, i: (bi, i, r))
    o, lse = pl.pallas_call(
        _band_attn_body,
        out_shape=[jax.ShapeDtypeStruct((b, length, dil * LANE_TILE), F32)] * 2,
        grid=(b, dil, length // lb),
        in_specs=[cur(qc), cur(kc), prev(kc), cur(vc), prev(vc), tab_cur, tab_cur, tab_prev, tab_prev,
                  gain, gain],
        out_specs=[out, out],
        compiler_params=_params("parallel", "parallel", "arbitrary"),
        name=f"band_attn_d{dil}",
    )(pv, pv, pv, pv, pv, cs, sn, cs, sn, gq, gk)
    return o.reshape(b, s, LANE_TILE), lse.reshape(b, s, LANE_TILE)


def _even_out_body(x_ref, a_ref, o0, o1, o2, l0, l1, l2, wa_ref, wb_ref, out_ref):
    m = jnp.maximum(jnp.maximum(l0[...], l1[...]), l2[...])
    e0 = jnp.exp(l0[...] - m)
    e1 = jnp.exp(l1[...] - m)
    e2 = jnp.exp(l2[...] - m)
    merged = (e0 * o0[...] + e1 * o1[...] + e2 * o2[...]) / (e0 + e1 + e2)
    acc = x_ref[...] + jnp.dot(a_ref[...], wa_ref[...], preferred_element_type=F32)
    out_ref[...] = acc + jnp.dot(merged.astype(BF16), wb_ref[...], preferred_element_type=F32)


def _even_out(x2, a2, outs, lses, wa, wb, *, tm=512):
    t, d = x2.shape
    row = lambda w: pl.BlockSpec((tm, w), lambda i: (i, 0))
    full = lambda a: pl.BlockSpec(a.shape, lambda i: (0, 0))
    return pl.pallas_call(
        _even_out_body,
        out_shape=jax.ShapeDtypeStruct((t, d), F32),
        grid=(t // tm,),
        in_specs=[row(d), row(GMLP_WIDTH)] + [row(LANE_TILE)] * 6 + [full(wa), full(wb)],
        out_specs=row(d),
        compiler_params=_params("parallel"),
        name="even_out_proj",
    )(x2, a2, *outs, *lses, wa, wb)


def _swiglu_step(h_bf16, wg, wu, wd):
    g = jnp.dot(h_bf16, wg, preferred_element_type=F32)
    u = jnp.dot(h_bf16, wu, preferred_element_type=F32)
    a = (g * jax.nn.sigmoid(g) * u).astype(BF16)
    return jnp.dot(a, wd, preferred_element_type=F32)


def _ffn_body(x_ref, g_ref, wg_ref, wu_ref, wd_ref, o_ref, h_s, acc_s):
    f = pl.program_id(1)

    @pl.when(f == 0)
    def _():
        x = x_ref[...]
        h_s[...] = _row_norm(x, g_ref[...]).astype(BF16)
        acc_s[...] = x

    acc_s[...] += _swiglu_step(h_s[...], wg_ref[...], wu_ref[...], wd_ref[...])

    @pl.when(f == pl.num_programs(1) - 1)
    def _():
        o_ref[...] = acc_s[...]


def _ffn(x2, gain, wg, wu, wd, *, tm=512, tf=1408):
    t, d = x2.shape
    ff = wg.shape[1]
    return pl.pallas_call(
        _ffn_body,
        out_shape=jax.ShapeDtypeStruct((t, d), F32),
        grid=(t // tm, ff // tf),
        in_specs=[
            pl.BlockSpec((tm, d), lambda i, f: (i, 0)),
            pl.BlockSpec((1, d), lambda i, f: (0, 0)),
            pl.BlockSpec((d, tf), lambda i, f: (0, f)),
            pl.BlockSpec((d, tf), lambda i, f: (0, f)),
            pl.BlockSpec((tf, d), lambda i, f: (f, 0)),
        ],
        out_specs=pl.BlockSpec((tm, d), lambda i, f: (i, 0)),
        scratch_shapes=[pltpu.VMEM((tm, d), BF16), pltpu.VMEM((tm, d), F32)],
        compiler_params=_params("parallel", "arbitrary"),
        name="ffn_swiglu",
    )(x2, gain.reshape(1, d), wg, wu, wd)


def _expert_body(te_ref, nv_ref, x_ref, wg_ref, wu_ref, wd_ref, o_ref, h_s, acc_s):
    i = pl.program_id(0)
    f = pl.program_id(1)
    live = i < nv_ref[0]

    @pl.when(f == 0)
    def _():
        h_s[...] = x_ref[...].astype(BF16)
        acc_s[...] = jnp.zeros_like(acc_s)

    @pl.when(live)
    def _():
        acc_s[...] += _swiglu_step(h_s[...], wg_ref[...], wu_ref[...], wd_ref[...])

    @pl.when(f == pl.num_programs(1) - 1)
    def _():
        o_ref[...] = acc_s[...]


def _expert_ffn(xs, tile_expert, n_live, wg, wu, wd, *, tm, tf=512):
    p, d = xs.shape
    ff = wg.shape[2]
    nf = ff // tf

    def w_in(i, f, te, nv):
        return (te[i], 0, jnp.where(i < nv[0], f, nf - 1))

    def w_out(i, f, te, nv):
        return (te[i], jnp.where(i < nv[0], f, nf - 1), 0)

    return pl.pallas_call(
        _expert_body,
        out_shape=jax.ShapeDtypeStruct((p, d), F32),
        grid_spec=pltpu.PrefetchScalarGridSpec(
            num_scalar_prefetch=2,
            grid=(p // tm, nf),
            in_specs=[
                pl.BlockSpec((tm, d), lambda i, f, te, nv: (i, 0)),
                pl.BlockSpec((None, d, tf), w_in),
                pl.BlockSpec((None, d, tf), w_in),
                pl.BlockSpec((None, tf, d), w_out),
            ],
            out_specs=pl.BlockSpec((tm, d), lambda i, f, te, nv: (i, 0)),
            scratch_shapes=[pltpu.VMEM((tm, d), BF16), pltpu.VMEM((tm, d), F32)],
        ),
        compiler_params=_params("arbitrary", "arbitrary"),
        name="expert_swiglu",
    )(tile_expert, n_live, xs, wg, wu, wd)


def _moba_body(q_ref, k_ref, v_ref, cos_ref, sin_ref, gq_ref, gk_ref, o_ref,
               kp_s, vt_s, kmean_s, sel_s, out_s):
    i = pl.program_id(2)
    nblk = kmean_s.shape[0]
    lane = _lane_ids()
    bd = _head_block_diag()
    cos = cos_ref[...]
    sin = sin_ref[...]

    @pl.when(i == 0)
    def _():
        kmean_s[...] = jnp.zeros_like(kmean_s)

    kc = _rope(_head_norm(k_ref[...].astype(F32), gk_ref[...], bd), cos, sin, lane)
    kp_s[i] = kc.astype(BF16)
    kmean_s[pl.ds(i, 1), :] = jnp.mean(kc, axis=0, keepdims=True)
    vt_s[i] = v_ref[...].astype(F32).T.astype(BF16)

    qn = _rope(_head_norm(q_ref[...].astype(F32), gq_ref[...], bd), cos, sin, lane)
    qn = (qn * (HEAD_DIM ** -0.5)).astype(BF16)
    blk = lax.broadcasted_iota(I32, (nblk, MOBA_BLOCK), 0).astype(F32)
    first_blk = i.astype(F32)
    km = kmean_s[...]
    krow = lax.broadcasted_iota(I32, (MOBA_BLOCK, MOBA_BLOCK), 0)
    qcol = lax.broadcasted_iota(I32, (MOBA_BLOCK, MOBA_BLOCK), 1)
    causal = krow <= qcol

    for h in range(HEADS_PER_TILE):
        hm = lane // HEAD_DIM == h
        qh = jnp.where(hm, qn, jnp.zeros_like(qn))
        rows = slice(h * HEAD_DIM, (h + 1) * HEAD_DIM)

        km_hi = km.astype(BF16)
        km_lo = (km - km_hi.astype(F32)).astype(BF16)
        gate = (lax.dot_general(km_hi, qh, _NT, preferred_element_type=F32)
                + lax.dot_general(km_lo, qh, _NT, preferred_element_type=F32))
        gate = jnp.where(blk < first_blk, gate, NEG_INF)
        sel = jnp.zeros((nblk, MOBA_BLOCK), F32)
        for _ in range(MOBA_TOPK):
            top = jnp.max(gate, axis=0, keepdims=True)
            idx = jnp.min(jnp.where(gate == top, blk, float(nblk)), axis=0, keepdims=True)
            hit = blk == idx
            sel = jnp.where(hit & (top > NEG_INF), 1.0, sel)
            gate = jnp.where(hit, NEG_INF, gate)
        sel_s[h] = sel

        s = lax.dot_general(kp_s[i], qh, _NT, preferred_element_type=F32)
        s = jnp.where(causal, s, NEG_INF)
        m0 = jnp.max(s, axis=0, keepdims=True)
        p = jnp.exp(s - m0)
        l0 = jnp.sum(p, axis=0, keepdims=True)
        acc0 = jnp.dot(vt_s[i, rows, :], p.astype(BF16), preferred_element_type=F32)

        def past_block(j, carry, qh=qh, rows=rows, h=h):
            m, l, acc = carry
            s = lax.dot_general(kp_s[j], qh, _NT, preferred_element_type=F32)
            s = jnp.where(sel_s[h, pl.ds(j, 1), :] > 0.0, s, NEG_INF)
            m_new = jnp.maximum(m, jnp.max(s, axis=0, keepdims=True))
            alpha = jnp.exp(m - m_new)
            p = jnp.exp(s - m_new)
            l = alpha * l + jnp.sum(p, axis=0, keepdims=True)
            acc = alpha * acc + jnp.dot(vt_s[j, rows, :], p.astype(BF16), preferred_element_type=F32)
            return m_new, l, acc

        _, l, acc = lax.fori_loop(0, i, past_block, (m0, l0, acc0))
        out_s[rows, :] = acc / l

    o_ref[...] = out_s[...].T.astype(o_ref.dtype)


def _moba(p3, cos_t, sin_t, gq, gk):
    b, s, _ = p3.shape
    nblk = s // MOBA_BLOCK
    tiles = MOBA_WIDTH // LANE_TILE

    def col(off):
        return pl.BlockSpec((None, MOBA_BLOCK, LANE_TILE), lambda bi, hg, i: (bi, i, off + hg))

    tab = pl.BlockSpec((MOBA_BLOCK, LANE_TILE), lambda bi, hg, i: (i, 0))
    gain = pl.BlockSpec((1, LANE_TILE), lambda bi, hg, i: (0, 0))
    return pl.pallas_call(
        _moba_body,
        out_shape=jax.ShapeDtypeStruct((b, s, MOBA_WIDTH), BF16),
        grid=(b, tiles, nblk),
        in_specs=[col(0), col(tiles), col(2 * tiles), tab, tab, gain, gain],
        out_specs=pl.BlockSpec((None, MOBA_BLOCK, LANE_TILE), lambda bi, hg, i: (bi, i, hg)),
        scratch_shapes=[
            pltpu.VMEM((nblk, MOBA_BLOCK, LANE_TILE), BF16),
            pltpu.VMEM((nblk, LANE_TILE, MOBA_BLOCK), BF16),
            pltpu.VMEM((nblk, LANE_TILE), F32),
            pltpu.VMEM((HEADS_PER_TILE, nblk, MOBA_BLOCK), F32),
            pltpu.VMEM((LANE_TILE, MOBA_BLOCK), F32),
        ],
        compiler_params=_params("arbitrary", "arbitrary", "arbitrary"),
        name="moba_attn",
    )(p3, p3, p3, cos_t, sin_t, gq, gk)


def _conv_body(bg_ref, cg_ref, xz_ref, w_ref, o_ref, *, ts):
    halo = 16
    w = w_ref[...]
    for c in range(o_ref.shape[0] // ts):
        lo = c * ts
        if c == 0:
            z = cg_ref[0:ts, :].astype(F32) * xz_ref[0:ts, :].astype(F32)
            zp = jnp.concatenate([jnp.zeros((halo, CONV_WIDTH), F32), z], axis=0)
        else:
            zp = cg_ref[lo - halo:lo + ts, :].astype(F32) * xz_ref[lo - halo:lo + ts, :].astype(F32)
        z1 = pltpu.roll(zp, 1, 0)[halo:]
        z2 = pltpu.roll(zp, 2, 0)[halo:]
        y = w[2:3, :] * zp[halo:] + w[1:2, :] * z1 + w[0:1, :] * z2
        o_ref[lo:lo + ts, :] = (bg_ref[lo:lo + ts, :].astype(F32) * y).astype(o_ref.dtype)


def _short_conv(p3, conv_w, *, col0, ts=512):
    b, s, _ = p3.shape
    taps = conv_w.shape[0]

    def col(c):
        return pl.BlockSpec((None, s, CONV_WIDTH), lambda bi: (bi, 0, col0 + c))

    return pl.pallas_call(
        functools.partial(_conv_body, ts=ts),
        out_shape=jax.ShapeDtypeStruct((b, s, CONV_WIDTH), BF16),
        grid=(b,),
        in_specs=[col(0), col(1), col(2), pl.BlockSpec((taps, CONV_WIDTH), lambda bi: (0, 0))],
        out_specs=pl.BlockSpec((None, s, CONV_WIDTH), lambda bi: (bi, 0, 0)),
        compiler_params=_params("parallel"),
        name="short_conv",
    )(p3, p3, p3, conv_w)


def _odd_out_body(x_ref, c_ref, d_ref, wc_ref, wd_ref, gain_ref, rw_ref, xo_ref, h_ref, route_ref):
    x = x_ref[...] + jnp.dot(c_ref[...], wc_ref[...], preferred_element_type=F32)
    x = x + jnp.dot(d_ref[...], wd_ref[...], preferred_element_type=F32)
    xo_ref[...] = x
    h = _row_norm(x, gain_ref[...])
    h_ref[...] = h
    logits = jnp.dot(h, rw_ref[...], preferred_element_type=F32, precision=lax.Precision.HIGHEST)
    lane = lax.broadcasted_iota(I32, logits.shape, 1)
    lane_f = lane.astype(F32)
    logits = jnp.where(lane < N_EXPERTS, logits, NEG_INF)
    v1 = jnp.max(logits, axis=-1, keepdims=True)
    i1 = jnp.min(jnp.where(logits == v1, lane_f, float(ROUTE_LANES)), axis=-1, keepdims=True)
    rest = jnp.where(lane_f == i1, NEG_INF, logits)
    v2 = jnp.max(rest, axis=-1, keepdims=True)
    i2 = jnp.min(jnp.where(rest == v2, lane_f, float(ROUTE_LANES)), axis=-1, keepdims=True)
    e = jnp.exp(v2 - v1)
    g1 = 1.0 / (1.0 + e)
    g2 = e / (1.0 + e)
    route = jnp.where(lane == 0, i1, 0.0)
    route = jnp.where(lane == 1, i2, route)
    route = jnp.where(lane == 2, g1, route)
    route_ref[...] = jnp.where(lane == 3, g2, route)


def _odd_out(x2, c2, d2, wc, wd, gain, router_pad, *, tm=512):
    t, d = x2.shape
    row = lambda w: pl.BlockSpec((tm, w), lambda i: (i, 0))
    full = lambda a: pl.BlockSpec(a.shape, lambda i: (0, 0))
    return pl.pallas_call(
        _odd_out_body,
        out_shape=[jax.ShapeDtypeStruct((t, d), F32), jax.ShapeDtypeStruct((t, d), F32),
                   jax.ShapeDtypeStruct((t, ROUTE_LANES), F32)],
        grid=(t // tm,),
        in_specs=[row(d), row(MOBA_WIDTH), row(CONV_WIDTH), full(wc), full(wd),
                  pl.BlockSpec((1, d), lambda i: (0, 0)), full(router_pad)],
        out_specs=[row(d), row(d), row(ROUTE_LANES)],
        compiler_params=_params("parallel"),
        name="odd_out_proj_router",
    )(x2, c2, d2, wc, wd, gain.reshape(1, d), router_pad)


def _row_copy(src_ref, src_row, dst_ref, dst_row, sem):
    return pltpu.make_async_copy(src_ref.at[pl.ds(src_row, 1)], dst_ref.at[pl.ds(dst_row, 1)], sem)


def _dispatch_body(pos_ref, h_ref, init_ref, xs_ref, sem):
    del init_ref
    td = h_ref.shape[0]

    def copies(r):
        return [_row_copy(h_ref, r, xs_ref, pos_ref[0, 0, k * td + r], sem) for k in range(2)]

    def start(r, c):
        for cp in copies(r):
            cp.start()
        return c

    def wait(r, c):
        for cp in copies(r):
            cp.wait()
        return c

    lax.fori_loop(0, td, start, 0)
    lax.fori_loop(0, td, wait, 0)


def _dispatch(h2, pos3, n_rows, *, td):
    t, d = h2.shape
    return pl.pallas_call(
        _dispatch_body,
        out_shape=jax.ShapeDtypeStruct((n_rows, d), F32),
        grid=(t // td,),
        in_specs=[
            pl.BlockSpec((1, 1, 2 * td), lambda i: (i, 0, 0), memory_space=pltpu.SMEM),
            pl.BlockSpec((td, d), lambda i: (i, 0)),
            pl.BlockSpec(memory_space=pl.ANY),
        ],
        out_specs=pl.BlockSpec(memory_space=pl.ANY),
        scratch_shapes=[pltpu.SemaphoreType.DMA(())],
        input_output_aliases={2: 0},
        compiler_params=_params("arbitrary"),
        name="expert_dispatch",
    )(pos3, h2, jnp.zeros((n_rows, d), F32))


def _combine_body(pos_ref, x_ref, route_ref, y_ref, o_ref, buf, sem):
    td = x_ref.shape[0]

    def copies(r):
        return [_row_copy(y_ref, pos_ref[0, 0, k * td + r], buf.at[k], r, sem) for k in range(2)]

    def start(r, c):
        for cp in copies(r):
            cp.start()
        return c

    def wait(r, c):
        for cp in copies(r):
            cp.wait()
        return c

    lax.fori_loop(0, td, start, 0)
    lax.fori_loop(0, td, wait, 0)
    route = route_ref[...]
    o_ref[...] = x_ref[...] + route[:, 2:3] * buf[0] + route[:, 3:4] * buf[1]


def _combine(x2, route, ys, pos3, *, td):
    t, d = x2.shape
    return pl.pallas_call(
        _combine_body,
        out_shape=jax.ShapeDtypeStruct((t, d), F32),
        grid=(t // td,),
        in_specs=[
            pl.BlockSpec((1, 1, 2 * td), lambda i: (i, 0, 0), memory_space=pltpu.SMEM),
            pl.BlockSpec((td, d), lambda i: (i, 0)),
            pl.BlockSpec((td, ROUTE_LANES), lambda i: (i, 0)),
            pl.BlockSpec(memory_space=pl.ANY),
        ],
        out_specs=pl.BlockSpec((td, d), lambda i: (i, 0)),
        scratch_shapes=[pltpu.VMEM((2, td, d), F32), pltpu.SemaphoreType.DMA(())],
        compiler_params=_params("arbitrary"),
        name="expert_combine",
    )(pos3, x2, route, ys)


def _route_plan(route, *, tm, td):
    t = route.shape[0]
    experts = jnp.concatenate([route[:, 0], route[:, 1]]).astype(I32)
    onehot = (experts[:, None] == jnp.arange(N_EXPERTS, dtype=I32)[None, :]).astype(I32)
    running = jnp.cumsum(onehot, axis=0)
    counts = running[-1]
    padded = ((counts + tm - 1) // tm) * tm
    ends = jnp.cumsum(padded)
    starts = ends - padded
    pos = jnp.sum(onehot * (starts[None, :] + running - 1), axis=1)
    n_tiles = (2 * t) // tm + N_EXPERTS
    tile_lo = jnp.arange(n_tiles, dtype=I32) * tm
    tile_expert = jnp.sum((tile_lo[:, None] >= ends[None, :]).astype(I32), axis=1)
    n_live = (ends[-1] // tm).astype(I32)
    last = tile_expert[jnp.maximum(n_live - 1, 0)]
    tile_expert = jnp.where(jnp.arange(n_tiles) < n_live, tile_expert, last)
    pos3 = pos.reshape(2, t // td, td).transpose(1, 0, 2).reshape(t // td, 1, 2 * td)
    return pos3, tile_expert, n_live.reshape(1), n_tiles * tm


def _rope_tables(seq):
    inv = 1.0 / (ROPE_THETA ** (jnp.arange(0, HEAD_DIM, 2, dtype=F32) / HEAD_DIM))
    ang = jnp.arange(seq, dtype=F32)[:, None] * inv[None, :]
    ang = jnp.concatenate([ang, ang], axis=-1)
    tile = lambda a: jnp.tile(a, (1, HEADS_PER_TILE))
    return tile(jnp.cos(ang)), tile(jnp.sin(ang))


def _tile_gain(g):
    return jnp.tile(g, HEADS_PER_TILE).reshape(1, LANE_TILE)


def _even_layer(x2, b, s, cos_t, sin_t, norm_mix, norm_ffn, w_in, v_gain, ws, bs, gq, gk, w_out, wg, wu, wd):
    p = _norm_proj(x2, norm_mix, w_in.astype(BF16))
    n_cols = p.shape[1] // LANE_TILE
    p3 = p.reshape(b, s, p.shape[1])
    a = _gmlp(p3, v_gain, ws, bs)
    outs, lses = [], []
    for group, (_, dil) in enumerate(DIL_CONFIGS):
        o, lse = _band_attn(p3, cos_t, sin_t, _tile_gain(gq), _tile_gain(gk), group=group, dil=dil, n_cols=n_cols)
        outs.append(o.reshape(b * s, LANE_TILE))
        lses.append(lse.reshape(b * s, LANE_TILE))
    w_out = w_out.astype(BF16)
    x2 = _even_out(x2, a.reshape(b * s, GMLP_WIDTH), outs, lses, w_out[:GMLP_WIDTH], w_out[GMLP_WIDTH:])
    return _ffn(x2, norm_ffn, wg.astype(BF16), wu.astype(BF16), wd.astype(BF16))


def _odd_layer(x2, b, s, cos_t, sin_t, norm_mix, norm_ffn, w_in, gq, gk, conv_w, w_out, router_w, wg, wu, wd,
               *, tm=512, td=256):
    p = _norm_proj(x2, norm_mix, w_in.astype(BF16))
    p3 = p.reshape(b, s, p.shape[1])
    c = _moba(p3, cos_t, sin_t, _tile_gain(gq), _tile_gain(gk))
    dconv = _short_conv(p3, conv_w, col0=3 * MOBA_WIDTH // CONV_WIDTH)
    w_out = w_out.astype(BF16)
    router_pad = jnp.pad(router_w, ((0, 0), (0, ROUTE_LANES - N_EXPERTS)))
    x2, h2, route = _odd_out(x2, c.reshape(b * s, MOBA_WIDTH), dconv.reshape(b * s, CONV_WIDTH),
                             w_out[:MOBA_WIDTH], w_out[MOBA_WIDTH:], norm_ffn, router_pad)
    pos3, tile_expert, n_live, n_rows = _route_plan(route, tm=tm, td=td)
    xs = _dispatch(h2, pos3, n_rows, td=td)
    ys = _expert_ffn(xs, tile_expert, n_live, wg.astype(BF16), wu.astype(BF16), wd.astype(BF16), tm=tm)
    return _combine(x2, route, ys, pos3, td=td)


def kernel(x, norm_mix, norm_ffn, w_in_ab, gmlp_v_gain, gmlp_ws, gmlp_bs, dil_q_gain, dil_k_gain, w_out_ab,
           ffn_w_gate, ffn_w_up, ffn_w_down, w_in_cd, moba_q_gain, moba_k_gain, conv_w, w_out_cd, router_w,
           moe_w_gate, moe_w_up, moe_w_down):
    b, s, d = x.shape
    depth = norm_mix.shape[0]
    cos_t, sin_t = _rope_tables(s)
    x2 = x.reshape(b * s, d)
    for layer in range(depth):
        i = layer // 2
        if layer % 2 == 0:
            x2 = _even_layer(x2, b, s, cos_t, sin_t, norm_mix[layer], norm_ffn[layer], w_in_ab[i], gmlp_v_gain[i],
                             gmlp_ws[i], gmlp_bs[i], dil_q_gain[i], dil_k_gain[i], w_out_ab[i], ffn_w_gate[i],
                             ffn_w_up[i], ffn_w_down[i])
        else:
            x2 = _odd_layer(x2, b, s, cos_t, sin_t, norm_mix[layer], norm_ffn[layer], w_in_cd[i], moba_q_gain[i],
                            moba_k_gain[i], conv_w[i], w_out_cd[i], router_w[i], moe_w_gate[i], moe_w_up[i],
                            moe_w_down[i])
    return x2.reshape(b, s, d)
```

```python
import functools

import jax
import jax.numpy as jnp
from jax import lax
from jax.experimental import pallas as pl
from jax.experimental.pallas import tpu as pltpu

F32 = jnp.float32
BF16 = jnp.bfloat16
I32 = jnp.int32

D_MODEL = 1024
HEAD_DIM = 64
ROPE_THETA = 10000.0
NORM_EPS = 1e-6
LANE_TILE = 256
HEADS_PER_TILE = LANE_TILE // HEAD_DIM
GMLP_WIDTH = 256
GMLP_CHUNK = 128
DIL_CONFIGS = ((128, 1), (512, 4), (2048, 16))
DIL_WIDTH = 768
BAND = 128
MOBA_WIDTH = 768
MOBA_BLOCK = 256
MOBA_TOPK = 3
MOBA_ACC_ROWS = HEAD_DIM + 16
LOG2_E = 1.4426950408889634
MOBA_M_INIT = -1e30
CONV_WIDTH = 256
N_EXPERTS = 8
ROUTE_LANES = 128
NEG_INF = float("-inf")
DMA_UNROLL = 8

_NT = (((1,), (1,)), ((), ()))


def _params(*sem):
    return pltpu.CompilerParams(dimension_semantics=tuple(sem), vmem_limit_bytes=56 * 1024 * 1024)


def _lane_ids():
    return lax.broadcasted_iota(I32, (1, LANE_TILE), 1)


def _head_block_diag():
    r = lax.broadcasted_iota(I32, (LANE_TILE, LANE_TILE), 0) // HEAD_DIM
    c = lax.broadcasted_iota(I32, (LANE_TILE, LANE_TILE), 1) // HEAD_DIM
    return jnp.where(r == c, 1.0, 0.0).astype(BF16)


def _head_mean_sq(x, bd):
    s = x * x
    hi = s.astype(BF16)
    lo = (s - hi.astype(F32)).astype(BF16)
    tot = jnp.dot(hi, bd, preferred_element_type=F32) + jnp.dot(lo, bd, preferred_element_type=F32)
    return tot * (1.0 / HEAD_DIM)


def _head_norm(x, gain, bd):
    return x * lax.rsqrt(_head_mean_sq(x, bd) + NORM_EPS) * gain


def _rope(x, cos, sin, lane):
    half = HEAD_DIM // 2
    x_up = pltpu.roll(x, LANE_TILE - half, 1)
    x_dn = pltpu.roll(x, half, 1)
    rot = jnp.where((lane % HEAD_DIM) < half, -x_up, x_dn)
    return x * cos + rot * sin


def _row_norm(x, gain):
    return x * lax.rsqrt(jnp.mean(x * x, axis=-1, keepdims=True) + NORM_EPS) * gain


def _norm_proj_body(x_ref, g_ref, w_ref, o_ref, *, tn):
    hb = _row_norm(x_ref[...], g_ref[...]).astype(BF16)
    for c in range(o_ref.shape[-1] // tn):
        sl = slice(c * tn, (c + 1) * tn)
        o_ref[:, sl] = jnp.dot(hb, w_ref[:, sl], preferred_element_type=F32).astype(o_ref.dtype)


def _norm_proj(x2, gain, w_bf16, *, tm=512, tn=256):
    t, d = x2.shape
    n = w_bf16.shape[1]
    return pl.pallas_call(
        functools.partial(_norm_proj_body, tn=tn),
        out_shape=jax.ShapeDtypeStruct((t, n), BF16),
        grid=(t // tm,),
        in_specs=[
            pl.BlockSpec((tm, d), lambda i: (i, 0)),
            pl.BlockSpec((1, d), lambda i: (0, 0)),
            pl.BlockSpec((d, n), lambda i: (0, 0)),
        ],
        out_specs=pl.BlockSpec((tm, n), lambda i: (i, 0)),
        compiler_params=_params("parallel"),
        name="norm_proj",
    )(x2, gain.reshape(1, d), w_bf16)


def _gmlp_body(u_ref, v_ref, gain_ref, ws_ref, bias_ref, o_ref):
    lane = _lane_ids()
    bd = _head_block_diag()
    r = lax.broadcasted_iota(I32, (GMLP_CHUNK, GMLP_CHUNK), 0)
    c = lax.broadcasted_iota(I32, (GMLP_CHUNK, GMLP_CHUNK), 1)
    w_tril = [jnp.where(r >= c, ws_ref[g], 0.0).astype(BF16) for g in range(HEADS_PER_TILE)]
    for ch in range(o_ref.shape[0] // GMLP_CHUNK):
        sl = slice(ch * GMLP_CHUNK, (ch + 1) * GMLP_CHUNK)
        u = jax.nn.gelu(u_ref[sl, :].astype(F32))
        v = jax.nn.gelu(v_ref[sl, :].astype(F32))
        vn = _head_norm(v, gain_ref[...], bd).astype(BF16)
        y = bias_ref[...]
        for g in range(HEADS_PER_TILE):
            yg = jnp.dot(w_tril[g], vn, preferred_element_type=F32)
            y = y + jnp.where(lane // HEAD_DIM == g, yg, 0.0)
        o_ref[sl, :] = (u * y).astype(o_ref.dtype)


def _gmlp(p3, v_gain, ws, bs, *, tc=512):
    b, s, _ = p3.shape
    groups, chunk = bs.shape
    bias = jnp.repeat(bs.T, HEAD_DIM, axis=1)
    return pl.pallas_call(
        _gmlp_body,
        out_shape=jax.ShapeDtypeStruct((b, s, GMLP_WIDTH), BF16),
        grid=(b, s // tc),
        in_specs=[
            pl.BlockSpec((None, tc, LANE_TILE), lambda bi, i: (bi, i, 0)),
            pl.BlockSpec((None, tc, LANE_TILE), lambda bi, i: (bi, i, 1)),
            pl.BlockSpec((1, GMLP_WIDTH), lambda bi, i: (0, 0)),
            pl.BlockSpec((groups, chunk, chunk), lambda bi, i: (0, 0, 0)),
            pl.BlockSpec((chunk, GMLP_WIDTH), lambda bi, i: (0, 0)),
        ],
        out_specs=pl.BlockSpec((None, tc, GMLP_WIDTH), lambda bi, i: (bi, i, 0)),
        compiler_params=_params("parallel", "parallel"),
        name="gmlp_gate",
    )(p3, p3, v_gain.reshape(1, GMLP_WIDTH), ws, bias)


def _band_attn_body(q_ref, k_ref, kp_ref, v_ref, vp_ref, cos_ref, sin_ref, cosp_ref, sinp_ref,
                    gq_ref, gk_ref, o_ref, lse_ref):
    i = pl.program_id(2)
    lb = q_ref.shape[0]
    lane = _lane_ids()
    bd = _head_block_diag()
    k_all = jnp.concatenate([kp_ref[...], k_ref[...]], axis=0).astype(F32)
    cos_all = jnp.concatenate([cosp_ref[...], cos_ref[...]], axis=0)
    sin_all = jnp.concatenate([sinp_ref[...], sin_ref[...]], axis=0)
    kn = _rope(_head_norm(k_all, gk_ref[...], bd), cos_all, sin_all, lane).astype(BF16)
    v_all = jnp.concatenate([vp_ref[...], v_ref[...]], axis=0)
    qn = _rope(_head_norm(q_ref[...].astype(F32), gq_ref[...], bd), cos_ref[...], sin_ref[...], lane)
    qn = (qn * (HEAD_DIM ** -0.5)).astype(BF16)
    qi = lax.broadcasted_iota(I32, (BAND, 2 * BAND), 0) + BAND
    kj = lax.broadcasted_iota(I32, (BAND, 2 * BAND), 1)
    in_band = (qi - kj >= 0) & (qi - kj <= BAND)
    first_ok = in_band & ((kj >= BAND) | (i > 0))
    for j in range(lb // BAND):
        qj = qn[j * BAND:(j + 1) * BAND]
        kw = kn[j * BAND:(j + 2) * BAND]
        vw = v_all[j * BAND:(j + 2) * BAND]
        valid = first_ok if j == 0 else in_band
        o = jnp.zeros((BAND, LANE_TILE), F32)
        lse = jnp.zeros((BAND, LANE_TILE), F32)
        for h in range(HEADS_PER_TILE):
            hm = lane // HEAD_DIM == h
            kh = jnp.where(hm, kw, jnp.zeros_like(kw))
            s = lax.dot_general(qj, kh, _NT, preferred_element_type=F32)
            s = jnp.where(valid, s, NEG_INF)
            m = jnp.max(s, axis=-1, keepdims=True)
            p = jnp.exp(s - m)
            l = jnp.sum(p, axis=-1, keepdims=True)
            oh = jnp.dot(p.astype(BF16), vw, preferred_element_type=F32)
            o = jnp.where(hm, oh / l, o)
            lse = jnp.where(hm, m + jnp.log(l), lse)
        o_ref[j * BAND:(j + 1) * BAND, :] = o
        lse_ref[j * BAND:(j + 1) * BAND, :] = lse


def _band_attn(p3, cos_t, sin_t, gq, gk, *, group, dil, n_cols):
    b, s, _ = p3.shape
    length = s // dil
    lb = min(length, 512)
    pv = p3.reshape(b, length, dil * n_cols * LANE_TILE)
    cs = cos_t.reshape(length, dil * LANE_TILE)
    sn = sin_t.reshape(length, dil * LANE_TILE)
    qc, kc, vc = 2 + group, 5 + group, 8 + group
    sub = lb // BAND

    def cur(col):
        return pl.BlockSpec((None, lb, LANE_TILE), lambda bi, r, i: (bi, i, r * n_cols + col))

    def prev(col):
        return pl.BlockSpec((None, BAND, LANE_TILE),
                            lambda bi, r, i: (bi, jnp.maximum(i * sub - 1, 0), r * n_cols + col))

    tab_cur = pl.BlockSpec((lb, LANE_TILE), lambda bi, r, i: (i, r))
    tab_prev = pl.BlockSpec((BAND, LANE_TILE), lambda bi, r, i: (jnp.maximum(i * sub - 1, 0), r))
    gain = pl.BlockSpec((1, LANE_TILE), lambda bi, r, i: (0, 0))
    out = pl.BlockSpec((None, lb, LANE_TILE), lambda bi, r, i: (bi, i, r))
    o, lse = pl.pallas_call(
        _band_attn_body,
        out_shape=[jax.ShapeDtypeStruct((b, length, dil * LANE_TILE), F32)] * 2,
        grid=(b, dil, length // lb),
        in_specs=[cur(qc), cur(kc), prev(kc), cur(vc), prev(vc), tab_cur, tab_cur, tab_prev, tab_prev,
                  gain, gain],
        out_specs=[out, out],
        compiler_params=_params("parallel", "parallel", "arbitrary"),
        name=f"band_attn_d{dil}",
    )(pv, pv, pv, pv, pv, cs, sn, cs, sn, gq, gk)
    return o.reshape(b, s, LANE_TILE), lse.reshape(b, s, LANE_TILE)


def _even_out_body(x_ref, a_ref, o0, o1, o2, l0, l1, l2, wa_ref, wb_ref, out_ref):
    m = jnp.maximum(jnp.maximum(l0[...], l1[...]), l2[...])
    e0 = jnp.exp(l0[...] - m)
    e1 = jnp.exp(l1[...] - m)
    e2 = jnp.exp(l2[...] - m)
    merged = (e0 * o0[...] + e1 * o1[...] + e2 * o2[...]) / (e0 + e1 + e2)
    acc = x_ref[...] + jnp.dot(a_ref[...], wa_ref[...], preferred_element_type=F32)
    out_ref[...] = acc + jnp.dot(merged.astype(BF16), wb_ref[...], preferred_element_type=F32)


def _even_out(x2, a2, outs, lses, wa, wb, *, tm=512):
    t, d = x2.shape
    row = lambda w: pl.BlockSpec((tm, w), lambda i: (i, 0))
    full = lambda a: pl.BlockSpec(a.shape, lambda i: (0, 0))
    return pl.pallas_call(
        _even_out_body,
        out_shape=jax.ShapeDtypeStruct((t, d), F32),
        grid=(t // tm,),
        in_specs=[row(d), row(GMLP_WIDTH)] + [row(LANE_TILE)] * 6 + [full(wa), full(wb)],
        out_specs=row(d),
        compiler_params=_params("parallel"),
        name="even_out_proj",
    )(x2, a2, *outs, *lses, wa, wb)


def _swiglu_step(h_bf16, wg, wu, wd):
    g = jnp.dot(h_bf16, wg, preferred_element_type=F32)
    u = jnp.dot(h_bf16, wu, preferred_element_type=F32)
    a = (g * jax.nn.sigmoid(g) * u).astype(BF16)
    return jnp.dot(a, wd, preferred_element_type=F32)


def _ffn_body(x_ref, g_ref, wg_ref, wu_ref, wd_ref, o_ref, h_s, acc_s):
    f = pl.program_id(1)

    @pl.when(f == 0)
    def _():
        x = x_ref[...]
        h_s[...] = _row_norm(x, g_ref[...]).astype(BF16)
        acc_s[...] = x

    acc_s[...] += _swiglu_step(h_s[...], wg_ref[...], wu_ref[...], wd_ref[...])

    @pl.when(f == pl.num_programs(1) - 1)
    def _():
        o_ref[...] = acc_s[...]


def _ffn(x2, gain, wg, wu, wd, *, tm=512, tf=1408):
    t, d = x2.shape
    ff = wg.shape[1]
    return pl.pallas_call(
        _ffn_body,
        out_shape=jax.ShapeDtypeStruct((t, d), F32),
        grid=(t // tm, ff // tf),
        in_specs=[
            pl.BlockSpec((tm, d), lambda i, f: (i, 0)),
            pl.BlockSpec((1, d), lambda i, f: (0, 0)),
            pl.BlockSpec((d, tf), lambda i, f: (0, f)),
            pl.BlockSpec((d, tf), lambda i, f: (0, f)),
            pl.BlockSpec((tf, d), lambda i, f: (f, 0)),
        ],
        out_specs=pl.BlockSpec((tm, d), lambda i, f: (i, 0)),
        scratch_shapes=[pltpu.VMEM((tm, d), BF16), pltpu.VMEM((tm, d), F32)],
        compiler_params=_params("parallel", "arbitrary"),
        name="ffn_swiglu",
    )(x2, gain.reshape(1, d), wg, wu, wd)


def _expert_body(te_ref, first_ref, nv_ref, x_ref, wg_ref, wu_ref, wd_ref, o_ref,
                 h_s, acc_s, cg_s, cu_s, cd_s):
    i = pl.program_id(0)
    f = pl.program_id(1)

    @pl.when(first_ref[i] == 1)
    def _():
        cg_s[f] = wg_ref[...].astype(BF16)
        cu_s[f] = wu_ref[...].astype(BF16)
        cd_s[f] = wd_ref[...].astype(BF16)

    @pl.when(f == 0)
    def _():
        h_s[...] = x_ref[...].astype(BF16)
        acc_s[...] = jnp.zeros_like(acc_s)

    @pl.when(i < nv_ref[0])
    def _():
        acc_s[...] += _swiglu_step(h_s[...], cg_s[f], cu_s[f], cd_s[f])

    @pl.when(f == pl.num_programs(1) - 1)
    def _():
        o_ref[...] = acc_s[...]


def _expert_ffn(xs, tile_expert, tile_first, n_live, wg, wu, wd, *, layer, tm, tf=512):
    p, d = xs.shape
    ff = wg.shape[-1]
    nf = ff // tf

    def chunk(i, f, first):
        return jnp.where(first[i] == 1, f, nf - 1)

    def w_in(i, f, te, first, nv):
        return (layer, te[i], 0, chunk(i, f, first))

    def w_out(i, f, te, first, nv):
        return (layer, te[i], chunk(i, f, first), 0)

    row = lambda i, f, te, first, nv: (i, 0)
    return pl.pallas_call(
        _expert_body,
        out_shape=jax.ShapeDtypeStruct((p, d), F32),
        grid_spec=pltpu.PrefetchScalarGridSpec(
            num_scalar_prefetch=3,
            grid=(p // tm, nf),
            in_specs=[
                pl.BlockSpec((tm, d), row),
                pl.BlockSpec((None, None, d, tf), w_in),
                pl.BlockSpec((None, None, d, tf), w_in),
                pl.BlockSpec((None, None, tf, d), w_out),
            ],
            out_specs=pl.BlockSpec((tm, d), row),
            scratch_shapes=[
                pltpu.VMEM((tm, d), BF16),
                pltpu.VMEM((tm, d), F32),
                pltpu.VMEM((nf, d, tf), BF16),
                pltpu.VMEM((nf, d, tf), BF16),
                pltpu.VMEM((nf, tf, d), BF16),
            ],
        ),
        compiler_params=_params("arbitrary", "arbitrary"),
        name="expert_swiglu",
    )(tile_expert, tile_first, n_live, xs, wg, wu, wd)


def _moba_body(q_ref, k_ref, v_ref, cos_ref, sin_ref, gq_ref, gk_ref, o_ref,
               kp_s, vt_s, kmean_s, sel_s, qh_s, sa_s, sb_s, m_s, out_s):
    i = pl.program_id(2)
    nblk = kmean_s.shape[0]
    lane = _lane_ids()
    bd = _head_block_diag()
    cos = cos_ref[...]
    sin = sin_ref[...]

    @pl.when(i == 0)
    def _():
        kmean_s[...] = jnp.zeros_like(kmean_s)

    kc = _rope(_head_norm(k_ref[...].astype(F32), gk_ref[...], bd), cos, sin, lane)
    kp_s[i] = kc.astype(BF16)
    kmean_s[pl.ds(i, 1), :] = jnp.mean(kc, axis=0, keepdims=True)
    vt = v_ref[...].astype(F32).T.astype(BF16)
    for h in range(HEADS_PER_TILE):
        vt_s[i, h * MOBA_ACC_ROWS:h * MOBA_ACC_ROWS + HEAD_DIM, :] = vt[h * HEAD_DIM:(h + 1) * HEAD_DIM, :]
        vt_s[i, h * MOBA_ACC_ROWS + HEAD_DIM:(h + 1) * MOBA_ACC_ROWS, :] = jnp.ones(
            (MOBA_ACC_ROWS - HEAD_DIM, MOBA_BLOCK), BF16)

    qn = _rope(_head_norm(q_ref[...].astype(F32), gq_ref[...], bd), cos, sin, lane)
    qn = (qn * (HEAD_DIM ** -0.5 * LOG2_E)).astype(BF16)
    blk = lax.broadcasted_iota(I32, (nblk, MOBA_BLOCK), 0).astype(F32)
    first_blk = i.astype(F32)
    km = kmean_s[...]
    krow = lax.broadcasted_iota(I32, (MOBA_BLOCK, MOBA_BLOCK), 0)
    qcol = lax.broadcasted_iota(I32, (MOBA_BLOCK, MOBA_BLOCK), 1)
    causal = krow <= qcol

    km_hi = km.astype(BF16)
    km_lo = (km - km_hi.astype(F32)).astype(BF16)
    acc_rows = [slice(h * MOBA_ACC_ROWS, (h + 1) * MOBA_ACC_ROWS) for h in range(HEADS_PER_TILE)]

    for h in range(HEADS_PER_TILE):
        qh = jnp.where(lane // HEAD_DIM == h, qn, jnp.zeros_like(qn))
        qh_s[h] = qh
        gate = (lax.dot_general(km_hi, qh, _NT, preferred_element_type=F32)
                + lax.dot_general(km_lo, qh, _NT, preferred_element_type=F32))
        gate = jnp.where(blk < first_blk, gate, NEG_INF)
        sel = jnp.zeros((nblk, MOBA_BLOCK), F32)
        for _ in range(MOBA_TOPK):
            top = jnp.max(gate, axis=0, keepdims=True)
            idx = jnp.min(jnp.where(gate == top, blk, float(nblk)), axis=0, keepdims=True)
            hit = blk == idx
            sel = jnp.where(hit & (top > NEG_INF), 1.0, sel)
            gate = jnp.where(hit, NEG_INF, gate)
        sel_s[h] = sel
        m_s[h] = jnp.full((1, MOBA_BLOCK), MOBA_M_INIT, F32)
    out_s[...] = jnp.zeros_like(out_s)

    def scores(j, buf):
        jc = jnp.minimum(j, i)
        for h in range(HEADS_PER_TILE):
            buf[h] = lax.dot_general(kp_s[jc], qh_s[h], _NT, preferred_element_type=F32)

    def absorb(j, buf):
        jc = jnp.minimum(j, i)
        bar = jnp.where(j < i, 0.0, 2.0)
        for h in range(HEADS_PER_TILE):
            bias = jnp.where(sel_s[h, pl.ds(jc, 1), :] > bar, 0.0, NEG_INF)
            s = buf[h]
            m = m_s[h]
            m_new = jnp.maximum(m, jnp.max(s, axis=0, keepdims=True) + bias)
            p = jnp.exp2(s + (bias - m_new)).astype(BF16)
            m_s[h] = m_new
            out_s[acc_rows[h], :] = (jnp.exp2(m - m_new) * out_s[acc_rows[h], :]
                                     + jnp.dot(vt_s[jc, acc_rows[h], :], p, preferred_element_type=F32))

    scores(0, sa_s)

    def step(t, carry):
        scores(2 * t + 1, sb_s)
        absorb(2 * t, sa_s)
        scores(2 * t + 2, sa_s)
        absorb(2 * t + 1, sb_s)
        return carry

    lax.fori_loop(0, (i + 1) // 2, step, 0)


    heads = []
    for h in range(HEADS_PER_TILE):
        s = lax.dot_general(kp_s[i], qh_s[h], _NT, preferred_element_type=F32)
        s = jnp.where(causal, s, NEG_INF)
        m = m_s[h]
        m_new = jnp.maximum(m, jnp.max(s, axis=0, keepdims=True))
        p = jnp.exp2(s - m_new).astype(BF16)
        acc = (jnp.exp2(m - m_new) * out_s[acc_rows[h], :]
               + jnp.dot(vt_s[i, acc_rows[h], :], p, preferred_element_type=F32))
        heads.append(acc[:HEAD_DIM, :] / acc[HEAD_DIM:HEAD_DIM + 1, :])
    o_ref[...] = jnp.concatenate(heads, axis=0).T.astype(o_ref.dtype)


def _moba(p3, cos_t, sin_t, gq, gk):
    b, s, _ = p3.shape
    nblk = s // MOBA_BLOCK
    tiles = MOBA_WIDTH // LANE_TILE

    def col(off):
        return pl.BlockSpec((None, MOBA_BLOCK, LANE_TILE), lambda bi, hg, i: (bi, i, off + hg))

    tab = pl.BlockSpec((MOBA_BLOCK, LANE_TILE), lambda bi, hg, i: (i, 0))
    gain = pl.BlockSpec((1, LANE_TILE), lambda bi, hg, i: (0, 0))
    return pl.pallas_call(
        _moba_body,
        out_shape=jax.ShapeDtypeStruct((b, s, MOBA_WIDTH), BF16),
        grid=(b, tiles, nblk),
        in_specs=[col(0), col(tiles), col(2 * tiles), tab, tab, gain, gain],
        out_specs=pl.BlockSpec((None, MOBA_BLOCK, LANE_TILE), lambda bi, hg, i: (bi, i, hg)),
        scratch_shapes=[
            pltpu.VMEM((nblk, MOBA_BLOCK, LANE_TILE), BF16),
            pltpu.VMEM((nblk, HEADS_PER_TILE * MOBA_ACC_ROWS, MOBA_BLOCK), BF16),
            pltpu.VMEM((nblk, LANE_TILE), F32),
            pltpu.VMEM((HEADS_PER_TILE, nblk, MOBA_BLOCK), F32),
            pltpu.VMEM((HEADS_PER_TILE, MOBA_BLOCK, LANE_TILE), BF16),
            pltpu.VMEM((HEADS_PER_TILE, MOBA_BLOCK, MOBA_BLOCK), F32),
            pltpu.VMEM((HEADS_PER_TILE, MOBA_BLOCK, MOBA_BLOCK), F32),
            pltpu.VMEM((HEADS_PER_TILE, 1, MOBA_BLOCK), F32),
            pltpu.VMEM((HEADS_PER_TILE * MOBA_ACC_ROWS, MOBA_BLOCK), F32),
        ],
        compiler_params=_params("arbitrary", "arbitrary", "arbitrary"),
        name="moba_attn",
    )(p3, p3, p3, cos_t, sin_t, gq, gk)


def _conv_body(bg_ref, cg_ref, xz_ref, w_ref, o_ref, *, ts):
    halo = 16
    w = w_ref[...]
    for c in range(o_ref.shape[0] // ts):
        lo = c * ts
        if c == 0:
            z = cg_ref[0:ts, :].astype(F32) * xz_ref[0:ts, :].astype(F32)
            zp = jnp.concatenate([jnp.zeros((halo, CONV_WIDTH), F32), z], axis=0)
        else:
            zp = cg_ref[lo - halo:lo + ts, :].astype(F32) * xz_ref[lo - halo:lo + ts, :].astype(F32)
        z1 = pltpu.roll(zp, 1, 0)[halo:]
        z2 = pltpu.roll(zp, 2, 0)[halo:]
        y = w[2:3, :] * zp[halo:] + w[1:2, :] * z1 + w[0:1, :] * z2
        o_ref[lo:lo + ts, :] = (bg_ref[lo:lo + ts, :].astype(F32) * y).astype(o_ref.dtype)


def _short_conv(p3, conv_w, *, col0, ts=512):
    b, s, _ = p3.shape
    taps = conv_w.shape[0]

    def col(c):
        return pl.BlockSpec((None, s, CONV_WIDTH), lambda bi: (bi, 0, col0 + c))

    return pl.pallas_call(
        functools.partial(_conv_body, ts=ts),
        out_shape=jax.ShapeDtypeStruct((b, s, CONV_WIDTH), BF16),
        grid=(b,),
        in_specs=[col(0), col(1), col(2), pl.BlockSpec((taps, CONV_WIDTH), lambda bi: (0, 0))],
        out_specs=pl.BlockSpec((None, s, CONV_WIDTH), lambda bi: (bi, 0, 0)),
        compiler_params=_params("parallel"),
        name="short_conv",
    )(p3, p3, p3, conv_w)


def _odd_out_body(x_ref, c_ref, d_ref, wc_ref, wd_ref, gain_ref, rw_ref, xo_ref, h_ref, route_ref):
    x = x_ref[...] + jnp.dot(c_ref[...], wc_ref[...], preferred_element_type=F32)
    x = x + jnp.dot(d_ref[...], wd_ref[...], preferred_element_type=F32)
    xo_ref[...] = x
    h = _row_norm(x, gain_ref[...])
    h_ref[...] = h
    logits = jnp.dot(h, rw_ref[...], preferred_element_type=F32, precision=lax.Precision.HIGHEST)
    lane = lax.broadcasted_iota(I32, logits.shape, 1)
    lane_f = lane.astype(F32)
    logits = jnp.where(lane < N_EXPERTS, logits, NEG_INF)
    v1 = jnp.max(logits, axis=-1, keepdims=True)
    i1 = jnp.min(jnp.where(logits == v1, lane_f, float(ROUTE_LANES)), axis=-1, keepdims=True)
    rest = jnp.where(lane_f == i1, NEG_INF, logits)
    v2 = jnp.max(rest, axis=-1, keepdims=True)
    i2 = jnp.min(jnp.where(rest == v2, lane_f, float(ROUTE_LANES)), axis=-1, keepdims=True)
    e = jnp.exp(v2 - v1)
    g1 = 1.0 / (1.0 + e)
    g2 = e / (1.0 + e)
    route = jnp.where(lane == 0, i1, 0.0)
    route = jnp.where(lane == 1, i2, route)
    route = jnp.where(lane == 2, g1, route)
    route_ref[...] = jnp.where(lane == 3, g2, route)


def _odd_out(x2, c2, d2, wc, wd, gain, router_pad, *, tm=512):
    t, d = x2.shape
    row = lambda w: pl.BlockSpec((tm, w), lambda i: (i, 0))
    full = lambda a: pl.BlockSpec(a.shape, lambda i: (0, 0))
    return pl.pallas_call(
        _odd_out_body,
        out_shape=[jax.ShapeDtypeStruct((t, d), F32), jax.ShapeDtypeStruct((t, d), F32),
                   jax.ShapeDtypeStruct((t, ROUTE_LANES), F32)],
        grid=(t // tm,),
        in_specs=[row(d), row(MOBA_WIDTH), row(CONV_WIDTH), full(wc), full(wd),
                  pl.BlockSpec((1, d), lambda i: (0, 0)), full(router_pad)],
        out_specs=[row(d), row(d), row(ROUTE_LANES)],
        compiler_params=_params("parallel"),
        name="odd_out_proj_router",
    )(x2, c2, d2, wc, wd, gain.reshape(1, d), router_pad)


def _row_copy(src_ref, src_row, dst_ref, dst_row, sem):
    return pltpu.make_async_copy(src_ref.at[pl.ds(src_row, 1)], dst_ref.at[pl.ds(dst_row, 1)], sem)


def _dispatch_body(pos_ref, h_ref, init_ref, xs_ref, sem):
    del init_ref
    td = h_ref.shape[0]

    def copies(r):
        return [_row_copy(h_ref, r, xs_ref, pos_ref[0, 0, k * td + r], sem) for k in range(2)]

    def start(r, c):
        for cp in copies(r):
            cp.start()
        return c

    lax.fori_loop(0, td, start, 0, unroll=DMA_UNROLL)
    for _ in range(2):
        pltpu.make_async_copy(h_ref, xs_ref.at[pl.ds(0, td)], sem).wait()


def _dispatch(h2, pos3, n_rows, *, td):
    t, d = h2.shape
    return pl.pallas_call(
        _dispatch_body,
        out_shape=jax.ShapeDtypeStruct((n_rows, d), F32),
        grid=(t // td,),
        in_specs=[
            pl.BlockSpec((1, 1, 2 * td), lambda i: (i, 0, 0), memory_space=pltpu.SMEM),
            pl.BlockSpec((td, d), lambda i: (i, 0)),
            pl.BlockSpec(memory_space=pl.ANY),
        ],
        out_specs=pl.BlockSpec(memory_space=pl.ANY),
        scratch_shapes=[pltpu.SemaphoreType.DMA(())],
        input_output_aliases={2: 0},
        compiler_params=_params("arbitrary"),
        name="expert_dispatch",
    )(pos3, h2, jnp.zeros((n_rows, d), F32))


def _combine_body(pos_ref, x_ref, route_ref, y_ref, o_ref, buf, sem):
    td = x_ref.shape[0]

    def copies(r):
        return [_row_copy(y_ref, pos_ref[0, 0, k * td + r], buf.at[k], r, sem) for k in range(2)]

    def start(r, c):
        for cp in copies(r):
            cp.start()
        return c

    lax.fori_loop(0, td, start, 0, unroll=DMA_UNROLL)
    for k in range(2):
        pltpu.make_async_copy(y_ref.at[pl.ds(0, td)], buf.at[k], sem).wait()
    route = route_ref[...]
    o_ref[...] = x_ref[...] + route[:, 2:3] * buf[0] + route[:, 3:4] * buf[1]


def _combine(x2, route, ys, pos3, *, td):
    t, d = x2.shape
    return pl.pallas_call(
        _combine_body,
        out_shape=jax.ShapeDtypeStruct((t, d), F32),
        grid=(t // td,),
        in_specs=[
            pl.BlockSpec((1, 1, 2 * td), lambda i: (i, 0, 0), memory_space=pltpu.SMEM),
            pl.BlockSpec((td, d), lambda i: (i, 0)),
            pl.BlockSpec((td, ROUTE_LANES), lambda i: (i, 0)),
            pl.BlockSpec(memory_space=pl.ANY),
        ],
        out_specs=pl.BlockSpec((td, d), lambda i: (i, 0)),
        scratch_shapes=[pltpu.VMEM((2, td, d), F32), pltpu.SemaphoreType.DMA(())],
        compiler_params=_params("arbitrary"),
        name="expert_combine",
    )(pos3, x2, route, ys)


def _route_plan(route, *, tm, td):
    t = route.shape[0]
    experts = jnp.concatenate([route[:, 0], route[:, 1]]).astype(I32)
    onehot = (experts[:, None] == jnp.arange(N_EXPERTS, dtype=I32)[None, :]).astype(I32)
    running = jnp.cumsum(onehot, axis=0)
    counts = running[-1]
    padded = ((counts + tm - 1) // tm) * tm
    ends = jnp.cumsum(padded)
    starts = ends - padded
    pos = jnp.sum(onehot * (starts[None, :] + running - 1), axis=1)
    n_tiles = (2 * t) // tm + N_EXPERTS
    tile_lo = jnp.arange(n_tiles, dtype=I32) * tm
    tile_expert = jnp.sum((tile_lo[:, None] >= ends[None, :]).astype(I32), axis=1)
    n_live = (ends[-1] // tm).astype(I32)
    last = tile_expert[jnp.maximum(n_live - 1, 0)]
    live = jnp.arange(n_tiles) < n_live
    tile_expert = jnp.where(live, tile_expert, last)
    prev = jnp.concatenate([jnp.full((1,), -1, I32), tile_expert[:-1]])
    tile_first = (live & (tile_expert != prev)).astype(I32)
    pos3 = pos.reshape(2, t // td, td).transpose(1, 0, 2).reshape(t // td, 1, 2 * td)
    return pos3, tile_expert, tile_first, n_live.reshape(1), n_tiles * tm


def _rope_tables(seq):
    inv = 1.0 / (ROPE_THETA ** (jnp.arange(0, HEAD_DIM, 2, dtype=F32) / HEAD_DIM))
    ang = jnp.arange(seq, dtype=F32)[:, None] * inv[None, :]
    ang = jnp.concatenate([ang, ang], axis=-1)
    tile = lambda a: jnp.tile(a, (1, HEADS_PER_TILE))
    return tile(jnp.cos(ang)), tile(jnp.sin(ang))


def _tile_gain(g):
    return jnp.tile(g, HEADS_PER_TILE).reshape(1, LANE_TILE)


def _even_layer(x2, b, s, cos_t, sin_t, norm_mix, norm_ffn, w_in, v_gain, ws, bs, gq, gk, w_out, wg, wu, wd):
    p = _norm_proj(x2, norm_mix, w_in.astype(BF16))
    n_cols = p.shape[1] // LANE_TILE
    p3 = p.reshape(b, s, p.shape[1])
    a = _gmlp(p3, v_gain, ws, bs)
    outs, lses = [], []
    for group, (_, dil) in enumerate(DIL_CONFIGS):
        o, lse = _band_attn(p3, cos_t, sin_t, _tile_gain(gq), _tile_gain(gk), group=group, dil=dil, n_cols=n_cols)
        outs.append(o.reshape(b * s, LANE_TILE))
        lses.append(lse.reshape(b * s, LANE_TILE))
    w_out = w_out.astype(BF16)
    x2 = _even_out(x2, a.reshape(b * s, GMLP_WIDTH), outs, lses, w_out[:GMLP_WIDTH], w_out[GMLP_WIDTH:])
    return _ffn(x2, norm_ffn, wg.astype(BF16), wu.astype(BF16), wd.astype(BF16))


def _odd_layer(x2, b, s, cos_t, sin_t, norm_mix, norm_ffn, w_in, gq, gk, conv_w, w_out, router_w, wg, wu, wd,
               *, layer, tm=512, td=256):
    p = _norm_proj(x2, norm_mix, w_in.astype(BF16))
    p3 = p.reshape(b, s, p.shape[1])
    c = _moba(p3, cos_t, sin_t, _tile_gain(gq), _tile_gain(gk))
    dconv = _short_conv(p3, conv_w, col0=3 * MOBA_WIDTH // CONV_WIDTH)
    w_out = w_out.astype(BF16)
    router_pad = jnp.pad(router_w, ((0, 0), (0, ROUTE_LANES - N_EXPERTS)))
    x2, h2, route = _odd_out(x2, c.reshape(b * s, MOBA_WIDTH), dconv.reshape(b * s, CONV_WIDTH),
                             w_out[:MOBA_WIDTH], w_out[MOBA_WIDTH:], norm_ffn, router_pad)
    pos3, tile_expert, tile_first, n_live, n_rows = _route_plan(route, tm=tm, td=td)
    xs = _dispatch(h2, pos3, n_rows, td=td)
    ys = _expert_ffn(xs, tile_expert, tile_first, n_live, wg, wu, wd, layer=layer, tm=tm)
    return _combine(x2, route, ys, pos3, td=td)


def kernel(x, norm_mix, norm_ffn, w_in_ab, gmlp_v_gain, gmlp_ws, gmlp_bs, dil_q_gain, dil_k_gain, w_out_ab,
           ffn_w_gate, ffn_w_up, ffn_w_down, w_in_cd, moba_q_gain, moba_k_gain, conv_w, w_out_cd, router_w,
           moe_w_gate, moe_w_up, moe_w_down):
    b, s, d = x.shape
    depth = norm_mix.shape[0]
    cos_t, sin_t = _rope_tables(s)
    x2 = x.reshape(b * s, d)
    for layer in range(depth):
        i = layer // 2
        if layer % 2 == 0:
            x2 = _even_layer(x2, b, s, cos_t, sin_t, norm_mix[layer], norm_ffn[layer], w_in_ab[i], gmlp_v_gain[i],
                             gmlp_ws[i], gmlp_bs[i], dil_q_gain[i], dil_k_gain[i], w_out_ab[i], ffn_w_gate[i],
                             ffn_w_up[i], ffn_w_down[i])
        else:
            x2 = _odd_layer(x2, b, s, cos_t, sin_t, norm_mix[layer], norm_ffn[layer], w_in_cd[i], moba_q_gain[i],
                            moba_k_gain[i], conv_w[i], w_out_cd[i], router_w[i], moe_w_gate, moe_w_up,
                            moe_w_down, layer=i)
    return x2.reshape(b, s, d)
```

```python
import functools

import jax
import jax.numpy as jnp
from jax import lax
from jax.experimental import pallas as pl
from jax.experimental.pallas import tpu as pltpu

F32 = jnp.float32
BF16 = jnp.bfloat16
I32 = jnp.int32

D_MODEL = 1024
HEAD_DIM = 64
ROPE_THETA = 10000.0
NORM_EPS = 1e-6
LANE_TILE = 256
HEADS_PER_TILE = LANE_TILE // HEAD_DIM
GMLP_WIDTH = 256
GMLP_CHUNK = 128
DIL_CONFIGS = ((128, 1), (512, 4), (2048, 16))
DIL_WIDTH = 768
BAND = 128
BAND_ACC_ROWS = 128
BAND_L_ROW = HEAD_DIM
BAND_M_ROW = HEAD_DIM + 1
BAND_LOOKAHEAD = 5
STRIDE_LANES = 128
MOBA_WIDTH = 768
MOBA_BLOCK = 256
MOBA_TOPK = 3
MOBA_ACC_ROWS = HEAD_DIM + 16
LOG2_E = 1.4426950408889634
MOBA_M_INIT = -1e30
CONV_WIDTH = 256
N_EXPERTS = 8
ROUTE_LANES = 128
NEG_INF = float("-inf")
DMA_UNROLL = 8

_NT = (((1,), (1,)), ((), ()))


def _params(*sem):
    return pltpu.CompilerParams(dimension_semantics=tuple(sem), vmem_limit_bytes=56 * 1024 * 1024)


def _lane_ids():
    return lax.broadcasted_iota(I32, (1, LANE_TILE), 1)


def _head_block_diag():
    r = lax.broadcasted_iota(I32, (LANE_TILE, LANE_TILE), 0) // HEAD_DIM
    c = lax.broadcasted_iota(I32, (LANE_TILE, LANE_TILE), 1) // HEAD_DIM
    return jnp.where(r == c, 1.0, 0.0).astype(BF16)


def _head_mean_sq(x, bd):
    s = x * x
    hi = s.astype(BF16)
    lo = (s - hi.astype(F32)).astype(BF16)
    tot = jnp.dot(hi, bd, preferred_element_type=F32) + jnp.dot(lo, bd, preferred_element_type=F32)
    return tot * (1.0 / HEAD_DIM)


def _head_norm(x, gain, bd):
    return x * lax.rsqrt(_head_mean_sq(x, bd) + NORM_EPS) * gain


def _rope(x, cos, sin, lane):
    half = HEAD_DIM // 2
    x_up = pltpu.roll(x, LANE_TILE - half, 1)
    x_dn = pltpu.roll(x, half, 1)
    rot = jnp.where((lane % HEAD_DIM) < half, -x_up, x_dn)
    return x * cos + rot * sin


def _row_norm(x, gain):
    return x * lax.rsqrt(jnp.mean(x * x, axis=-1, keepdims=True) + NORM_EPS) * gain


def _norm_proj_body(x_ref, g_ref, w_ref, o_ref, *, tn):
    hb = _row_norm(x_ref[...], g_ref[...]).astype(BF16)
    for c in range(o_ref.shape[-1] // tn):
        sl = slice(c * tn, (c + 1) * tn)
        o_ref[:, sl] = jnp.dot(hb, w_ref[:, sl], preferred_element_type=F32).astype(o_ref.dtype)


def _norm_proj(x2, gain, w_bf16, *, tm=512, tn=256):
    t, d = x2.shape
    n = w_bf16.shape[1]
    return pl.pallas_call(
        functools.partial(_norm_proj_body, tn=tn),
        out_shape=jax.ShapeDtypeStruct((t, n), BF16),
        grid=(t // tm,),
        in_specs=[
            pl.BlockSpec((tm, d), lambda i: (i, 0)),
            pl.BlockSpec((1, d), lambda i: (0, 0)),
            pl.BlockSpec((d, n), lambda i: (0, 0)),
        ],
        out_specs=pl.BlockSpec((tm, n), lambda i: (i, 0)),
        compiler_params=_params("parallel"),
        name="norm_proj",
    )(x2, gain.reshape(1, d), w_bf16)


def _gmlp_body(u_ref, v_ref, gain_ref, ws_ref, bias_ref, o_ref):
    lane = _lane_ids()
    bd = _head_block_diag()
    r = lax.broadcasted_iota(I32, (GMLP_CHUNK, GMLP_CHUNK), 0)
    c = lax.broadcasted_iota(I32, (GMLP_CHUNK, GMLP_CHUNK), 1)
    w_tril = [jnp.where(r >= c, ws_ref[g], 0.0).astype(BF16) for g in range(HEADS_PER_TILE)]
    for ch in range(o_ref.shape[0] // GMLP_CHUNK):
        sl = slice(ch * GMLP_CHUNK, (ch + 1) * GMLP_CHUNK)
        u = jax.nn.gelu(u_ref[sl, :].astype(F32))
        v = jax.nn.gelu(v_ref[sl, :].astype(F32))
        vn = _head_norm(v, gain_ref[...], bd).astype(BF16)
        y = bias_ref[...]
        for g in range(HEADS_PER_TILE):
            yg = jnp.dot(w_tril[g], vn, preferred_element_type=F32)
            y = y + jnp.where(lane // HEAD_DIM == g, yg, 0.0)
        o_ref[sl, :] = (u * y).astype(o_ref.dtype)


def _gmlp(p3, v_gain, ws, bs, *, tc=512):
    b, s, _ = p3.shape
    groups, chunk = bs.shape
    bias = jnp.repeat(bs.T, HEAD_DIM, axis=1)
    return pl.pallas_call(
        _gmlp_body,
        out_shape=jax.ShapeDtypeStruct((b, s, GMLP_WIDTH), BF16),
        grid=(b, s // tc),
        in_specs=[
            pl.BlockSpec((None, tc, LANE_TILE), lambda bi, i: (bi, i, 0)),
            pl.BlockSpec((None, tc, LANE_TILE), lambda bi, i: (bi, i, 1)),
            pl.BlockSpec((1, GMLP_WIDTH), lambda bi, i: (0, 0)),
            pl.BlockSpec((groups, chunk, chunk), lambda bi, i: (0, 0, 0)),
            pl.BlockSpec((chunk, GMLP_WIDTH), lambda bi, i: (0, 0)),
        ],
        out_specs=pl.BlockSpec((None, tc, GMLP_WIDTH), lambda bi, i: (bi, i, 0)),
        compiler_params=_params("parallel", "parallel"),
        name="gmlp_gate",
    )(p3, p3, v_gain.reshape(1, GMLP_WIDTH), ws, bias)


def _residues(val, scr, dil):
    if dil == 1:
        return [val]
    rows = val.shape[0] // dil
    slabs = LANE_TILE // STRIDE_LANES
    for c in range(slabs):
        scr[c] = val[:, c * STRIDE_LANES:(c + 1) * STRIDE_LANES]
    return [jnp.concatenate([scr[c, pl.ds(r, rows, stride=dil), :] for c in range(slabs)], axis=1)
            for r in range(dil)]


def _qkv_prep_body(q_ref, k_ref, v_ref, cos_ref, sin_ref, gq_ref, gk_ref, *rest, dil, with_kmean):
    n_out = 4 if with_kmean else 3
    qo_ref, ko_ref, vt_ref = rest[:3]
    scr = list(rest[n_out:]) + [None] * 3
    lane = _lane_ids()
    bd = _head_block_diag()
    cos = cos_ref[...]
    sin = sin_ref[...]
    qn = _rope(_head_norm(q_ref[...].astype(F32), gq_ref[...], bd), cos, sin, lane) * (HEAD_DIM ** -0.5 * LOG2_E)
    kn = _rope(_head_norm(k_ref[...].astype(F32), gk_ref[...], bd), cos, sin, lane)
    if with_kmean:
        km_ref = rest[3]
        km_ref[...] = jnp.mean(kn.reshape(km_ref.shape[0], MOBA_BLOCK, LANE_TILE), axis=1)
    for r, piece in enumerate(_residues(qn, scr[0], dil)):
        qo_ref[r] = piece.astype(BF16)
    for r, piece in enumerate(_residues(kn, scr[1], dil)):
        ko_ref[r] = piece.astype(BF16)
    n_blocks, acc_w, vblock = vt_ref.shape[1:]
    head_rows = acc_w // HEADS_PER_TILE
    ones = jnp.ones((head_rows - HEAD_DIM, vblock), BF16)
    for r, piece in enumerate(_residues(v_ref[...].astype(F32), scr[2], dil)):
        for c in range(n_blocks):
            vt = piece[c * vblock:(c + 1) * vblock, :].T.astype(BF16)
            for h in range(HEADS_PER_TILE):
                vt_ref[r, c, h * head_rows:h * head_rows + HEAD_DIM, :] = vt[h * HEAD_DIM:(h + 1) * HEAD_DIM, :]
                vt_ref[r, c, h * head_rows + HEAD_DIM:(h + 1) * head_rows, :] = ones


def _qkv_prep(p3, cos_t, sin_t, gq, gk, *, q_col, k_col, v_col, n_ht, dil, tt, vblock, head_rows, with_kmean):
    b, s, _ = p3.shape
    length = s // dil
    tr = tt // dil
    acc_w = HEADS_PER_TILE * head_rows

    def col(c0):
        return pl.BlockSpec((None, tt, LANE_TILE), lambda bi, ht, i: (bi, i, c0 + ht))

    tab = pl.BlockSpec((tt, LANE_TILE), lambda bi, ht, i: (i, 0))
    gain = pl.BlockSpec((1, LANE_TILE), lambda bi, ht, i: (0, 0))
    qk_shape = jax.ShapeDtypeStruct((b, n_ht, dil, length, LANE_TILE), BF16)
    qk_spec = pl.BlockSpec((None, None, dil, tr, LANE_TILE), lambda bi, ht, i: (bi, ht, 0, i, 0))
    out_shape = [qk_shape, qk_shape,
                 jax.ShapeDtypeStruct((b, n_ht, dil, length // vblock, acc_w, vblock), BF16)]
    out_specs = [qk_spec, qk_spec,
                 pl.BlockSpec((None, None, dil, tr // vblock, acc_w, vblock), lambda bi, ht, i: (bi, ht, 0, i, 0, 0))]
    if with_kmean:
        out_shape.append(jax.ShapeDtypeStruct((b, n_ht, s // MOBA_BLOCK, LANE_TILE), F32))
        out_specs.append(pl.BlockSpec((None, None, tt // MOBA_BLOCK, LANE_TILE), lambda bi, ht, i: (bi, ht, i, 0)))
    scratch = [pltpu.VMEM((LANE_TILE // STRIDE_LANES, tt, STRIDE_LANES), F32)] * (3 if dil > 1 else 0)
    return pl.pallas_call(
        functools.partial(_qkv_prep_body, dil=dil, with_kmean=with_kmean),
        out_shape=out_shape,
        grid=(b, n_ht, s // tt),
        in_specs=[col(q_col), col(k_col), col(v_col), tab, tab, gain, gain],
        out_specs=out_specs,
        scratch_shapes=scratch,
        compiler_params=_params("parallel", "parallel", "parallel"),
        name=f"qkv_prep_d{dil}",
    )(p3, p3, p3, cos_t, sin_t, gq, gk)


def _band_attn_body(q_ref, k_ref, kp_ref, vt_ref, vtp_ref, o_ref):
    i = pl.program_id(2)
    lb = q_ref.shape[0]
    lane = _lane_ids()
    kk = lax.broadcasted_iota(I32, (2 * BAND, BAND), 0)
    qq = lax.broadcasted_iota(I32, (2 * BAND, BAND), 1) + BAND
    in_band = (qq - kk >= 0) & (qq - kk <= BAND)
    first_ok = in_band & ((kk >= BAND) | (i > 0))
    acc_row = lax.broadcasted_iota(I32, (BAND_ACC_ROWS, BAND), 0)
    k_all = jnp.concatenate([kp_ref[...], k_ref[...]], axis=0)
    chains = [(j, h) for j in range(lb // BAND) for h in range(HEADS_PER_TILE)]

    def scores(j, h):
        qj = q_ref[j * BAND:(j + 1) * BAND, :]
        qh = jnp.where(lane // HEAD_DIM == h, qj, jnp.zeros_like(qj))
        return lax.dot_general(k_all[j * BAND:(j + 2) * BAND], qh, _NT, preferred_element_type=F32)

    staged = [scores(*c) for c in chains[:BAND_LOOKAHEAD]]
    tiles = []
    for n, (j, h) in enumerate(chains):
        if n + BAND_LOOKAHEAD < len(chains):
            staged.append(scores(*chains[n + BAND_LOOKAHEAD]))
        rows = slice(h * BAND_ACC_ROWS, (h + 1) * BAND_ACC_ROWS)
        s = jnp.where(first_ok if j == 0 else in_band, staged[n], NEG_INF)
        m = jnp.max(s, axis=0, keepdims=True)
        p = jnp.exp2(s - m).astype(BF16)
        v_prev = vtp_ref[0, rows, :] if j == 0 else vt_ref[j - 1, rows, :]
        vw = jnp.concatenate([v_prev, vt_ref[j, rows, :]], axis=1)
        acc = jnp.dot(vw, p, preferred_element_type=F32)
        tiles.append(jnp.where(acc_row == BAND_M_ROW, m, acc))
        if h == HEADS_PER_TILE - 1:
            o_ref[j * BAND:(j + 1) * BAND, :] = jnp.concatenate(tiles, axis=0).T
            tiles = []


def _band_attn(qg, kg, vtg):
    b, _, dil, length, _ = qg.shape
    lb = min(length, 512)
    sub = lb // BAND
    acc_w = HEADS_PER_TILE * BAND_ACC_ROWS
    before = lambda i: jnp.maximum(i * sub - 1, 0)
    cur = pl.BlockSpec((None, None, None, lb, LANE_TILE), lambda bi, r, i: (bi, 0, r, i, 0))
    prev = pl.BlockSpec((None, None, None, BAND, LANE_TILE), lambda bi, r, i: (bi, 0, r, before(i), 0))
    vt_cur = pl.BlockSpec((None, None, None, sub, acc_w, BAND), lambda bi, r, i: (bi, 0, r, i, 0, 0))
    vt_prev = pl.BlockSpec((None, None, None, 1, acc_w, BAND), lambda bi, r, i: (bi, 0, r, before(i), 0, 0))
    return pl.pallas_call(
        _band_attn_body,
        out_shape=jax.ShapeDtypeStruct((b, dil, length, acc_w), F32),
        grid=(b, dil, length // lb),
        in_specs=[cur, cur, prev, vt_cur, vt_prev],
        out_specs=pl.BlockSpec((None, None, lb, acc_w), lambda bi, r, i: (bi, r, i, 0)),
        compiler_params=_params("parallel", "parallel", "arbitrary"),
        name=f"band_attn_d{dil}",
    )(qg, kg, kg, vtg, vtg)


def _even_out_body(x_ref, a_ref, o0_ref, o1_ref, o2_ref, wa_ref, wb_ref, out_ref, *scratch):
    tm = x_ref.shape[0]

    def natural_order(o_ref, scr):
        dil = o_ref.shape[0]
        if dil == 1:
            return [o_ref[0, :, h * BAND_ACC_ROWS:(h + 1) * BAND_ACC_ROWS] for h in range(HEADS_PER_TILE)]
        for r in range(dil):
            for h in range(HEADS_PER_TILE):
                scr[h, pl.ds(r, tm // dil, stride=dil), :] = o_ref[r, :, h * BAND_ACC_ROWS:(h + 1) * BAND_ACC_ROWS]
        return [scr[h] for h in range(HEADS_PER_TILE)]

    o_refs = (o0_ref, o1_ref, o2_ref)
    spare = list(scratch)
    groups = [natural_order(o, spare.pop(0) if o.shape[0] > 1 else None) for o in o_refs]
    acc = x_ref[...] + jnp.dot(a_ref[...], wa_ref[...], preferred_element_type=F32)
    for h in range(HEADS_PER_TILE):
        tiles = [g[h] for g in groups]
        tops = [t[:, BAND_M_ROW:BAND_M_ROW + 1] for t in tiles]
        top = jnp.maximum(jnp.maximum(tops[0], tops[1]), tops[2])
        ws = [jnp.exp2(m - top) for m in tops]
        num = sum(w * t[:, :HEAD_DIM] for w, t in zip(ws, tiles))
        den = sum(w * t[:, BAND_L_ROW:BAND_L_ROW + 1] for w, t in zip(ws, tiles))
        merged = (num / den).astype(BF16)
        acc = acc + jnp.dot(merged, wb_ref[h * HEAD_DIM:(h + 1) * HEAD_DIM, :], preferred_element_type=F32)
    out_ref[...] = acc


def _even_out(x3, a3, accs, wa, wb, *, tm=512):
    b, s, d = x3.shape
    acc_w = HEADS_PER_TILE * BAND_ACC_ROWS
    full = lambda a: pl.BlockSpec(a.shape, lambda bi, i: (0, 0))

    def acc_spec(a):
        dil = a.shape[1]
        return pl.BlockSpec((None, dil, tm // dil, acc_w), lambda bi, i: (bi, 0, i, 0))

    scratch = [pltpu.VMEM((HEADS_PER_TILE, tm, BAND_ACC_ROWS), F32) for a in accs if a.shape[1] > 1]
    return pl.pallas_call(
        _even_out_body,
        out_shape=jax.ShapeDtypeStruct((b, s, d), F32),
        grid=(b, s // tm),
        in_specs=[pl.BlockSpec((None, tm, d), lambda bi, i: (bi, i, 0)),
                  pl.BlockSpec((None, tm, GMLP_WIDTH), lambda bi, i: (bi, i, 0))]
        + [acc_spec(a) for a in accs] + [full(wa), full(wb)],
        out_specs=pl.BlockSpec((None, tm, d), lambda bi, i: (bi, i, 0)),
        scratch_shapes=scratch,
        compiler_params=_params("parallel", "parallel"),
        name="even_out_proj",
    )(x3, a3, *accs, wa, wb)


def _swiglu_step(h_bf16, wg, wu, wd):
    g = jnp.dot(h_bf16, wg, preferred_element_type=F32)
    u = jnp.dot(h_bf16, wu, preferred_element_type=F32)
    a = (g * jax.nn.sigmoid(g) * u).astype(BF16)
    return jnp.dot(a, wd, preferred_element_type=F32)


def _ffn_body(x_ref, g_ref, wg_ref, wu_ref, wd_ref, o_ref, *, tf):
    x = x_ref[...]
    h = _row_norm(x, g_ref[...]).astype(BF16)
    acc = x
    for c in range(wg_ref.shape[1] // tf):
        sl = slice(c * tf, (c + 1) * tf)
        acc = acc + _swiglu_step(h, wg_ref[:, sl], wu_ref[:, sl], wd_ref[sl, :])
    o_ref[...] = acc


def _ffn(x2, gain, wg, wu, wd, *, tm=512, tf=256):
    t, d = x2.shape
    resident = lambda a: pl.BlockSpec(a.shape, lambda i: (0, 0), pipeline_mode=pl.Buffered(1))
    return pl.pallas_call(
        functools.partial(_ffn_body, tf=tf),
        out_shape=jax.ShapeDtypeStruct((t, d), F32),
        grid=(t // tm,),
        in_specs=[
            pl.BlockSpec((tm, d), lambda i: (i, 0)),
            pl.BlockSpec((1, d), lambda i: (0, 0)),
            resident(wg), resident(wu), resident(wd),
        ],
        out_specs=pl.BlockSpec((tm, d), lambda i: (i, 0)),
        compiler_params=_params("parallel"),
        name="ffn_swiglu",
    )(x2, gain.reshape(1, d), wg, wu, wd)


def _expert_body(te_ref, first_ref, nv_ref, x_ref, wg_ref, wu_ref, wd_ref, o_ref,
                 h_s, acc_s, cg_s, cu_s, cd_s):
    i = pl.program_id(0)
    f = pl.program_id(1)

    @pl.when(first_ref[i] == 1)
    def _():
        cg_s[f] = wg_ref[...].astype(BF16)
        cu_s[f] = wu_ref[...].astype(BF16)
        cd_s[f] = wd_ref[...].astype(BF16)

    @pl.when(f == 0)
    def _():
        h_s[...] = x_ref[...].astype(BF16)
        acc_s[...] = jnp.zeros_like(acc_s)

    @pl.when(i < nv_ref[0])
    def _():
        acc_s[...] += _swiglu_step(h_s[...], cg_s[f], cu_s[f], cd_s[f])

    @pl.when(f == pl.num_programs(1) - 1)
    def _():
        o_ref[...] = acc_s[...]


def _expert_ffn(xs, tile_expert, tile_first, n_live, wg, wu, wd, *, layer, tm, tf=512):
    p, d = xs.shape
    ff = wg.shape[-1]
    nf = ff // tf

    def chunk(i, f, first):
        return jnp.where(first[i] == 1, f, nf - 1)

    def w_in(i, f, te, first, nv):
        return (layer, te[i], 0, chunk(i, f, first))

    def w_out(i, f, te, first, nv):
        return (layer, te[i], chunk(i, f, first), 0)

    row = lambda i, f, te, first, nv: (i, 0)
    return pl.pallas_call(
        _expert_body,
        out_shape=jax.ShapeDtypeStruct((p, d), F32),
        grid_spec=pltpu.PrefetchScalarGridSpec(
            num_scalar_prefetch=3,
            grid=(p // tm, nf),
            in_specs=[
                pl.BlockSpec((tm, d), row),
                pl.BlockSpec((None, None, d, tf), w_in),
                pl.BlockSpec((None, None, d, tf), w_in),
                pl.BlockSpec((None, None, tf, d), w_out),
            ],
            out_specs=pl.BlockSpec((tm, d), row),
            scratch_shapes=[
                pltpu.VMEM((tm, d), BF16),
                pltpu.VMEM((tm, d), F32),
                pltpu.VMEM((nf, d, tf), BF16),
                pltpu.VMEM((nf, d, tf), BF16),
                pltpu.VMEM((nf, tf, d), BF16),
            ],
        ),
        compiler_params=_params("arbitrary", "arbitrary"),
        name="expert_swiglu",
    )(tile_expert, tile_first, n_live, xs, wg, wu, wd)


def _moba_body(q_ref, k_ref, vt_ref, km_ref, o_ref, sel_s, qh_s, sa_s, sb_s, m_s, out_s):
    i = pl.program_id(2)
    nblk = km_ref.shape[0]
    lane = _lane_ids()
    qn = q_ref[...]
    blk = lax.broadcasted_iota(I32, (nblk, MOBA_BLOCK), 0).astype(F32)
    first_blk = i.astype(F32)
    km = km_ref[...]
    krow = lax.broadcasted_iota(I32, (MOBA_BLOCK, MOBA_BLOCK), 0)
    qcol = lax.broadcasted_iota(I32, (MOBA_BLOCK, MOBA_BLOCK), 1)
    causal = krow <= qcol

    km_hi = km.astype(BF16)
    km_lo = (km - km_hi.astype(F32)).astype(BF16)
    acc_rows = [slice(h * MOBA_ACC_ROWS, (h + 1) * MOBA_ACC_ROWS) for h in range(HEADS_PER_TILE)]

    def keys(j):
        return k_ref[pl.ds(pl.multiple_of(j * MOBA_BLOCK, MOBA_BLOCK), MOBA_BLOCK), :]

    heads = range(HEADS_PER_TILE)
    qh = [jnp.where(lane // HEAD_DIM == h, qn, jnp.zeros_like(qn)) for h in heads]
    for h in heads:
        qh_s[h] = qh[h]
    gates = [lax.dot_general(km_hi, qh[h], _NT, preferred_element_type=F32)
             + lax.dot_general(km_lo, qh[h], _NT, preferred_element_type=F32) for h in heads]
    k_own = keys(i)
    own = [lax.dot_general(k_own, qh[h], _NT, preferred_element_type=F32) for h in heads]
    k_first = keys(0)
    for h in heads:
        sa_s[h] = lax.dot_general(k_first, qh[h], _NT, preferred_element_type=F32)

    for h in heads:
        gate = jnp.where(blk < first_blk, gates[h], NEG_INF)
        sel = jnp.zeros((nblk, MOBA_BLOCK), F32)
        for _ in range(MOBA_TOPK):
            top = jnp.max(gate, axis=0, keepdims=True)
            idx = jnp.min(jnp.where(gate == top, blk, float(nblk)), axis=0, keepdims=True)
            hit = blk == idx
            sel = jnp.where(hit & (top > NEG_INF), 1.0, sel)
            gate = jnp.where(hit, NEG_INF, gate)
        sel_s[h] = sel

    for h in heads:
        s = jnp.where(causal, own[h], NEG_INF)
        m0 = jnp.max(s, axis=0, keepdims=True)
        m_s[h] = m0
        out_s[acc_rows[h], :] = jnp.dot(vt_ref[i, acc_rows[h], :], jnp.exp2(s - m0).astype(BF16),
                                        preferred_element_type=F32)

    def stage(j_next, buf_next, j, buf):
        k_next = keys(jnp.minimum(j_next, nblk - 1))
        jc = jnp.minimum(j, nblk - 1)
        bar = jnp.where(j < i, 0.0, 2.0)
        for h in heads:
            buf_next[h] = lax.dot_general(k_next, qh_s[h], _NT, preferred_element_type=F32)
        for h in heads:
            bias = jnp.where(sel_s[h, pl.ds(jc, 1), :] > bar, 0.0, NEG_INF)
            s = buf[h]
            m = m_s[h]
            m_new = jnp.maximum(m, jnp.max(s, axis=0, keepdims=True) + bias)
            p = jnp.exp2(s + (bias - m_new)).astype(BF16)
            m_s[h] = m_new
            out_s[acc_rows[h], :] = (jnp.exp2(m - m_new) * out_s[acc_rows[h], :]
                                     + jnp.dot(vt_ref[jc, acc_rows[h], :], p, preferred_element_type=F32))

    def step(t, carry):
        stage(2 * t + 1, sb_s, 2 * t, sa_s)
        stage(2 * t + 2, sa_s, 2 * t + 1, sb_s)
        return carry

    lax.fori_loop(0, (i + 1) // 2, step, 0)

    outs = []
    for h in heads:
        acc = out_s[acc_rows[h], :]
        outs.append(acc[:HEAD_DIM, :] / acc[HEAD_DIM:HEAD_DIM + 1, :])
    o_ref[...] = jnp.concatenate(outs, axis=0).T.astype(o_ref.dtype)


def _moba(qp, kp, vtp, kmean):
    b, tiles, _, s, _ = qp.shape
    nblk = s // MOBA_BLOCK
    acc_w = HEADS_PER_TILE * MOBA_ACC_ROWS
    return pl.pallas_call(
        _moba_body,
        out_shape=jax.ShapeDtypeStruct((b, s, MOBA_WIDTH), BF16),
        grid=(b, tiles, nblk),
        in_specs=[
            pl.BlockSpec((None, None, None, MOBA_BLOCK, LANE_TILE), lambda bi, hg, i: (bi, hg, 0, i, 0)),
            pl.BlockSpec((None, None, None, s, LANE_TILE), lambda bi, hg, i: (bi, hg, 0, 0, 0)),
            pl.BlockSpec((None, None, None, nblk, acc_w, MOBA_BLOCK), lambda bi, hg, i: (bi, hg, 0, 0, 0, 0)),
            pl.BlockSpec((None, None, nblk, LANE_TILE), lambda bi, hg, i: (bi, hg, 0, 0)),
        ],
        out_specs=pl.BlockSpec((None, MOBA_BLOCK, LANE_TILE), lambda bi, hg, i: (bi, i, hg)),
        scratch_shapes=[
            pltpu.VMEM((HEADS_PER_TILE, nblk, MOBA_BLOCK), F32),
            pltpu.VMEM((HEADS_PER_TILE, MOBA_BLOCK, LANE_TILE), BF16),
            pltpu.VMEM((HEADS_PER_TILE, MOBA_BLOCK, MOBA_BLOCK), F32),
            pltpu.VMEM((HEADS_PER_TILE, MOBA_BLOCK, MOBA_BLOCK), F32),
            pltpu.VMEM((HEADS_PER_TILE, 1, MOBA_BLOCK), F32),
            pltpu.VMEM((HEADS_PER_TILE * MOBA_ACC_ROWS, MOBA_BLOCK), F32),
        ],
        compiler_params=_params("parallel", "parallel", "arbitrary"),
        name="moba_attn",
    )(qp, kp, vtp, kmean)


def _conv_body(bg_ref, cg_ref, xz_ref, w_ref, o_ref, *, ts):
    halo = 16
    w = w_ref[...]
    for c in range(o_ref.shape[0] // ts):
        lo = c * ts
        if c == 0:
            z = cg_ref[0:ts, :].astype(F32) * xz_ref[0:ts, :].astype(F32)
            zp = jnp.concatenate([jnp.zeros((halo, CONV_WIDTH), F32), z], axis=0)
        else:
            zp = cg_ref[lo - halo:lo + ts, :].astype(F32) * xz_ref[lo - halo:lo + ts, :].astype(F32)
        z1 = pltpu.roll(zp, 1, 0)[halo:]
        z2 = pltpu.roll(zp, 2, 0)[halo:]
        y = w[2:3, :] * zp[halo:] + w[1:2, :] * z1 + w[0:1, :] * z2
        o_ref[lo:lo + ts, :] = (bg_ref[lo:lo + ts, :].astype(F32) * y).astype(o_ref.dtype)


def _short_conv(p3, conv_w, *, col0, ts=512):
    b, s, _ = p3.shape
    taps = conv_w.shape[0]

    def col(c):
        return pl.BlockSpec((None, s, CONV_WIDTH), lambda bi: (bi, 0, col0 + c))

    return pl.pallas_call(
        functools.partial(_conv_body, ts=ts),
        out_shape=jax.ShapeDtypeStruct((b, s, CONV_WIDTH), BF16),
        grid=(b,),
        in_specs=[col(0), col(1), col(2), pl.BlockSpec((taps, CONV_WIDTH), lambda bi: (0, 0))],
        out_specs=pl.BlockSpec((None, s, CONV_WIDTH), lambda bi: (bi, 0, 0)),
        compiler_params=_params("parallel"),
        name="short_conv",
    )(p3, p3, p3, conv_w)


def _odd_out_body(x_ref, c_ref, d_ref, wc_ref, wd_ref, gain_ref, rw_ref, xo_ref, h_ref, route_ref):
    x = x_ref[...] + jnp.dot(c_ref[...], wc_ref[...], preferred_element_type=F32)
    x = x + jnp.dot(d_ref[...], wd_ref[...], preferred_element_type=F32)
    xo_ref[...] = x
    h = _row_norm(x, gain_ref[...])
    h_ref[...] = h
    rw = rw_ref[...]
    h_hi = h.astype(BF16)
    h_lo = (h - h_hi.astype(F32)).astype(BF16)
    rw_hi = rw.astype(BF16)
    rw_lo = (rw - rw_hi.astype(F32)).astype(BF16)
    logits = (jnp.dot(h_hi, rw_hi, preferred_element_type=F32) + jnp.dot(h_lo, rw_hi, preferred_element_type=F32)
              + jnp.dot(h_hi, rw_lo, preferred_element_type=F32))
    lane = lax.broadcasted_iota(I32, logits.shape, 1)
    lane_f = lane.astype(F32)
    logits = jnp.where(lane < N_EXPERTS, logits, NEG_INF)
    v1 = jnp.max(logits, axis=-1, keepdims=True)
    i1 = jnp.min(jnp.where(logits == v1, lane_f, float(ROUTE_LANES)), axis=-1, keepdims=True)
    rest = jnp.where(lane_f == i1, NEG_INF, logits)
    v2 = jnp.max(rest, axis=-1, keepdims=True)
    i2 = jnp.min(jnp.where(rest == v2, lane_f, float(ROUTE_LANES)), axis=-1, keepdims=True)
    e = jnp.exp(v2 - v1)
    g1 = 1.0 / (1.0 + e)
    g2 = e / (1.0 + e)
    route = jnp.where(lane == 0, i1, 0.0)
    route = jnp.where(lane == 1, i2, route)
    route = jnp.where(lane == 2, g1, route)
    route_ref[...] = jnp.where(lane == 3, g2, route)


def _odd_out(x2, c2, d2, wc, wd, gain, router_pad, *, tm=512):
    t, d = x2.shape
    row = lambda w: pl.BlockSpec((tm, w), lambda i: (i, 0))
    full = lambda a: pl.BlockSpec(a.shape, lambda i: (0, 0))
    return pl.pallas_call(
        _odd_out_body,
        out_shape=[jax.ShapeDtypeStruct((t, d), F32), jax.ShapeDtypeStruct((t, d), F32),
                   jax.ShapeDtypeStruct((t, ROUTE_LANES), F32)],
        grid=(t // tm,),
        in_specs=[row(d), row(MOBA_WIDTH), row(CONV_WIDTH), full(wc), full(wd),
                  pl.BlockSpec((1, d), lambda i: (0, 0)), full(router_pad)],
        out_specs=[row(d), row(d), row(ROUTE_LANES)],
        compiler_params=_params("parallel"),
        name="odd_out_proj_router",
    )(x2, c2, d2, wc, wd, gain.reshape(1, d), router_pad)


def _row_copy(src_ref, src_row, dst_ref, dst_row, sem):
    return pltpu.make_async_copy(src_ref.at[pl.ds(src_row, 1)], dst_ref.at[pl.ds(dst_row, 1)], sem)


def _dispatch_body(pos_ref, h_ref, init_ref, xs_ref, sem):
    del init_ref
    td = h_ref.shape[0]

    def copies(r):
        return [_row_copy(h_ref, r, xs_ref, pos_ref[0, 0, k * td + r], sem) for k in range(2)]

    def start(r, c):
        for cp in copies(r):
            cp.start()
        return c

    lax.fori_loop(0, td, start, 0, unroll=DMA_UNROLL)
    for _ in range(2):
        pltpu.make_async_copy(h_ref, xs_ref.at[pl.ds(0, td)], sem).wait()


def _dispatch(h2, pos3, n_rows, *, td):
    t, d = h2.shape
    return pl.pallas_call(
        _dispatch_body,
        out_shape=jax.ShapeDtypeStruct((n_rows, d), F32),
        grid=(t // td,),
        in_specs=[
            pl.BlockSpec((1, 1, 2 * td), lambda i: (i, 0, 0), memory_space=pltpu.SMEM),
            pl.BlockSpec((td, d), lambda i: (i, 0)),
            pl.BlockSpec(memory_space=pl.ANY),
        ],
        out_specs=pl.BlockSpec(memory_space=pl.ANY),
        scratch_shapes=[pltpu.SemaphoreType.DMA(())],
        input_output_aliases={2: 0},
        compiler_params=_params("arbitrary"),
        name="expert_dispatch",
    )(pos3, h2, jnp.zeros((n_rows, d), F32))


def _combine_body(pos_ref, x_ref, route_ref, y_ref, o_ref, buf, sem):
    td = x_ref.shape[0]

    def copies(r):
        return [_row_copy(y_ref, pos_ref[0, 0, k * td + r], buf.at[k], r, sem) for k in range(2)]

    def start(r, c):
        for cp in copies(r):
            cp.start()
        return c

    lax.fori_loop(0, td, start, 0, unroll=DMA_UNROLL)
    for k in range(2):
        pltpu.make_async_copy(y_ref.at[pl.ds(0, td)], buf.at[k], sem).wait()
    route = route_ref[...]
    o_ref[...] = x_ref[...] + route[:, 2:3] * buf[0] + route[:, 3:4] * buf[1]


def _combine(x2, route, ys, pos3, *, td):
    t, d = x2.shape
    return pl.pallas_call(
        _combine_body,
        out_shape=jax.ShapeDtypeStruct((t, d), F32),
        grid=(t // td,),
        in_specs=[
            pl.BlockSpec((1, 1, 2 * td), lambda i: (i, 0, 0), memory_space=pltpu.SMEM),
            pl.BlockSpec((td, d), lambda i: (i, 0)),
            pl.BlockSpec((td, ROUTE_LANES), lambda i: (i, 0)),
            pl.BlockSpec(memory_space=pl.ANY),
        ],
        out_specs=pl.BlockSpec((td, d), lambda i: (i, 0)),
        scratch_shapes=[pltpu.VMEM((2, td, d), F32), pltpu.SemaphoreType.DMA(())],
        compiler_params=_params("arbitrary"),
        name="expert_combine",
    )(pos3, x2, route, ys)


def _route_plan(route, *, tm, td):
    t = route.shape[0]
    experts = jnp.concatenate([route[:, 0], route[:, 1]]).astype(I32)
    onehot = (experts[:, None] == jnp.arange(N_EXPERTS, dtype=I32)[None, :]).astype(I32)
    running = jnp.cumsum(onehot, axis=0)
    counts = running[-1]
    padded = ((counts + tm - 1) // tm) * tm
    ends = jnp.cumsum(padded)
    starts = ends - padded
    pos = jnp.sum(onehot * (starts[None, :] + running - 1), axis=1)
    n_tiles = (2 * t) // tm + N_EXPERTS
    tile_lo = jnp.arange(n_tiles, dtype=I32) * tm
    tile_expert = jnp.sum((tile_lo[:, None] >= ends[None, :]).astype(I32), axis=1)
    n_live = (ends[-1] // tm).astype(I32)
    last = tile_expert[jnp.maximum(n_live - 1, 0)]
    live = jnp.arange(n_tiles) < n_live
    tile_expert = jnp.where(live, tile_expert, last)
    prev = jnp.concatenate([jnp.full((1,), -1, I32), tile_expert[:-1]])
    tile_first = (live & (tile_expert != prev)).astype(I32)
    pos3 = pos.reshape(2, t // td, td).transpose(1, 0, 2).reshape(t // td, 1, 2 * td)
    return pos3, tile_expert, tile_first, n_live.reshape(1), n_tiles * tm


def _rope_tables(seq):
    inv = 1.0 / (ROPE_THETA ** (jnp.arange(0, HEAD_DIM, 2, dtype=F32) / HEAD_DIM))
    ang = jnp.arange(seq, dtype=F32)[:, None] * inv[None, :]
    ang = jnp.concatenate([ang, ang], axis=-1)
    tile = lambda a: jnp.tile(a, (1, HEADS_PER_TILE))
    return tile(jnp.cos(ang)), tile(jnp.sin(ang))


def _tile_gain(g):
    return jnp.tile(g, HEADS_PER_TILE).reshape(1, LANE_TILE)


def _even_layer(x2, b, s, cos_t, sin_t, norm_mix, norm_ffn, w_in, v_gain, ws, bs, gq, gk, w_out, wg, wu, wd):
    p = _norm_proj(x2, norm_mix, w_in.astype(BF16))
    p3 = p.reshape(b, s, p.shape[1])
    a = _gmlp(p3, v_gain, ws, bs)
    q0 = 2 * GMLP_WIDTH // LANE_TILE
    n_groups = len(DIL_CONFIGS)
    accs = []
    for group, (window, dil) in enumerate(DIL_CONFIGS):
        assert window // dil == BAND
        qg, kg, vtg = _qkv_prep(p3, cos_t, sin_t, _tile_gain(gq), _tile_gain(gk), q_col=q0 + group,
                                k_col=q0 + n_groups + group, v_col=q0 + 2 * n_groups + group, n_ht=1, dil=dil,
                                tt=max(4, dil) * BAND, vblock=BAND, head_rows=BAND_ACC_ROWS, with_kmean=False)
        accs.append(_band_attn(qg, kg, vtg))
    w_out = w_out.astype(BF16)
    x3 = _even_out(x2.reshape(b, s, -1), a, accs, w_out[:GMLP_WIDTH], w_out[GMLP_WIDTH:])
    return _ffn(x3.reshape(b * s, -1), norm_ffn, wg.astype(BF16), wu.astype(BF16), wd.astype(BF16))


def _odd_layer(x2, b, s, cos_t, sin_t, norm_mix, norm_ffn, w_in, gq, gk, conv_w, w_out, router_w, wg, wu, wd,
               *, layer, tm=512, td=256):
    p = _norm_proj(x2, norm_mix, w_in.astype(BF16))
    p3 = p.reshape(b, s, p.shape[1])
    tiles = MOBA_WIDTH // LANE_TILE
    qp, kp, vtp, kmean = _qkv_prep(p3, cos_t, sin_t, _tile_gain(gq), _tile_gain(gk), q_col=0, k_col=tiles,
                                   v_col=2 * tiles, n_ht=tiles, dil=1, tt=8 * MOBA_BLOCK, vblock=MOBA_BLOCK,
                                   head_rows=MOBA_ACC_ROWS, with_kmean=True)
    c = _moba(qp, kp, vtp, kmean)
    dconv = _short_conv(p3, conv_w, col0=3 * MOBA_WIDTH // CONV_WIDTH)
    w_out = w_out.astype(BF16)
    router_pad = jnp.pad(router_w, ((0, 0), (0, ROUTE_LANES - N_EXPERTS)))
    x2, h2, route = _odd_out(x2, c.reshape(b * s, MOBA_WIDTH), dconv.reshape(b * s, CONV_WIDTH),
                             w_out[:MOBA_WIDTH], w_out[MOBA_WIDTH:], norm_ffn, router_pad)
    pos3, tile_expert, tile_first, n_live, n_rows = _route_plan(route, tm=tm, td=td)
    xs = _dispatch(h2, pos3, n_rows, td=td)
    ys = _expert_ffn(xs, tile_expert, tile_first, n_live, wg, wu, wd, layer=layer, tm=tm)
    return _combine(x2, route, ys, pos3, td=td)


def kernel(x, norm_mix, norm_ffn, w_in_ab, gmlp_v_gain, gmlp_ws, gmlp_bs, dil_q_gain, dil_k_gain, w_out_ab,
           ffn_w_gate, ffn_w_up, ffn_w_down, w_in_cd, moba_q_gain, moba_k_gain, conv_w, w_out_cd, router_w,
           moe_w_gate, moe_w_up, moe_w_down):
    b, s, d = x.shape
    depth = norm_mix.shape[0]
    cos_t, sin_t = _rope_tables(s)
    x2 = x.reshape(b * s, d)
    for layer in range(depth):
        i = layer // 2
        if layer % 2 == 0:
            x2 = _even_layer(x2, b, s, cos_t, sin_t, norm_mix[layer], norm_ffn[layer], w_in_ab[i], gmlp_v_gain[i],
                             gmlp_ws[i], gmlp_bs[i], dil_q_gain[i], dil_k_gain[i], w_out_ab[i], ffn_w_gate[i],
                             ffn_w_up[i], ffn_w_down[i])
        else:
            x2 = _odd_layer(x2, b, s, cos_t, sin_t, norm_mix[layer], norm_ffn[layer], w_in_cd[i], moba_q_gain[i],
                            moba_k_gain[i], conv_w[i], w_out_cd[i], router_w[i], moe_w_gate, moe_w_up,
                            moe_w_down, layer=i)
    return x2.reshape(b, s, d)
```

```python
import functools

import jax
import jax.numpy as jnp
from jax import lax
from jax.experimental import pallas as pl
from jax.experimental.pallas import tpu as pltpu

F32 = jnp.float32
BF16 = jnp.bfloat16
I32 = jnp.int32

D_MODEL = 1024
HEAD_DIM = 64
ROPE_THETA = 10000.0
NORM_EPS = 1e-6
LANE_TILE = 256
HEADS_PER_TILE = LANE_TILE // HEAD_DIM
GMLP_WIDTH = 256
GMLP_CHUNK = 128
DIL_CONFIGS = ((128, 1), (512, 4), (2048, 16))
DIL_WIDTH = 768
BAND = 128
BAND_ACC_ROWS = 128
BAND_L_ROW = HEAD_DIM
BAND_M_ROW = HEAD_DIM + 1
BAND_LOOKAHEAD = 5
STRIDE_LANES = 128
MOBA_WIDTH = 768
MOBA_BLOCK = 256
MOBA_TOPK = 3
MOBA_ACC_ROWS = HEAD_DIM + 16
LOG2_E = 1.4426950408889634
MOBA_M_INIT = -1e30
CONV_WIDTH = 256
N_EXPERTS = 8
ROUTE_LANES = 128
NEG_INF = float("-inf")
DMA_UNROLL = 8

_NT = (((1,), (1,)), ((), ()))


def _params(*sem):
    return pltpu.CompilerParams(dimension_semantics=tuple(sem), vmem_limit_bytes=56 * 1024 * 1024)


def _lane_ids():
    return lax.broadcasted_iota(I32, (1, LANE_TILE), 1)


def _head_block_diag():
    r = lax.broadcasted_iota(I32, (LANE_TILE, LANE_TILE), 0) // HEAD_DIM
    c = lax.broadcasted_iota(I32, (LANE_TILE, LANE_TILE), 1) // HEAD_DIM
    return jnp.where(r == c, 1.0, 0.0).astype(BF16)


def _head_mean_sq(x, bd):
    s = x * x
    hi = s.astype(BF16)
    lo = (s - hi.astype(F32)).astype(BF16)
    tot = jnp.dot(hi, bd, preferred_element_type=F32) + jnp.dot(lo, bd, preferred_element_type=F32)
    return tot * (1.0 / HEAD_DIM)


def _head_norm(x, gain, bd):
    return x * lax.rsqrt(_head_mean_sq(x, bd) + NORM_EPS) * gain


def _rope(x, cos, sin, lane):
    half = HEAD_DIM // 2
    x_up = pltpu.roll(x, LANE_TILE - half, 1)
    x_dn = pltpu.roll(x, half, 1)
    rot = jnp.where((lane % HEAD_DIM) < half, -x_up, x_dn)
    return x * cos + rot * sin


def _row_norm(x, gain):
    return x * lax.rsqrt(jnp.mean(x * x, axis=-1, keepdims=True) + NORM_EPS) * gain


def _norm_proj_body(x_ref, g_ref, w_ref, o_ref, *, tn):
    hb = _row_norm(x_ref[...], g_ref[...]).astype(BF16)
    for c in range(o_ref.shape[-1] // tn):
        sl = slice(c * tn, (c + 1) * tn)
        o_ref[:, sl] = jnp.dot(hb, w_ref[:, sl], preferred_element_type=F32).astype(o_ref.dtype)


def _norm_proj(x2, gain, w_bf16, *, tm=512, tn=256):
    t, d = x2.shape
    n = w_bf16.shape[1]
    return pl.pallas_call(
        functools.partial(_norm_proj_body, tn=tn),
        out_shape=jax.ShapeDtypeStruct((t, n), BF16),
        grid=(t // tm,),
        in_specs=[
            pl.BlockSpec((tm, d), lambda i: (i, 0)),
            pl.BlockSpec((1, d), lambda i: (0, 0)),
            pl.BlockSpec((d, n), lambda i: (0, 0)),
        ],
        out_specs=pl.BlockSpec((tm, n), lambda i: (i, 0)),
        compiler_params=_params("parallel"),
        name="norm_proj",
    )(x2, gain.reshape(1, d), w_bf16)


def _gmlp_body(u_ref, v_ref, gain_ref, ws_ref, bias_ref, o_ref):
    lane = _lane_ids()
    bd = _head_block_diag()
    r = lax.broadcasted_iota(I32, (GMLP_CHUNK, GMLP_CHUNK), 0)
    c = lax.broadcasted_iota(I32, (GMLP_CHUNK, GMLP_CHUNK), 1)
    w_tril = [jnp.where(r >= c, ws_ref[g], 0.0).astype(BF16) for g in range(HEADS_PER_TILE)]
    for ch in range(o_ref.shape[0] // GMLP_CHUNK):
        sl = slice(ch * GMLP_CHUNK, (ch + 1) * GMLP_CHUNK)
        u = jax.nn.gelu(u_ref[sl, :].astype(F32))
        v = jax.nn.gelu(v_ref[sl, :].astype(F32))
        vn = _head_norm(v, gain_ref[...], bd).astype(BF16)
        y = bias_ref[...]
        for g in range(HEADS_PER_TILE):
            yg = jnp.dot(w_tril[g], vn, preferred_element_type=F32)
            y = y + jnp.where(lane // HEAD_DIM == g, yg, 0.0)
        o_ref[sl, :] = (u * y).astype(o_ref.dtype)


def _gmlp(p3, v_gain, ws, bs, *, tc=512):
    b, s, _ = p3.shape
    groups, chunk = bs.shape
    bias = jnp.repeat(bs.T, HEAD_DIM, axis=1)
    return pl.pallas_call(
        _gmlp_body,
        out_shape=jax.ShapeDtypeStruct((b, s, GMLP_WIDTH), BF16),
        grid=(b, s // tc),
        in_specs=[
            pl.BlockSpec((None, tc, LANE_TILE), lambda bi, i: (bi, i, 0)),
            pl.BlockSpec((None, tc, LANE_TILE), lambda bi, i: (bi, i, 1)),
            pl.BlockSpec((1, GMLP_WIDTH), lambda bi, i: (0, 0)),
            pl.BlockSpec((groups, chunk, chunk), lambda bi, i: (0, 0, 0)),
            pl.BlockSpec((chunk, GMLP_WIDTH), lambda bi, i: (0, 0)),
        ],
        out_specs=pl.BlockSpec((None, tc, GMLP_WIDTH), lambda bi, i: (bi, i, 0)),
        compiler_params=_params("parallel", "parallel"),
        name="gmlp_gate",
    )(p3, p3, v_gain.reshape(1, GMLP_WIDTH), ws, bias)


def _residues(val, scr, dil):
    if dil == 1:
        return [val]
    rows = val.shape[0] // dil
    slabs = LANE_TILE // STRIDE_LANES
    for c in range(slabs):
        scr[c] = val[:, c * STRIDE_LANES:(c + 1) * STRIDE_LANES]
    return [jnp.concatenate([scr[c, pl.ds(r, rows, stride=dil), :] for c in range(slabs)], axis=1)
            for r in range(dil)]


def _qkv_prep_body(q_ref, k_ref, v_ref, cos_ref, sin_ref, gq_ref, gk_ref, *rest, dil, with_kmean):
    n_out = 4 if with_kmean else 3
    qo_ref, ko_ref, vt_ref = rest[:3]
    scr = list(rest[n_out:]) + [None] * 3
    lane = _lane_ids()
    bd = _head_block_diag()
    cos = cos_ref[...]
    sin = sin_ref[...]
    qn = _rope(_head_norm(q_ref[...].astype(F32), gq_ref[...], bd), cos, sin, lane) * (HEAD_DIM ** -0.5 * LOG2_E)
    kn = _rope(_head_norm(k_ref[...].astype(F32), gk_ref[...], bd), cos, sin, lane)
    if with_kmean:
        km_ref = rest[3]
        km_ref[...] = jnp.mean(kn.reshape(km_ref.shape[0], MOBA_BLOCK, LANE_TILE), axis=1)
    for r, piece in enumerate(_residues(qn, scr[0], dil)):
        qo_ref[r] = piece.astype(BF16)
    for r, piece in enumerate(_residues(kn, scr[1], dil)):
        ko_ref[r] = piece.astype(BF16)
    n_blocks, acc_w, vblock = vt_ref.shape[1:]
    head_rows = acc_w // HEADS_PER_TILE
    ones = jnp.ones((head_rows - HEAD_DIM, vblock), BF16)
    for r, piece in enumerate(_residues(v_ref[...].astype(F32), scr[2], dil)):
        for c in range(n_blocks):
            vt = piece[c * vblock:(c + 1) * vblock, :].T.astype(BF16)
            for h in range(HEADS_PER_TILE):
                vt_ref[r, c, h * head_rows:h * head_rows + HEAD_DIM, :] = vt[h * HEAD_DIM:(h + 1) * HEAD_DIM, :]
                vt_ref[r, c, h * head_rows + HEAD_DIM:(h + 1) * head_rows, :] = ones


def _qkv_prep(p3, cos_t, sin_t, gq, gk, *, q_col, k_col, v_col, n_ht, dil, tt, vblock, head_rows, with_kmean):
    b, s, _ = p3.shape
    length = s // dil
    tr = tt // dil
    acc_w = HEADS_PER_TILE * head_rows

    def col(c0):
        return pl.BlockSpec((None, tt, LANE_TILE), lambda bi, ht, i: (bi, i, c0 + ht))

    tab = pl.BlockSpec((tt, LANE_TILE), lambda bi, ht, i: (i, 0))
    gain = pl.BlockSpec((1, LANE_TILE), lambda bi, ht, i: (0, 0))
    qk_shape = jax.ShapeDtypeStruct((b, n_ht, dil, length, LANE_TILE), BF16)
    qk_spec = pl.BlockSpec((None, None, dil, tr, LANE_TILE), lambda bi, ht, i: (bi, ht, 0, i, 0))
    out_shape = [qk_shape, qk_shape,
                 jax.ShapeDtypeStruct((b, n_ht, dil, length // vblock, acc_w, vblock), BF16)]
    out_specs = [qk_spec, qk_spec,
                 pl.BlockSpec((None, None, dil, tr // vblock, acc_w, vblock), lambda bi, ht, i: (bi, ht, 0, i, 0, 0))]
    if with_kmean:
        out_shape.append(jax.ShapeDtypeStruct((b, n_ht, s // MOBA_BLOCK, LANE_TILE), F32))
        out_specs.append(pl.BlockSpec((None, None, tt // MOBA_BLOCK, LANE_TILE), lambda bi, ht, i: (bi, ht, i, 0)))
    scratch = [pltpu.VMEM((LANE_TILE // STRIDE_LANES, tt, STRIDE_LANES), F32)] * (3 if dil > 1 else 0)
    return pl.pallas_call(
        functools.partial(_qkv_prep_body, dil=dil, with_kmean=with_kmean),
        out_shape=out_shape,
        grid=(b, n_ht, s // tt),
        in_specs=[col(q_col), col(k_col), col(v_col), tab, tab, gain, gain],
        out_specs=out_specs,
        scratch_shapes=scratch,
        compiler_params=_params("parallel", "parallel", "parallel"),
        name=f"qkv_prep_d{dil}",
    )(p3, p3, p3, cos_t, sin_t, gq, gk)


def _band_attn_body(q_ref, k_ref, kp_ref, vt_ref, vtp_ref, o_ref):
    i = pl.program_id(2)
    lb = q_ref.shape[0]
    lane = _lane_ids()
    kk = lax.broadcasted_iota(I32, (2 * BAND, BAND), 0)
    qq = lax.broadcasted_iota(I32, (2 * BAND, BAND), 1) + BAND
    in_band = (qq - kk >= 0) & (qq - kk <= BAND)
    first_ok = in_band & ((kk >= BAND) | (i > 0))
    acc_row = lax.broadcasted_iota(I32, (BAND_ACC_ROWS, BAND), 0)
    k_all = jnp.concatenate([kp_ref[...], k_ref[...]], axis=0)
    chains = [(j, h) for j in range(lb // BAND) for h in range(HEADS_PER_TILE)]

    def scores(j, h):
        qj = q_ref[j * BAND:(j + 1) * BAND, :]
        qh = jnp.where(lane // HEAD_DIM == h, qj, jnp.zeros_like(qj))
        return lax.dot_general(k_all[j * BAND:(j + 2) * BAND], qh, _NT, preferred_element_type=F32)

    staged = [scores(*c) for c in chains[:BAND_LOOKAHEAD]]
    tiles = []
    for n, (j, h) in enumerate(chains):
        if n + BAND_LOOKAHEAD < len(chains):
            staged.append(scores(*chains[n + BAND_LOOKAHEAD]))
        rows = slice(h * BAND_ACC_ROWS, (h + 1) * BAND_ACC_ROWS)
        s = jnp.where(first_ok if j == 0 else in_band, staged[n], NEG_INF)
        m = jnp.max(s, axis=0, keepdims=True)
        p = jnp.exp2(s - m).astype(BF16)
        v_prev = vtp_ref[0, rows, :] if j == 0 else vt_ref[j - 1, rows, :]
        vw = jnp.concatenate([v_prev, vt_ref[j, rows, :]], axis=1)
        acc = jnp.dot(vw, p, preferred_element_type=F32)
        tiles.append(jnp.where(acc_row == BAND_M_ROW, m, acc))
        if h == HEADS_PER_TILE - 1:
            o_ref[j * BAND:(j + 1) * BAND, :] = jnp.concatenate(tiles, axis=0).T
            tiles = []


def _band_attn(qg, kg, vtg):
    b, _, dil, length, _ = qg.shape
    lb = min(length, 512)
    sub = lb // BAND
    acc_w = HEADS_PER_TILE * BAND_ACC_ROWS
    before = lambda i: jnp.maximum(i * sub - 1, 0)
    cur = pl.BlockSpec((None, None, None, lb, LANE_TILE), lambda bi, r, i: (bi, 0, r, i, 0))
    prev = pl.BlockSpec((None, None, None, BAND, LANE_TILE), lambda bi, r, i: (bi, 0, r, before(i), 0))
    vt_cur = pl.BlockSpec((None, None, None, sub, acc_w, BAND), lambda bi, r, i: (bi, 0, r, i, 0, 0))
    vt_prev = pl.BlockSpec((None, None, None, 1, acc_w, BAND), lambda bi, r, i: (bi, 0, r, before(i), 0, 0))
    return pl.pallas_call(
        _band_attn_body,
        out_shape=jax.ShapeDtypeStruct((b, dil, length, acc_w), F32),
        grid=(b, dil, length // lb),
        in_specs=[cur, cur, prev, vt_cur, vt_prev],
        out_specs=pl.BlockSpec((None, None, lb, acc_w), lambda bi, r, i: (bi, r, i, 0)),
        compiler_params=_params("parallel", "parallel", "arbitrary"),
        name=f"band_attn_d{dil}",
    )(qg, kg, kg, vtg, vtg)


def _even_out_body(x_ref, a_ref, o0_ref, o1_ref, o2_ref, wa_ref, wb_ref, out_ref, *scratch):
    tm = x_ref.shape[0]

    def natural_order(o_ref, scr):
        dil = o_ref.shape[0]
        if dil == 1:
            return [o_ref[0, :, h * BAND_ACC_ROWS:(h + 1) * BAND_ACC_ROWS] for h in range(HEADS_PER_TILE)]
        for r in range(dil):
            for h in range(HEADS_PER_TILE):
                scr[h, pl.ds(r, tm // dil, stride=dil), :] = o_ref[r, :, h * BAND_ACC_ROWS:(h + 1) * BAND_ACC_ROWS]
        return [scr[h] for h in range(HEADS_PER_TILE)]

    o_refs = (o0_ref, o1_ref, o2_ref)
    spare = list(scratch)
    groups = [natural_order(o, spare.pop(0) if o.shape[0] > 1 else None) for o in o_refs]
    acc = x_ref[...] + jnp.dot(a_ref[...], wa_ref[...], preferred_element_type=F32)
    for h in range(HEADS_PER_TILE):
        tiles = [g[h] for g in groups]
        tops = [t[:, BAND_M_ROW:BAND_M_ROW + 1] for t in tiles]
        top = jnp.maximum(jnp.maximum(tops[0], tops[1]), tops[2])
        ws = [jnp.exp2(m - top) for m in tops]
        num = sum(w * t[:, :HEAD_DIM] for w, t in zip(ws, tiles))
        den = sum(w * t[:, BAND_L_ROW:BAND_L_ROW + 1] for w, t in zip(ws, tiles))
        merged = (num / den).astype(BF16)
        acc = acc + jnp.dot(merged, wb_ref[h * HEAD_DIM:(h + 1) * HEAD_DIM, :], preferred_element_type=F32)
    out_ref[...] = acc


def _even_out(x3, a3, accs, wa, wb, *, tm=512):
    b, s, d = x3.shape
    acc_w = HEADS_PER_TILE * BAND_ACC_ROWS
    full = lambda a: pl.BlockSpec(a.shape, lambda bi, i: (0, 0))

    def acc_spec(a):
        dil = a.shape[1]
        return pl.BlockSpec((None, dil, tm // dil, acc_w), lambda bi, i: (bi, 0, i, 0))

    scratch = [pltpu.VMEM((HEADS_PER_TILE, tm, BAND_ACC_ROWS), F32) for a in accs if a.shape[1] > 1]
    return pl.pallas_call(
        _even_out_body,
        out_shape=jax.ShapeDtypeStruct((b, s, d), F32),
        grid=(b, s // tm),
        in_specs=[pl.BlockSpec((None, tm, d), lambda bi, i: (bi, i, 0)),
                  pl.BlockSpec((None, tm, GMLP_WIDTH), lambda bi, i: (bi, i, 0))]
        + [acc_spec(a) for a in accs] + [full(wa), full(wb)],
        out_specs=pl.BlockSpec((None, tm, d), lambda bi, i: (bi, i, 0)),
        scratch_shapes=scratch,
        compiler_params=_params("parallel", "parallel"),
        name="even_out_proj",
    )(x3, a3, *accs, wa, wb)


def _swiglu_step(h_bf16, wg, wu, wd):
    g = jnp.dot(h_bf16, wg, preferred_element_type=F32)
    u = jnp.dot(h_bf16, wu, preferred_element_type=F32)
    a = (g * jax.nn.sigmoid(g) * u).astype(BF16)
    return jnp.dot(a, wd, preferred_element_type=F32)


def _ffn_body(x_ref, g_ref, wg_ref, wu_ref, wd_ref, o_ref, *, tf):
    x = x_ref[...]
    h = _row_norm(x, g_ref[...]).astype(BF16)
    acc = x
    for c in range(wg_ref.shape[1] // tf):
        sl = slice(c * tf, (c + 1) * tf)
        acc = acc + _swiglu_step(h, wg_ref[:, sl], wu_ref[:, sl], wd_ref[sl, :])
    o_ref[...] = acc


def _ffn(x2, gain, wg, wu, wd, *, tm=512, tf=256):
    t, d = x2.shape
    resident = lambda a: pl.BlockSpec(a.shape, lambda i: (0, 0), pipeline_mode=pl.Buffered(1))
    return pl.pallas_call(
        functools.partial(_ffn_body, tf=tf),
        out_shape=jax.ShapeDtypeStruct((t, d), F32),
        grid=(t // tm,),
        in_specs=[
            pl.BlockSpec((tm, d), lambda i: (i, 0)),
            pl.BlockSpec((1, d), lambda i: (0, 0)),
            resident(wg), resident(wu), resident(wd),
        ],
        out_specs=pl.BlockSpec((tm, d), lambda i: (i, 0)),
        compiler_params=_params("parallel"),
        name="ffn_swiglu",
    )(x2, gain.reshape(1, d), wg, wu, wd)


def _expert_body(te_ref, first_ref, nv_ref, x_ref, wg_hbm, wu_hbm, wd_hbm, o_ref,
                 cg_s, cu_s, cd_s, sg_s, su_s, sd_s, sems, *, layer, tf):
    i = pl.program_id(0)
    nf = cg_s.shape[0]
    e = te_ref[i]

    def chunk_copies(f, slot):
        cols = pl.ds(f * tf, tf)
        return (pltpu.make_async_copy(wg_hbm.at[layer, e, :, cols], sg_s.at[slot], sems.at[0, slot]),
                pltpu.make_async_copy(wu_hbm.at[layer, e, :, cols], su_s.at[slot], sems.at[1, slot]),
                pltpu.make_async_copy(wd_hbm.at[layer, e, cols, :], sd_s.at[slot], sems.at[2, slot]))

    def compute(load_chunk):
        h = x_ref[...].astype(BF16)
        acc = jnp.zeros(o_ref.shape, F32)
        for f in range(nf):
            load_chunk(f)
            acc = acc + _swiglu_step(h, cg_s[f], cu_s[f], cd_s[f])
        o_ref[...] = acc

    @pl.when(first_ref[i] == 1)
    def _():
        def load_chunk(f):
            if f + 1 < nf:
                for cp in chunk_copies(f + 1, (f + 1) % 2):
                    cp.start()
            for cp in chunk_copies(f, f % 2):
                cp.wait()
            cg_s[f] = sg_s[f % 2].astype(BF16)
            cu_s[f] = su_s[f % 2].astype(BF16)
            cd_s[f] = sd_s[f % 2].astype(BF16)

        for cp in chunk_copies(0, 0):
            cp.start()
        compute(load_chunk)

    @pl.when((first_ref[i] == 0) & (i < nv_ref[0]))
    def _():
        compute(lambda f: None)

    @pl.when(i >= nv_ref[0])
    def _():
        o_ref[...] = jnp.zeros_like(o_ref)


def _expert_ffn(xs, tile_expert, tile_first, n_live, wg, wu, wd, *, layer, tm, tf=512):
    p, d = xs.shape
    ff = wg.shape[-1]
    nf = ff // tf
    row = lambda i, te, first, nv: (i, 0)
    return pl.pallas_call(
        functools.partial(_expert_body, layer=layer, tf=tf),
        out_shape=jax.ShapeDtypeStruct((p, d), F32),
        grid_spec=pltpu.PrefetchScalarGridSpec(
            num_scalar_prefetch=3,
            grid=(p // tm,),
            in_specs=[
                pl.BlockSpec((tm, d), row),
                pl.BlockSpec(memory_space=pl.ANY),
                pl.BlockSpec(memory_space=pl.ANY),
                pl.BlockSpec(memory_space=pl.ANY),
            ],
            out_specs=pl.BlockSpec((tm, d), row),
            scratch_shapes=[
                pltpu.VMEM((nf, d, tf), BF16),
                pltpu.VMEM((nf, d, tf), BF16),
                pltpu.VMEM((nf, tf, d), BF16),
                pltpu.VMEM((2, d, tf), F32),
                pltpu.VMEM((2, d, tf), F32),
                pltpu.VMEM((2, tf, d), F32),
                pltpu.SemaphoreType.DMA((3, 2)),
            ],
        ),
        compiler_params=_params("arbitrary"),
        name="expert_swiglu",
    )(tile_expert, tile_first, n_live, xs, wg, wu, wd)


def _moba_body(q_ref, k_ref, vt_ref, km_ref, o_ref, sel_s, qh_s, sa_s, sb_s, m_s, out_s):
    i = pl.program_id(2)
    nblk = km_ref.shape[0]
    lane = _lane_ids()
    qn = q_ref[...]
    blk = lax.broadcasted_iota(I32, (nblk, MOBA_BLOCK), 0).astype(F32)
    first_blk = i.astype(F32)
    km = km_ref[...]
    krow = lax.broadcasted_iota(I32, (MOBA_BLOCK, MOBA_BLOCK), 0)
    qcol = lax.broadcasted_iota(I32, (MOBA_BLOCK, MOBA_BLOCK), 1)
    causal = krow <= qcol

    km_hi = km.astype(BF16)
    km_lo = (km - km_hi.astype(F32)).astype(BF16)
    acc_rows = [slice(h * MOBA_ACC_ROWS, (h + 1) * MOBA_ACC_ROWS) for h in range(HEADS_PER_TILE)]

    def keys(j):
        return k_ref[pl.ds(pl.multiple_of(j * MOBA_BLOCK, MOBA_BLOCK), MOBA_BLOCK), :]

    heads = range(HEADS_PER_TILE)
    qh = [jnp.where(lane // HEAD_DIM == h, qn, jnp.zeros_like(qn)) for h in heads]
    for h in heads:
        qh_s[h] = qh[h]
    gates = [lax.dot_general(km_hi, qh[h], _NT, preferred_element_type=F32)
             + lax.dot_general(km_lo, qh[h], _NT, preferred_element_type=F32) for h in heads]
    k_own = keys(i)
    own = [lax.dot_general(k_own, qh[h], _NT, preferred_element_type=F32) for h in heads]
    k_first = keys(0)
    for h in heads:
        sa_s[h] = lax.dot_general(k_first, qh[h], _NT, preferred_element_type=F32)

    for h in heads:
        gate = jnp.where(blk < first_blk, gates[h], NEG_INF)
        sel = jnp.zeros((nblk, MOBA_BLOCK), F32)
        for _ in range(MOBA_TOPK):
            top = jnp.max(gate, axis=0, keepdims=True)
            idx = jnp.min(jnp.where(gate == top, blk, float(nblk)), axis=0, keepdims=True)
            hit = blk == idx
            sel = jnp.where(hit & (top > NEG_INF), 1.0, sel)
            gate = jnp.where(hit, NEG_INF, gate)
        sel_s[h] = sel

    for h in heads:
        s = jnp.where(causal, own[h], NEG_INF)
        m0 = jnp.max(s, axis=0, keepdims=True)
        m_s[h] = m0
        out_s[acc_rows[h], :] = jnp.dot(vt_ref[i, acc_rows[h], :], jnp.exp2(s - m0).astype(BF16),
                                        preferred_element_type=F32)

    def stage(j_next, buf_next, j, buf):
        k_next = keys(jnp.minimum(j_next, nblk - 1))
        jc = jnp.minimum(j, nblk - 1)
        bar = jnp.where(j < i, 0.0, 2.0)
        for h in heads:
            buf_next[h] = lax.dot_general(k_next, qh_s[h], _NT, preferred_element_type=F32)
        for h in heads:
            bias = jnp.where(sel_s[h, pl.ds(jc, 1), :] > bar, 0.0, NEG_INF)
            s = buf[h]
            m = m_s[h]
            m_new = jnp.maximum(m, jnp.max(s, axis=0, keepdims=True) + bias)
            p = jnp.exp2(s + (bias - m_new)).astype(BF16)
            m_s[h] = m_new
            out_s[acc_rows[h], :] = (jnp.exp2(m - m_new) * out_s[acc_rows[h], :]
                                     + jnp.dot(vt_ref[jc, acc_rows[h], :], p, preferred_element_type=F32))

    def step(t, carry):
        stage(2 * t + 1, sb_s, 2 * t, sa_s)
        stage(2 * t + 2, sa_s, 2 * t + 1, sb_s)
        return carry

    lax.fori_loop(0, (i + 1) // 2, step, 0)

    outs = []
    for h in heads:
        acc = out_s[acc_rows[h], :]
        outs.append(acc[:HEAD_DIM, :] / acc[HEAD_DIM:HEAD_DIM + 1, :])
    o_ref[...] = jnp.concatenate(outs, axis=0).T.astype(o_ref.dtype)


def _moba(qp, kp, vtp, kmean):
    b, tiles, _, s, _ = qp.shape
    nblk = s // MOBA_BLOCK
    acc_w = HEADS_PER_TILE * MOBA_ACC_ROWS
    return pl.pallas_call(
        _moba_body,
        out_shape=jax.ShapeDtypeStruct((b, s, MOBA_WIDTH), BF16),
        grid=(b, tiles, nblk),
        in_specs=[
            pl.BlockSpec((None, None, None, MOBA_BLOCK, LANE_TILE), lambda bi, hg, i: (bi, hg, 0, i, 0)),
            pl.BlockSpec((None, None, None, s, LANE_TILE), lambda bi, hg, i: (bi, hg, 0, 0, 0)),
            pl.BlockSpec((None, None, None, nblk, acc_w, MOBA_BLOCK), lambda bi, hg, i: (bi, hg, 0, 0, 0, 0)),
            pl.BlockSpec((None, None, nblk, LANE_TILE), lambda bi, hg, i: (bi, hg, 0, 0)),
        ],
        out_specs=pl.BlockSpec((None, MOBA_BLOCK, LANE_TILE), lambda bi, hg, i: (bi, i, hg)),
        scratch_shapes=[
            pltpu.VMEM((HEADS_PER_TILE, nblk, MOBA_BLOCK), F32),
            pltpu.VMEM((HEADS_PER_TILE, MOBA_BLOCK, LANE_TILE), BF16),
            pltpu.VMEM((HEADS_PER_TILE, MOBA_BLOCK, MOBA_BLOCK), F32),
            pltpu.VMEM((HEADS_PER_TILE, MOBA_BLOCK, MOBA_BLOCK), F32),
            pltpu.VMEM((HEADS_PER_TILE, 1, MOBA_BLOCK), F32),
            pltpu.VMEM((HEADS_PER_TILE * MOBA_ACC_ROWS, MOBA_BLOCK), F32),
        ],
        compiler_params=_params("parallel", "parallel", "arbitrary"),
        name="moba_attn",
    )(qp, kp, vtp, kmean)


def _conv_body(bg_ref, cg_ref, xz_ref, w_ref, o_ref, *, ts):
    halo = 16
    w = w_ref[...]
    for c in range(o_ref.shape[0] // ts):
        lo = c * ts
        if c == 0:
            z = cg_ref[0:ts, :].astype(F32) * xz_ref[0:ts, :].astype(F32)
            zp = jnp.concatenate([jnp.zeros((halo, CONV_WIDTH), F32), z], axis=0)
        else:
            zp = cg_ref[lo - halo:lo + ts, :].astype(F32) * xz_ref[lo - halo:lo + ts, :].astype(F32)
        z1 = pltpu.roll(zp, 1, 0)[halo:]
        z2 = pltpu.roll(zp, 2, 0)[halo:]
        y = w[2:3, :] * zp[halo:] + w[1:2, :] * z1 + w[0:1, :] * z2
        o_ref[lo:lo + ts, :] = (bg_ref[lo:lo + ts, :].astype(F32) * y).astype(o_ref.dtype)


def _short_conv(p3, conv_w, *, col0, ts=512):
    b, s, _ = p3.shape
    taps = conv_w.shape[0]

    def col(c):
        return pl.BlockSpec((None, s, CONV_WIDTH), lambda bi: (bi, 0, col0 + c))

    return pl.pallas_call(
        functools.partial(_conv_body, ts=ts),
        out_shape=jax.ShapeDtypeStruct((b, s, CONV_WIDTH), BF16),
        grid=(b,),
        in_specs=[col(0), col(1), col(2), pl.BlockSpec((taps, CONV_WIDTH), lambda bi: (0, 0))],
        out_specs=pl.BlockSpec((None, s, CONV_WIDTH), lambda bi: (bi, 0, 0)),
        compiler_params=_params("parallel"),
        name="short_conv",
    )(p3, p3, p3, conv_w)


def _odd_out_body(x_ref, c_ref, d_ref, wc_ref, wd_ref, gain_ref, rw_ref, xo_ref, h_ref, route_ref):
    x = x_ref[...] + jnp.dot(c_ref[...], wc_ref[...], preferred_element_type=F32)
    x = x + jnp.dot(d_ref[...], wd_ref[...], preferred_element_type=F32)
    xo_ref[...] = x
    h = _row_norm(x, gain_ref[...])
    h_ref[...] = h
    rw = rw_ref[...]
    h_hi = h.astype(BF16)
    h_lo = (h - h_hi.astype(F32)).astype(BF16)
    rw_hi = rw.astype(BF16)
    rw_lo = (rw - rw_hi.astype(F32)).astype(BF16)
    logits = (jnp.dot(h_hi, rw_hi, preferred_element_type=F32) + jnp.dot(h_lo, rw_hi, preferred_element_type=F32)
              + jnp.dot(h_hi, rw_lo, preferred_element_type=F32))
    lane = lax.broadcasted_iota(I32, logits.shape, 1)
    lane_f = lane.astype(F32)
    logits = jnp.where(lane < N_EXPERTS, logits, NEG_INF)
    v1 = jnp.max(logits, axis=-1, keepdims=True)
    i1 = jnp.min(jnp.where(logits == v1, lane_f, float(ROUTE_LANES)), axis=-1, keepdims=True)
    rest = jnp.where(lane_f == i1, NEG_INF, logits)
    v2 = jnp.max(rest, axis=-1, keepdims=True)
    i2 = jnp.min(jnp.where(rest == v2, lane_f, float(ROUTE_LANES)), axis=-1, keepdims=True)
    e = jnp.exp(v2 - v1)
    g1 = 1.0 / (1.0 + e)
    g2 = e / (1.0 + e)
    route = jnp.where(lane == 0, i1, 0.0)
    route = jnp.where(lane == 1, i2, route)
    route = jnp.where(lane == 2, g1, route)
    route_ref[...] = jnp.where(lane == 3, g2, route)


def _odd_out(x2, c2, d2, wc, wd, gain, router_pad, *, tm=512):
    t, d = x2.shape
    row = lambda w: pl.BlockSpec((tm, w), lambda i: (i, 0))
    full = lambda a: pl.BlockSpec(a.shape, lambda i: (0, 0))
    return pl.pallas_call(
        _odd_out_body,
        out_shape=[jax.ShapeDtypeStruct((t, d), F32), jax.ShapeDtypeStruct((t, d), F32),
                   jax.ShapeDtypeStruct((t, ROUTE_LANES), F32)],
        grid=(t // tm,),
        in_specs=[row(d), row(MOBA_WIDTH), row(CONV_WIDTH), full(wc), full(wd),
                  pl.BlockSpec((1, d), lambda i: (0, 0)), full(router_pad)],
        out_specs=[row(d), row(d), row(ROUTE_LANES)],
        compiler_params=_params("parallel"),
        name="odd_out_proj_router",
    )(x2, c2, d2, wc, wd, gain.reshape(1, d), router_pad)


def _row_copy(src_ref, src_row, dst_ref, dst_row, sem):
    return pltpu.make_async_copy(src_ref.at[pl.ds(src_row, 1)], dst_ref.at[pl.ds(dst_row, 1)], sem)


def _dispatch_body(pos_ref, h_ref, init_ref, xs_ref, sem):
    del init_ref
    td = h_ref.shape[0]

    def copies(r):
        return [_row_copy(h_ref, r, xs_ref, pos_ref[0, 0, k * td + r], sem) for k in range(2)]

    def start(r, c):
        for cp in copies(r):
            cp.start()
        return c

    lax.fori_loop(0, td, start, 0, unroll=DMA_UNROLL)
    for _ in range(2):
        pltpu.make_async_copy(h_ref, xs_ref.at[pl.ds(0, td)], sem).wait()


def _dispatch(h2, pos3, n_rows, *, td):
    t, d = h2.shape
    return pl.pallas_call(
        _dispatch_body,
        out_shape=jax.ShapeDtypeStruct((n_rows, d), F32),
        grid=(t // td,),
        in_specs=[
            pl.BlockSpec((1, 1, 2 * td), lambda i: (i, 0, 0), memory_space=pltpu.SMEM),
            pl.BlockSpec((td, d), lambda i: (i, 0)),
            pl.BlockSpec(memory_space=pl.ANY),
        ],
        out_specs=pl.BlockSpec(memory_space=pl.ANY),
        scratch_shapes=[pltpu.SemaphoreType.DMA(())],
        input_output_aliases={2: 0},
        compiler_params=_params("arbitrary"),
        name="expert_dispatch",
    )(pos3, h2, jnp.zeros((n_rows, d), F32))


def _combine_body(pos_ref, x_ref, route_ref, y_ref, o_ref, buf, sems):
    i = pl.program_id(0)
    n = pl.num_programs(0)
    td = x_ref.shape[0]
    t = n * td

    def issue(tile, slot):
        def start(r, c):
            for k in range(2):
                _row_copy(y_ref, pos_ref[k * t + tile * td + r], buf.at[slot, k], r, sems.at[slot]).start()
            return c

        lax.fori_loop(0, td, start, 0, unroll=DMA_UNROLL)

    @pl.when(i == 0)
    def _():
        issue(0, 0)

    slot = i % 2

    @pl.when(i + 1 < n)
    def _():
        issue(i + 1, 1 - slot)

    for k in range(2):
        pltpu.make_async_copy(y_ref.at[pl.ds(0, td)], buf.at[slot, k], sems.at[slot]).wait()
    route = route_ref[...]
    o_ref[...] = x_ref[...] + route[:, 2:3] * buf[slot, 0] + route[:, 3:4] * buf[slot, 1]


def _combine(x2, route, ys, pos, *, td):
    t, d = x2.shape
    return pl.pallas_call(
        _combine_body,
        out_shape=jax.ShapeDtypeStruct((t, d), F32),
        grid_spec=pltpu.PrefetchScalarGridSpec(
            num_scalar_prefetch=1,
            grid=(t // td,),
            in_specs=[
                pl.BlockSpec((td, d), lambda i, pos: (i, 0)),
                pl.BlockSpec((td, ROUTE_LANES), lambda i, pos: (i, 0)),
                pl.BlockSpec(memory_space=pl.ANY),
            ],
            out_specs=pl.BlockSpec((td, d), lambda i, pos: (i, 0)),
            scratch_shapes=[pltpu.VMEM((2, 2, td, d), F32), pltpu.SemaphoreType.DMA((2,))],
        ),
        compiler_params=_params("arbitrary"),
        name="expert_combine",
    )(pos, x2, route, ys)


def _route_plan(route, *, tm, td):
    t = route.shape[0]
    experts = jnp.concatenate([route[:, 0], route[:, 1]]).astype(I32)
    onehot = (experts[:, None] == jnp.arange(N_EXPERTS, dtype=I32)[None, :]).astype(I32)
    running = jnp.cumsum(onehot, axis=0)
    counts = running[-1]
    padded = ((counts + tm - 1) // tm) * tm
    ends = jnp.cumsum(padded)
    starts = ends - padded
    pos = jnp.sum(onehot * (starts[None, :] + running - 1), axis=1)
    n_tiles = (2 * t) // tm + N_EXPERTS
    tile_lo = jnp.arange(n_tiles, dtype=I32) * tm
    tile_expert = jnp.sum((tile_lo[:, None] >= ends[None, :]).astype(I32), axis=1)
    n_live = (ends[-1] // tm).astype(I32)
    last = tile_expert[jnp.maximum(n_live - 1, 0)]
    live = jnp.arange(n_tiles) < n_live
    tile_expert = jnp.where(live, tile_expert, last)
    prev = jnp.concatenate([jnp.full((1,), -1, I32), tile_expert[:-1]])
    tile_first = (live & (tile_expert != prev)).astype(I32)
    pos3 = pos.reshape(2, t // td, td).transpose(1, 0, 2).reshape(t // td, 1, 2 * td)
    return pos, pos3, tile_expert, tile_first, n_live.reshape(1), n_tiles * tm


def _rope_tables(seq):
    inv = 1.0 / (ROPE_THETA ** (jnp.arange(0, HEAD_DIM, 2, dtype=F32) / HEAD_DIM))
    ang = jnp.arange(seq, dtype=F32)[:, None] * inv[None, :]
    ang = jnp.concatenate([ang, ang], axis=-1)
    tile = lambda a: jnp.tile(a, (1, HEADS_PER_TILE))
    return tile(jnp.cos(ang)), tile(jnp.sin(ang))


def _tile_gain(g):
    return jnp.tile(g, HEADS_PER_TILE).reshape(1, LANE_TILE)


def _even_layer(x2, b, s, cos_t, sin_t, norm_mix, norm_ffn, w_in, v_gain, ws, bs, gq, gk, w_out, wg, wu, wd):
    p = _norm_proj(x2, norm_mix, w_in.astype(BF16))
    p3 = p.reshape(b, s, p.shape[1])
    a = _gmlp(p3, v_gain, ws, bs)
    q0 = 2 * GMLP_WIDTH // LANE_TILE
    n_groups = len(DIL_CONFIGS)
    accs = []
    for group, (window, dil) in enumerate(DIL_CONFIGS):
        assert window // dil == BAND
        qg, kg, vtg = _qkv_prep(p3, cos_t, sin_t, _tile_gain(gq), _tile_gain(gk), q_col=q0 + group,
                                k_col=q0 + n_groups + group, v_col=q0 + 2 * n_groups + group, n_ht=1, dil=dil,
                                tt=max(4, dil) * BAND, vblock=BAND, head_rows=BAND_ACC_ROWS, with_kmean=False)
        accs.append(_band_attn(qg, kg, vtg))
    w_out = w_out.astype(BF16)
    x3 = _even_out(x2.reshape(b, s, -1), a, accs, w_out[:GMLP_WIDTH], w_out[GMLP_WIDTH:])
    return _ffn(x3.reshape(b * s, -1), norm_ffn, wg.astype(BF16), wu.astype(BF16), wd.astype(BF16))


def _odd_layer(x2, b, s, cos_t, sin_t, norm_mix, norm_ffn, w_in, gq, gk, conv_w, w_out, router_w, wg, wu, wd,
               *, layer, tm=512, td=256):
    p = _norm_proj(x2, norm_mix, w_in.astype(BF16))
    p3 = p.reshape(b, s, p.shape[1])
    tiles = MOBA_WIDTH // LANE_TILE
    qp, kp, vtp, kmean = _qkv_prep(p3, cos_t, sin_t, _tile_gain(gq), _tile_gain(gk), q_col=0, k_col=tiles,
                                   v_col=2 * tiles, n_ht=tiles, dil=1, tt=8 * MOBA_BLOCK, vblock=MOBA_BLOCK,
                                   head_rows=MOBA_ACC_ROWS, with_kmean=True)
    c = _moba(qp, kp, vtp, kmean)
    dconv = _short_conv(p3, conv_w, col0=3 * MOBA_WIDTH // CONV_WIDTH)
    w_out = w_out.astype(BF16)
    router_pad = jnp.pad(router_w, ((0, 0), (0, ROUTE_LANES - N_EXPERTS)))
    x2, h2, route = _odd_out(x2, c.reshape(b * s, MOBA_WIDTH), dconv.reshape(b * s, CONV_WIDTH),
                             w_out[:MOBA_WIDTH], w_out[MOBA_WIDTH:], norm_ffn, router_pad)
    pos, pos3, tile_expert, tile_first, n_live, n_rows = _route_plan(route, tm=tm, td=td)
    xs = _dispatch(h2, pos3, n_rows, td=td)
    ys = _expert_ffn(xs, tile_expert, tile_first, n_live, wg, wu, wd, layer=layer, tm=tm)
    return _combine(x2, route, ys, pos, td=td)


def kernel(x, norm_mix, norm_ffn, w_in_ab, gmlp_v_gain, gmlp_ws, gmlp_bs, dil_q_gain, dil_k_gain, w_out_ab,
           ffn_w_gate, ffn_w_up, ffn_w_down, w_in_cd, moba_q_gain, moba_k_gain, conv_w, w_out_cd, router_w,
           moe_w_gate, moe_w_up, moe_w_down):
    b, s, d = x.shape
    depth = norm_mix.shape[0]
    cos_t, sin_t = _rope_tables(s)
    x2 = x.reshape(b * s, d)
    for layer in range(depth):
        i = layer // 2
        if layer % 2 == 0:
            x2 = _even_layer(x2, b, s, cos_t, sin_t, norm_mix[layer], norm_ffn[layer], w_in_ab[i], gmlp_v_gain[i],
                             gmlp_ws[i], gmlp_bs[i], dil_q_gain[i], dil_k_gain[i], w_out_ab[i], ffn_w_gate[i],
                             ffn_w_up[i], ffn_w_down[i])
        else:
            x2 = _odd_layer(x2, b, s, cos_t, sin_t, norm_mix[layer], norm_ffn[layer], w_in_cd[i], moba_q_gain[i],
                            moba_k_gain[i], conv_w[i], w_out_cd[i], router_w[i], moe_w_gate, moe_w_up,
                            moe_w_down, layer=i)
    return x2.reshape(b, s, d)
```

```python
import functools

import jax
import jax.numpy as jnp
from jax import lax
from jax.experimental import pallas as pl
from jax.experimental.pallas import tpu as pltpu

F32 = jnp.float32
BF16 = jnp.bfloat16
I32 = jnp.int32

D_MODEL = 1024
HEAD_DIM = 64
ROPE_THETA = 10000.0
NORM_EPS = 1e-6
LANE_TILE = 256
HEADS_PER_TILE = LANE_TILE // HEAD_DIM
GMLP_WIDTH = 256
GMLP_CHUNK = 128
DIL_CONFIGS = ((128, 1), (512, 4), (2048, 16))
DIL_WIDTH = 768
BAND = 128
BAND_ACC_ROWS = 128
BAND_L_ROW = HEAD_DIM
BAND_M_ROW = HEAD_DIM + 1
BAND_LOOKAHEAD = 5
STRIDE_LANES = 128
MOBA_WIDTH = 768
MOBA_BLOCK = 256
MOBA_TOPK = 3
MOBA_ACC_ROWS = HEAD_DIM + 16
LOG2_E = 1.4426950408889634
MOBA_M_INIT = -1e30
CONV_WIDTH = 256
N_EXPERTS = 8
ROUTE_LANES = 128
NEG_INF = float("-inf")
DMA_UNROLL = 8

_NT = (((1,), (1,)), ((), ()))


def _params(*sem):
    return pltpu.CompilerParams(dimension_semantics=tuple(sem), vmem_limit_bytes=56 * 1024 * 1024)


def _lane_ids():
    return lax.broadcasted_iota(I32, (1, LANE_TILE), 1)


def _head_block_diag():
    r = lax.broadcasted_iota(I32, (LANE_TILE, LANE_TILE), 0) // HEAD_DIM
    c = lax.broadcasted_iota(I32, (LANE_TILE, LANE_TILE), 1) // HEAD_DIM
    return jnp.where(r == c, 1.0, 0.0).astype(BF16)


def _head_mean_sq(x, bd):
    s = x * x
    hi = s.astype(BF16)
    lo = (s - hi.astype(F32)).astype(BF16)
    tot = jnp.dot(hi, bd, preferred_element_type=F32) + jnp.dot(lo, bd, preferred_element_type=F32)
    return tot * (1.0 / HEAD_DIM)


def _head_norm(x, gain, bd):
    return x * lax.rsqrt(_head_mean_sq(x, bd) + NORM_EPS) * gain


def _rope(x, cos, sin, lane):
    half = HEAD_DIM // 2
    x_up = pltpu.roll(x, LANE_TILE - half, 1)
    x_dn = pltpu.roll(x, half, 1)
    rot = jnp.where((lane % HEAD_DIM) < half, -x_up, x_dn)
    return x * cos + rot * sin


def _row_norm(x, gain):
    return x * lax.rsqrt(jnp.mean(x * x, axis=-1, keepdims=True) + NORM_EPS) * gain


def _norm_proj_body(x_ref, g_ref, w_ref, o_ref, *, tn):
    hb = _row_norm(x_ref[...], g_ref[...]).astype(BF16)
    for c in range(o_ref.shape[-1] // tn):
        sl = slice(c * tn, (c + 1) * tn)
        o_ref[:, sl] = jnp.dot(hb, w_ref[:, sl], preferred_element_type=F32).astype(o_ref.dtype)


def _norm_proj(x2, gain, w_bf16, *, tm=512, tn=256):
    t, d = x2.shape
    n = w_bf16.shape[1]
    return pl.pallas_call(
        functools.partial(_norm_proj_body, tn=tn),
        out_shape=jax.ShapeDtypeStruct((t, n), BF16),
        grid=(t // tm,),
        in_specs=[
            pl.BlockSpec((tm, d), lambda i: (i, 0)),
            pl.BlockSpec((1, d), lambda i: (0, 0)),
            pl.BlockSpec((d, n), lambda i: (0, 0)),
        ],
        out_specs=pl.BlockSpec((tm, n), lambda i: (i, 0)),
        compiler_params=_params("parallel"),
        name="norm_proj",
    )(x2, gain.reshape(1, d), w_bf16)


def _gmlp_body(u_ref, v_ref, gain_ref, ws_ref, bias_ref, o_ref):
    lane = _lane_ids()
    bd = _head_block_diag()
    r = lax.broadcasted_iota(I32, (GMLP_CHUNK, GMLP_CHUNK), 0)
    c = lax.broadcasted_iota(I32, (GMLP_CHUNK, GMLP_CHUNK), 1)
    w_tril = [jnp.where(r >= c, ws_ref[g], 0.0).astype(BF16) for g in range(HEADS_PER_TILE)]
    for ch in range(o_ref.shape[0] // GMLP_CHUNK):
        sl = slice(ch * GMLP_CHUNK, (ch + 1) * GMLP_CHUNK)
        u = jax.nn.gelu(u_ref[sl, :].astype(F32))
        v = jax.nn.gelu(v_ref[sl, :].astype(F32))
        vn = _head_norm(v, gain_ref[...], bd).astype(BF16)
        y = bias_ref[...]
        for g in range(HEADS_PER_TILE):
            yg = jnp.dot(w_tril[g], vn, preferred_element_type=F32)
            y = y + jnp.where(lane // HEAD_DIM == g, yg, 0.0)
        o_ref[sl, :] = (u * y).astype(o_ref.dtype)


def _gmlp(p3, v_gain, ws, bs, *, tc=512):
    b, s, _ = p3.shape
    groups, chunk = bs.shape
    bias = jnp.repeat(bs.T, HEAD_DIM, axis=1)
    return pl.pallas_call(
        _gmlp_body,
        out_shape=jax.ShapeDtypeStruct((b, s, GMLP_WIDTH), BF16),
        grid=(b, s // tc),
        in_specs=[
            pl.BlockSpec((None, tc, LANE_TILE), lambda bi, i: (bi, i, 0)),
            pl.BlockSpec((None, tc, LANE_TILE), lambda bi, i: (bi, i, 1)),
            pl.BlockSpec((1, GMLP_WIDTH), lambda bi, i: (0, 0)),
            pl.BlockSpec((groups, chunk, chunk), lambda bi, i: (0, 0, 0)),
            pl.BlockSpec((chunk, GMLP_WIDTH), lambda bi, i: (0, 0)),
        ],
        out_specs=pl.BlockSpec((None, tc, GMLP_WIDTH), lambda bi, i: (bi, i, 0)),
        compiler_params=_params("parallel", "parallel"),
        name="gmlp_gate",
    )(p3, p3, v_gain.reshape(1, GMLP_WIDTH), ws, bias)


def _residues(val, scr, dil):
    if dil == 1:
        return [val]
    rows = val.shape[0] // dil
    slabs = LANE_TILE // STRIDE_LANES
    for c in range(slabs):
        scr[c] = val[:, c * STRIDE_LANES:(c + 1) * STRIDE_LANES]
    return [jnp.concatenate([scr[c, pl.ds(r, rows, stride=dil), :] for c in range(slabs)], axis=1)
            for r in range(dil)]


def _qkv_prep_body(q_ref, k_ref, v_ref, cos_ref, sin_ref, gq_ref, gk_ref, *rest, dil, with_kmean):
    n_out = 4 if with_kmean else 3
    qo_ref, ko_ref, vt_ref = rest[:3]
    scr = list(rest[n_out:]) + [None] * 3
    lane = _lane_ids()
    bd = _head_block_diag()
    cos = cos_ref[...]
    sin = sin_ref[...]
    qn = _rope(_head_norm(q_ref[...].astype(F32), gq_ref[...], bd), cos, sin, lane) * (HEAD_DIM ** -0.5 * LOG2_E)
    kn = _rope(_head_norm(k_ref[...].astype(F32), gk_ref[...], bd), cos, sin, lane)
    if with_kmean:
        km_ref = rest[3]
        km_ref[...] = jnp.mean(kn.reshape(km_ref.shape[0], MOBA_BLOCK, LANE_TILE), axis=1)
    for r, piece in enumerate(_residues(qn, scr[0], dil)):
        qo_ref[r] = piece.astype(BF16)
    for r, piece in enumerate(_residues(kn, scr[1], dil)):
        ko_ref[r] = piece.astype(BF16)
    n_blocks, acc_w, vblock = vt_ref.shape[1:]
    head_rows = acc_w // HEADS_PER_TILE
    ones = jnp.ones((head_rows - HEAD_DIM, vblock), BF16)
    for r, piece in enumerate(_residues(v_ref[...].astype(F32), scr[2], dil)):
        for c in range(n_blocks):
            vt = piece[c * vblock:(c + 1) * vblock, :].T.astype(BF16)
            for h in range(HEADS_PER_TILE):
                vt_ref[r, c, h * head_rows:h * head_rows + HEAD_DIM, :] = vt[h * HEAD_DIM:(h + 1) * HEAD_DIM, :]
                vt_ref[r, c, h * head_rows + HEAD_DIM:(h + 1) * head_rows, :] = ones


def _qkv_prep(p3, cos_t, sin_t, gq, gk, *, q_col, k_col, v_col, n_ht, dil, tt, vblock, head_rows, with_kmean):
    b, s, _ = p3.shape
    length = s // dil
    tr = tt // dil
    acc_w = HEADS_PER_TILE * head_rows

    def col(c0):
        return pl.BlockSpec((None, tt, LANE_TILE), lambda bi, ht, i: (bi, i, c0 + ht))

    tab = pl.BlockSpec((tt, LANE_TILE), lambda bi, ht, i: (i, 0))
    gain = pl.BlockSpec((1, LANE_TILE), lambda bi, ht, i: (0, 0))
    qk_shape = jax.ShapeDtypeStruct((b, n_ht, dil, length, LANE_TILE), BF16)
    qk_spec = pl.BlockSpec((None, None, dil, tr, LANE_TILE), lambda bi, ht, i: (bi, ht, 0, i, 0))
    out_shape = [qk_shape, qk_shape,
                 jax.ShapeDtypeStruct((b, n_ht, dil, length // vblock, acc_w, vblock), BF16)]
    out_specs = [qk_spec, qk_spec,
                 pl.BlockSpec((None, None, dil, tr // vblock, acc_w, vblock), lambda bi, ht, i: (bi, ht, 0, i, 0, 0))]
    if with_kmean:
        out_shape.append(jax.ShapeDtypeStruct((b, n_ht, s // MOBA_BLOCK, LANE_TILE), F32))
        out_specs.append(pl.BlockSpec((None, None, tt // MOBA_BLOCK, LANE_TILE), lambda bi, ht, i: (bi, ht, i, 0)))
    scratch = [pltpu.VMEM((LANE_TILE // STRIDE_LANES, tt, STRIDE_LANES), F32)] * (3 if dil > 1 else 0)
    return pl.pallas_call(
        functools.partial(_qkv_prep_body, dil=dil, with_kmean=with_kmean),
        out_shape=out_shape,
        grid=(b, n_ht, s // tt),
        in_specs=[col(q_col), col(k_col), col(v_col), tab, tab, gain, gain],
        out_specs=out_specs,
        scratch_shapes=scratch,
        compiler_params=_params("parallel", "parallel", "parallel"),
        name=f"qkv_prep_d{dil}",
    )(p3, p3, p3, cos_t, sin_t, gq, gk)


def _band_attn_body(q_ref, k_ref, kp_ref, vt_ref, vtp_ref, o_ref):
    i = pl.program_id(2)
    lb = q_ref.shape[0]
    lane = _lane_ids()
    kk = lax.broadcasted_iota(I32, (2 * BAND, BAND), 0)
    qq = lax.broadcasted_iota(I32, (2 * BAND, BAND), 1) + BAND
    in_band = (qq - kk >= 0) & (qq - kk <= BAND)
    first_ok = in_band & ((kk >= BAND) | (i > 0))
    acc_row = lax.broadcasted_iota(I32, (BAND_ACC_ROWS, BAND), 0)
    k_all = jnp.concatenate([kp_ref[...], k_ref[...]], axis=0)
    chains = [(j, h) for j in range(lb // BAND) for h in range(HEADS_PER_TILE)]

    def scores(j, h):
        qj = q_ref[j * BAND:(j + 1) * BAND, :]
        qh = jnp.where(lane // HEAD_DIM == h, qj, jnp.zeros_like(qj))
        return lax.dot_general(k_all[j * BAND:(j + 2) * BAND], qh, _NT, preferred_element_type=F32)

    staged = [scores(*c) for c in chains[:BAND_LOOKAHEAD]]
    tiles = []
    for n, (j, h) in enumerate(chains):
        if n + BAND_LOOKAHEAD < len(chains):
            staged.append(scores(*chains[n + BAND_LOOKAHEAD]))
        rows = slice(h * BAND_ACC_ROWS, (h + 1) * BAND_ACC_ROWS)
        s = jnp.where(first_ok if j == 0 else in_band, staged[n], NEG_INF)
        m = jnp.max(s, axis=0, keepdims=True)
        p = jnp.exp2(s - m).astype(BF16)
        v_prev = vtp_ref[0, rows, :] if j == 0 else vt_ref[j - 1, rows, :]
        vw = jnp.concatenate([v_prev, vt_ref[j, rows, :]], axis=1)
        acc = jnp.dot(vw, p, preferred_element_type=F32)
        tiles.append(jnp.where(acc_row == BAND_M_ROW, m, acc))
        if h == HEADS_PER_TILE - 1:
            o_ref[j * BAND:(j + 1) * BAND, :] = jnp.concatenate(tiles, axis=0).T
            tiles = []


def _band_attn(qg, kg, vtg):
    b, _, dil, length, _ = qg.shape
    lb = min(length, 512)
    sub = lb // BAND
    acc_w = HEADS_PER_TILE * BAND_ACC_ROWS
    before = lambda i: jnp.maximum(i * sub - 1, 0)
    cur = pl.BlockSpec((None, None, None, lb, LANE_TILE), lambda bi, r, i: (bi, 0, r, i, 0))
    prev = pl.BlockSpec((None, None, None, BAND, LANE_TILE), lambda bi, r, i: (bi, 0, r, before(i), 0))
    vt_cur = pl.BlockSpec((None, None, None, sub, acc_w, BAND), lambda bi, r, i: (bi, 0, r, i, 0, 0))
    vt_prev = pl.BlockSpec((None, None, None, 1, acc_w, BAND), lambda bi, r, i: (bi, 0, r, before(i), 0, 0))
    return pl.pallas_call(
        _band_attn_body,
        out_shape=jax.ShapeDtypeStruct((b, dil, length, acc_w), F32),
        grid=(b, dil, length // lb),
        in_specs=[cur, cur, prev, vt_cur, vt_prev],
        out_specs=pl.BlockSpec((None, None, lb, acc_w), lambda bi, r, i: (bi, r, i, 0)),
        compiler_params=_params("parallel", "parallel", "arbitrary"),
        name=f"band_attn_d{dil}",
    )(qg, kg, kg, vtg, vtg)


def _even_out_body(x_ref, a_ref, o0_ref, o1_ref, o2_ref, wa_ref, wb_ref, out_ref, *scratch):
    tm = x_ref.shape[0]

    def natural_order(o_ref, scr):
        dil = o_ref.shape[0]
        if dil == 1:
            return [o_ref[0, :, h * BAND_ACC_ROWS:(h + 1) * BAND_ACC_ROWS] for h in range(HEADS_PER_TILE)]
        for r in range(dil):
            for h in range(HEADS_PER_TILE):
                scr[h, pl.ds(r, tm // dil, stride=dil), :] = o_ref[r, :, h * BAND_ACC_ROWS:(h + 1) * BAND_ACC_ROWS]
        return [scr[h] for h in range(HEADS_PER_TILE)]

    o_refs = (o0_ref, o1_ref, o2_ref)
    spare = list(scratch)
    groups = [natural_order(o, spare.pop(0) if o.shape[0] > 1 else None) for o in o_refs]
    acc = x_ref[...] + jnp.dot(a_ref[...], wa_ref[...], preferred_element_type=F32)
    for h in range(HEADS_PER_TILE):
        tiles = [g[h] for g in groups]
        tops = [t[:, BAND_M_ROW:BAND_M_ROW + 1] for t in tiles]
        top = jnp.maximum(jnp.maximum(tops[0], tops[1]), tops[2])
        ws = [jnp.exp2(m - top) for m in tops]
        num = sum(w * t[:, :HEAD_DIM] for w, t in zip(ws, tiles))
        den = sum(w * t[:, BAND_L_ROW:BAND_L_ROW + 1] for w, t in zip(ws, tiles))
        merged = (num / den).astype(BF16)
        acc = acc + jnp.dot(merged, wb_ref[h * HEAD_DIM:(h + 1) * HEAD_DIM, :], preferred_element_type=F32)
    out_ref[...] = acc


def _even_out(x3, a3, accs, wa, wb, *, tm=512):
    b, s, d = x3.shape
    acc_w = HEADS_PER_TILE * BAND_ACC_ROWS
    full = lambda a: pl.BlockSpec(a.shape, lambda bi, i: (0, 0))

    def acc_spec(a):
        dil = a.shape[1]
        return pl.BlockSpec((None, dil, tm // dil, acc_w), lambda bi, i: (bi, 0, i, 0))

    scratch = [pltpu.VMEM((HEADS_PER_TILE, tm, BAND_ACC_ROWS), F32) for a in accs if a.shape[1] > 1]
    return pl.pallas_call(
        _even_out_body,
        out_shape=jax.ShapeDtypeStruct((b, s, d), F32),
        grid=(b, s // tm),
        in_specs=[pl.BlockSpec((None, tm, d), lambda bi, i: (bi, i, 0)),
                  pl.BlockSpec((None, tm, GMLP_WIDTH), lambda bi, i: (bi, i, 0))]
        + [acc_spec(a) for a in accs] + [full(wa), full(wb)],
        out_specs=pl.BlockSpec((None, tm, d), lambda bi, i: (bi, i, 0)),
        scratch_shapes=scratch,
        compiler_params=_params("parallel", "parallel"),
        name="even_out_proj",
    )(x3, a3, *accs, wa, wb)


def _swiglu_step(h_bf16, wg, wu, wd):
    g = jnp.dot(h_bf16, wg, preferred_element_type=F32)
    u = jnp.dot(h_bf16, wu, preferred_element_type=F32)
    a = (g * jax.nn.sigmoid(g) * u).astype(BF16)
    return jnp.dot(a, wd, preferred_element_type=F32)


def _ffn_body(x_ref, g_ref, wg_ref, wu_ref, wd_ref, o_ref, *, tf):
    x = x_ref[...]
    h = _row_norm(x, g_ref[...]).astype(BF16)
    acc = x
    for c in range(wg_ref.shape[1] // tf):
        sl = slice(c * tf, (c + 1) * tf)
        acc = acc + _swiglu_step(h, wg_ref[:, sl], wu_ref[:, sl], wd_ref[sl, :])
    o_ref[...] = acc


def _ffn(x2, gain, wg, wu, wd, *, tm=512, tf=256):
    t, d = x2.shape
    resident = lambda a: pl.BlockSpec(a.shape, lambda i: (0, 0), pipeline_mode=pl.Buffered(1))
    return pl.pallas_call(
        functools.partial(_ffn_body, tf=tf),
        out_shape=jax.ShapeDtypeStruct((t, d), F32),
        grid=(t // tm,),
        in_specs=[
            pl.BlockSpec((tm, d), lambda i: (i, 0)),
            pl.BlockSpec((1, d), lambda i: (0, 0)),
            resident(wg), resident(wu), resident(wd),
        ],
        out_specs=pl.BlockSpec((tm, d), lambda i: (i, 0)),
        compiler_params=_params("parallel"),
        name="ffn_swiglu",
    )(x2, gain.reshape(1, d), wg, wu, wd)


def _expert_body(te_ref, first_ref, nv_ref, x_ref, wg_hbm, wu_hbm, wd_hbm, o_ref,
                 cg_s, cu_s, cd_s, sg_s, su_s, sd_s, sems, *, layer, tf):
    i = pl.program_id(0)
    nf = cg_s.shape[0]
    e = te_ref[i]

    def chunk_copies(f, slot):
        cols = pl.ds(f * tf, tf)
        return (pltpu.make_async_copy(wg_hbm.at[layer, e, :, cols], sg_s.at[slot], sems.at[0, slot]),
                pltpu.make_async_copy(wu_hbm.at[layer, e, :, cols], su_s.at[slot], sems.at[1, slot]),
                pltpu.make_async_copy(wd_hbm.at[layer, e, cols, :], sd_s.at[slot], sems.at[2, slot]))

    def compute(load_chunk):
        h = x_ref[...].astype(BF16)
        acc = jnp.zeros(o_ref.shape, F32)
        for f in range(nf):
            load_chunk(f)
            acc = acc + _swiglu_step(h, cg_s[f], cu_s[f], cd_s[f])
        o_ref[...] = acc

    @pl.when(first_ref[i] == 1)
    def _():
        def load_chunk(f):
            if f + 1 < nf:
                for cp in chunk_copies(f + 1, (f + 1) % 2):
                    cp.start()
            for cp in chunk_copies(f, f % 2):
                cp.wait()
            cg_s[f] = sg_s[f % 2].astype(BF16)
            cu_s[f] = su_s[f % 2].astype(BF16)
            cd_s[f] = sd_s[f % 2].astype(BF16)

        for cp in chunk_copies(0, 0):
            cp.start()
        compute(load_chunk)

    @pl.when((first_ref[i] == 0) & (i < nv_ref[0]))
    def _():
        compute(lambda f: None)

    @pl.when(i >= nv_ref[0])
    def _():
        o_ref[...] = jnp.zeros_like(o_ref)


def _expert_ffn(xs, tile_expert, tile_first, n_live, wg, wu, wd, *, layer, tm, tf=512):
    p, d = xs.shape
    ff = wg.shape[-1]
    nf = ff // tf
    row = lambda i, te, first, nv: (i, 0)
    return pl.pallas_call(
        functools.partial(_expert_body, layer=layer, tf=tf),
        out_shape=jax.ShapeDtypeStruct((p, d), F32),
        grid_spec=pltpu.PrefetchScalarGridSpec(
            num_scalar_prefetch=3,
            grid=(p // tm,),
            in_specs=[
                pl.BlockSpec((tm, d), row),
                pl.BlockSpec(memory_space=pl.ANY),
                pl.BlockSpec(memory_space=pl.ANY),
                pl.BlockSpec(memory_space=pl.ANY),
            ],
            out_specs=pl.BlockSpec((tm, d), row),
            scratch_shapes=[
                pltpu.VMEM((nf, d, tf), BF16),
                pltpu.VMEM((nf, d, tf), BF16),
                pltpu.VMEM((nf, tf, d), BF16),
                pltpu.VMEM((2, d, tf), F32),
                pltpu.VMEM((2, d, tf), F32),
                pltpu.VMEM((2, tf, d), F32),
                pltpu.SemaphoreType.DMA((3, 2)),
            ],
        ),
        compiler_params=_params("arbitrary"),
        name="expert_swiglu",
    )(tile_expert, tile_first, n_live, xs, wg, wu, wd)


def _moba_body(q_ref, k_ref, vt_ref, km_ref, o_ref, sel_s, qh_s, sa_s, sb_s, ta_s, tb_s, m_s, out_s):
    i = pl.program_id(2)
    nblk = km_ref.shape[0]
    lane = _lane_ids()
    qn = q_ref[...]
    blk = lax.broadcasted_iota(I32, (nblk, MOBA_BLOCK), 0).astype(F32)
    first_blk = i.astype(F32)
    km = km_ref[...]
    krow = lax.broadcasted_iota(I32, (MOBA_BLOCK, MOBA_BLOCK), 0)
    qcol = lax.broadcasted_iota(I32, (MOBA_BLOCK, MOBA_BLOCK), 1)
    causal = krow <= qcol

    km_hi = km.astype(BF16)
    km_lo = (km - km_hi.astype(F32)).astype(BF16)
    acc_rows = [slice(h * MOBA_ACC_ROWS, (h + 1) * MOBA_ACC_ROWS) for h in range(HEADS_PER_TILE)]

    def keys(j):
        return k_ref[pl.ds(pl.multiple_of(j * MOBA_BLOCK, MOBA_BLOCK), MOBA_BLOCK), :]

    heads = range(HEADS_PER_TILE)
    qh = [jnp.where(lane // HEAD_DIM == h, qn, jnp.zeros_like(qn)) for h in heads]
    for h in heads:
        qh_s[h] = qh[h]
    gates = [lax.dot_general(km_hi, qh[h], _NT, preferred_element_type=F32)
             + lax.dot_general(km_lo, qh[h], _NT, preferred_element_type=F32) for h in heads]
    k_own = keys(i)
    own = [lax.dot_general(k_own, qh[h], _NT, preferred_element_type=F32) for h in heads]
    def stage_scores(kj, q_of, buf, top):
        for h in heads:
            s = lax.dot_general(kj, q_of(h), _NT, preferred_element_type=F32)
            buf[h] = s
            top[h] = jnp.max(s, axis=0, keepdims=True)

    stage_scores(keys(0), lambda h: qh[h], sa_s, ta_s)

    for h in heads:
        gate = jnp.where(blk < first_blk, gates[h], NEG_INF)
        sel = jnp.zeros((nblk, MOBA_BLOCK), F32)
        for _ in range(MOBA_TOPK):
            top = jnp.max(gate, axis=0, keepdims=True)
            idx = jnp.min(jnp.where(gate == top, blk, float(nblk)), axis=0, keepdims=True)
            hit = blk == idx
            sel = jnp.where(hit & (top > NEG_INF), 1.0, sel)
            gate = jnp.where(hit, NEG_INF, gate)
        sel_s[h] = sel

    for h in heads:
        s = jnp.where(causal, own[h], NEG_INF)
        m0 = jnp.max(s, axis=0, keepdims=True)
        m_s[h] = m0
        out_s[acc_rows[h], :] = jnp.dot(vt_ref[i, acc_rows[h], :], jnp.exp2(s - m0).astype(BF16),
                                        preferred_element_type=F32)

    def stage(j_next, buf_next, top_next, j, buf, top):
        jc = jnp.minimum(j, nblk - 1)
        bar = jnp.where(j < i, 0.0, 2.0)
        stage_scores(keys(jnp.minimum(j_next, nblk - 1)), lambda h: qh_s[h], buf_next, top_next)
        for h in heads:
            bias = jnp.where(sel_s[h, pl.ds(jc, 1), :] > bar, 0.0, NEG_INF)
            m = m_s[h]
            m_new = jnp.maximum(m, top[h] + bias)
            p = jnp.exp2(buf[h] + (bias - m_new)).astype(BF16)
            m_s[h] = m_new
            out_s[acc_rows[h], :] = (jnp.exp2(m - m_new) * out_s[acc_rows[h], :]
                                     + jnp.dot(vt_ref[jc, acc_rows[h], :], p, preferred_element_type=F32))

    def step(t, carry):
        stage(2 * t + 1, sb_s, tb_s, 2 * t, sa_s, ta_s)
        stage(2 * t + 2, sa_s, ta_s, 2 * t + 1, sb_s, tb_s)
        return carry

    lax.fori_loop(0, (i + 1) // 2, step, 0)

    outs = []
    for h in heads:
        acc = out_s[acc_rows[h], :]
        outs.append(acc[:HEAD_DIM, :] / acc[HEAD_DIM:HEAD_DIM + 1, :])
    o_ref[...] = jnp.concatenate(outs, axis=0).T.astype(o_ref.dtype)


def _moba(qp, kp, vtp, kmean):
    b, tiles, _, s, _ = qp.shape
    nblk = s // MOBA_BLOCK
    acc_w = HEADS_PER_TILE * MOBA_ACC_ROWS
    return pl.pallas_call(
        _moba_body,
        out_shape=jax.ShapeDtypeStruct((b, s, MOBA_WIDTH), BF16),
        grid=(b, tiles, nblk),
        in_specs=[
            pl.BlockSpec((None, None, None, MOBA_BLOCK, LANE_TILE), lambda bi, hg, i: (bi, hg, 0, i, 0)),
            pl.BlockSpec((None, None, None, s, LANE_TILE), lambda bi, hg, i: (bi, hg, 0, 0, 0)),
            pl.BlockSpec((None, None, None, nblk, acc_w, MOBA_BLOCK), lambda bi, hg, i: (bi, hg, 0, 0, 0, 0)),
            pl.BlockSpec((None, None, nblk, LANE_TILE), lambda bi, hg, i: (bi, hg, 0, 0)),
        ],
        out_specs=pl.BlockSpec((None, MOBA_BLOCK, LANE_TILE), lambda bi, hg, i: (bi, i, hg)),
        scratch_shapes=[
            pltpu.VMEM((HEADS_PER_TILE, nblk, MOBA_BLOCK), F32),
            pltpu.VMEM((HEADS_PER_TILE, MOBA_BLOCK, LANE_TILE), BF16),
            pltpu.VMEM((HEADS_PER_TILE, MOBA_BLOCK, MOBA_BLOCK), F32),
            pltpu.VMEM((HEADS_PER_TILE, MOBA_BLOCK, MOBA_BLOCK), F32),
            pltpu.VMEM((HEADS_PER_TILE, 1, MOBA_BLOCK), F32),
            pltpu.VMEM((HEADS_PER_TILE, 1, MOBA_BLOCK), F32),
            pltpu.VMEM((HEADS_PER_TILE, 1, MOBA_BLOCK), F32),
            pltpu.VMEM((HEADS_PER_TILE * MOBA_ACC_ROWS, MOBA_BLOCK), F32),
        ],
        compiler_params=_params("parallel", "parallel", "arbitrary"),
        name="moba_attn",
    )(qp, kp, vtp, kmean)


def _conv_body(bg_ref, cg_ref, xz_ref, w_ref, o_ref, *, ts):
    halo = 16
    w = w_ref[...]
    for c in range(o_ref.shape[0] // ts):
        lo = c * ts
        if c == 0:
            z = cg_ref[0:ts, :].astype(F32) * xz_ref[0:ts, :].astype(F32)
            zp = jnp.concatenate([jnp.zeros((halo, CONV_WIDTH), F32), z], axis=0)
        else:
            zp = cg_ref[lo - halo:lo + ts, :].astype(F32) * xz_ref[lo - halo:lo + ts, :].astype(F32)
        z1 = pltpu.roll(zp, 1, 0)[halo:]
        z2 = pltpu.roll(zp, 2, 0)[halo:]
        y = w[2:3, :] * zp[halo:] + w[1:2, :] * z1 + w[0:1, :] * z2
        o_ref[lo:lo + ts, :] = (bg_ref[lo:lo + ts, :].astype(F32) * y).astype(o_ref.dtype)


def _short_conv(p3, conv_w, *, col0, ts=512):
    b, s, _ = p3.shape
    taps = conv_w.shape[0]

    def col(c):
        return pl.BlockSpec((None, s, CONV_WIDTH), lambda bi: (bi, 0, col0 + c))

    return pl.pallas_call(
        functools.partial(_conv_body, ts=ts),
        out_shape=jax.ShapeDtypeStruct((b, s, CONV_WIDTH), BF16),
        grid=(b,),
        in_specs=[col(0), col(1), col(2), pl.BlockSpec((taps, CONV_WIDTH), lambda bi: (0, 0))],
        out_specs=pl.BlockSpec((None, s, CONV_WIDTH), lambda bi: (bi, 0, 0)),
        compiler_params=_params("parallel"),
        name="short_conv",
    )(p3, p3, p3, conv_w)


def _odd_out_body(x_ref, c_ref, d_ref, wc_ref, wd_ref, gain_ref, rw_ref, xo_ref, h_ref, route_ref):
    x = x_ref[...] + jnp.dot(c_ref[...], wc_ref[...], preferred_element_type=F32)
    x = x + jnp.dot(d_ref[...], wd_ref[...], preferred_element_type=F32)
    xo_ref[...] = x
    h = _row_norm(x, gain_ref[...])
    h_ref[...] = h
    rw = rw_ref[...]
    h_hi = h.astype(BF16)
    h_lo = (h - h_hi.astype(F32)).astype(BF16)
    rw_hi = rw.astype(BF16)
    rw_lo = (rw - rw_hi.astype(F32)).astype(BF16)
    logits = (jnp.dot(h_hi, rw_hi, preferred_element_type=F32) + jnp.dot(h_lo, rw_hi, preferred_element_type=F32)
              + jnp.dot(h_hi, rw_lo, preferred_element_type=F32))
    lane = lax.broadcasted_iota(I32, logits.shape, 1)
    lane_f = lane.astype(F32)
    logits = jnp.where(lane < N_EXPERTS, logits, NEG_INF)
    v1 = jnp.max(logits, axis=-1, keepdims=True)
    i1 = jnp.min(jnp.where(logits == v1, lane_f, float(ROUTE_LANES)), axis=-1, keepdims=True)
    rest = jnp.where(lane_f == i1, NEG_INF, logits)
    v2 = jnp.max(rest, axis=-1, keepdims=True)
    i2 = jnp.min(jnp.where(rest == v2, lane_f, float(ROUTE_LANES)), axis=-1, keepdims=True)
    e = jnp.exp(v2 - v1)
    g1 = 1.0 / (1.0 + e)
    g2 = e / (1.0 + e)
    route = jnp.where(lane == 0, i1, 0.0)
    route = jnp.where(lane == 1, i2, route)
    route = jnp.where(lane == 2, g1, route)
    route_ref[...] = jnp.where(lane == 3, g2, route)


def _odd_out(x2, c2, d2, wc, wd, gain, router_pad, *, tm=512):
    t, d = x2.shape
    row = lambda w: pl.BlockSpec((tm, w), lambda i: (i, 0))
    full = lambda a: pl.BlockSpec(a.shape, lambda i: (0, 0))
    return pl.pallas_call(
        _odd_out_body,
        out_shape=[jax.ShapeDtypeStruct((t, d), F32), jax.ShapeDtypeStruct((t, d), F32),
                   jax.ShapeDtypeStruct((t, ROUTE_LANES), F32)],
        grid=(t // tm,),
        in_specs=[row(d), row(MOBA_WIDTH), row(CONV_WIDTH), full(wc), full(wd),
                  pl.BlockSpec((1, d), lambda i: (0, 0)), full(router_pad)],
        out_specs=[row(d), row(d), row(ROUTE_LANES)],
        compiler_params=_params("parallel"),
        name="odd_out_proj_router",
    )(x2, c2, d2, wc, wd, gain.reshape(1, d), router_pad)


def _row_copy(src_ref, src_row, dst_ref, dst_row, sem):
    return pltpu.make_async_copy(src_ref.at[pl.ds(src_row, 1)], dst_ref.at[pl.ds(dst_row, 1)], sem)


def _dispatch_body(pos_ref, pad_ref, h_ref, xs_ref, zeros_s, sem, pad_sem):
    td = h_ref.shape[0]

    @pl.when(pl.program_id(0) == 0)
    def _():
        zeros_s[...] = jnp.zeros_like(zeros_s)
        tm = zeros_s.shape[0]

        n_tiles = xs_ref.shape[0] // tm
        n_live = pad_ref[N_EXPERTS]

        def fills():
            for e in range(N_EXPERTS):
                start = pl.multiple_of(jnp.maximum(pad_ref[e], 0), tm)
                yield pad_ref[e] >= 0, pltpu.make_async_copy(zeros_s, xs_ref.at[pl.ds(start, tm)], pad_sem)
            for c in range(N_EXPERTS):
                tile = n_tiles - 1 - c
                yield tile >= n_live, pltpu.make_async_copy(zeros_s, xs_ref.at[pl.ds(tile * tm, tm)], pad_sem)

        for has_tile, cp in fills():
            pl.when(has_tile)(cp.start)
        for has_tile, cp in fills():
            pl.when(has_tile)(cp.wait)

    def copies(r):
        return [_row_copy(h_ref, r, xs_ref, pos_ref[0, 0, k * td + r], sem) for k in range(2)]

    def start(r, c):
        for k, cp in enumerate(copies(r)):
            cp.start(priority=k)
        return c

    lax.fori_loop(0, td, start, 0, unroll=DMA_UNROLL)
    for _ in range(2):
        pltpu.make_async_copy(h_ref, xs_ref.at[pl.ds(0, td)], sem).wait()


def _dispatch(h2, pos3, pad_tiles, n_rows, *, td, tm):
    t, d = h2.shape
    return pl.pallas_call(
        _dispatch_body,
        out_shape=jax.ShapeDtypeStruct((n_rows, d), F32),
        grid=(t // td,),
        in_specs=[
            pl.BlockSpec((1, 1, 2 * td), lambda i: (i, 0, 0), memory_space=pltpu.SMEM),
            pl.BlockSpec(memory_space=pltpu.SMEM),
            pl.BlockSpec((td, d), lambda i: (i, 0)),
        ],
        out_specs=pl.BlockSpec(memory_space=pl.ANY),
        scratch_shapes=[pltpu.VMEM((tm, d), F32), pltpu.SemaphoreType.DMA(()), pltpu.SemaphoreType.DMA(())],
        compiler_params=_params("arbitrary"),
        name="expert_dispatch",
    )(pos3, pad_tiles, h2)


def _combine_body(pos_ref, x_ref, route_ref, y_ref, o_ref, buf, sems):
    i = pl.program_id(0)
    n = pl.num_programs(0)
    td = x_ref.shape[0]
    t = n * td

    def issue(tile, slot):
        def start(r, c):
            for k in range(2):
                _row_copy(y_ref, pos_ref[k * t + tile * td + r], buf.at[slot, k], r,
                          sems.at[slot]).start(priority=k)
            return c

        lax.fori_loop(0, td, start, 0, unroll=DMA_UNROLL)

    @pl.when(i == 0)
    def _():
        issue(0, 0)

    slot = i % 2

    @pl.when(i + 1 < n)
    def _():
        issue(i + 1, 1 - slot)

    for k in range(2):
        pltpu.make_async_copy(y_ref.at[pl.ds(0, td)], buf.at[slot, k], sems.at[slot]).wait()
    route = route_ref[...]
    o_ref[...] = x_ref[...] + route[:, 2:3] * buf[slot, 0] + route[:, 3:4] * buf[slot, 1]


def _combine(x2, route, ys, pos, *, td):
    t, d = x2.shape
    return pl.pallas_call(
        _combine_body,
        out_shape=jax.ShapeDtypeStruct((t, d), F32),
        grid_spec=pltpu.PrefetchScalarGridSpec(
            num_scalar_prefetch=1,
            grid=(t // td,),
            in_specs=[
                pl.BlockSpec((td, d), lambda i, pos: (i, 0)),
                pl.BlockSpec((td, ROUTE_LANES), lambda i, pos: (i, 0)),
                pl.BlockSpec(memory_space=pl.ANY),
            ],
            out_specs=pl.BlockSpec((td, d), lambda i, pos: (i, 0)),
            scratch_shapes=[pltpu.VMEM((2, 2, td, d), F32), pltpu.SemaphoreType.DMA((2,))],
        ),
        compiler_params=_params("arbitrary"),
        name="expert_combine",
    )(pos, x2, route, ys)


def _route_plan(route, *, tm, td):
    t = route.shape[0]
    experts = jnp.concatenate([route[:, 0], route[:, 1]]).astype(I32)
    onehot = (experts[:, None] == jnp.arange(N_EXPERTS, dtype=I32)[None, :]).astype(I32)
    running = jnp.cumsum(onehot, axis=0)
    counts = running[-1]
    padded = ((counts + tm - 1) // tm) * tm
    ends = jnp.cumsum(padded)
    starts = ends - padded
    pos = jnp.sum(onehot * (starts[None, :] + running - 1), axis=1)
    n_tiles = (2 * t) // tm + N_EXPERTS
    tile_lo = jnp.arange(n_tiles, dtype=I32) * tm
    tile_expert = jnp.sum((tile_lo[:, None] >= ends[None, :]).astype(I32), axis=1)
    n_live = (ends[-1] // tm).astype(I32)
    last = tile_expert[jnp.maximum(n_live - 1, 0)]
    live = jnp.arange(n_tiles) < n_live
    tile_expert = jnp.where(live, tile_expert, last)
    prev = jnp.concatenate([jnp.full((1,), -1, I32), tile_expert[:-1]])
    tile_first = (live & (tile_expert != prev)).astype(I32)
    pos3 = pos.reshape(2, t // td, td).transpose(1, 0, 2).reshape(t // td, 1, 2 * td)
    pad_tiles = jnp.concatenate([jnp.where(padded > 0, ends - tm, -1), n_live.reshape(1)]).astype(I32)
    return pos, pos3, pad_tiles, tile_expert, tile_first, n_live.reshape(1), n_tiles * tm


def _rope_tables(seq):
    inv = 1.0 / (ROPE_THETA ** (jnp.arange(0, HEAD_DIM, 2, dtype=F32) / HEAD_DIM))
    ang = jnp.arange(seq, dtype=F32)[:, None] * inv[None, :]
    ang = jnp.concatenate([ang, ang], axis=-1)
    tile = lambda a: jnp.tile(a, (1, HEADS_PER_TILE))
    return tile(jnp.cos(ang)), tile(jnp.sin(ang))


def _tile_gain(g):
    return jnp.tile(g, HEADS_PER_TILE).reshape(1, LANE_TILE)


def _even_layer(x2, b, s, cos_t, sin_t, norm_mix, norm_ffn, w_in, v_gain, ws, bs, gq, gk, w_out, wg, wu, wd):
    p = _norm_proj(x2, norm_mix, w_in.astype(BF16))
    p3 = p.reshape(b, s, p.shape[1])
    a = _gmlp(p3, v_gain, ws, bs)
    q0 = 2 * GMLP_WIDTH // LANE_TILE
    n_groups = len(DIL_CONFIGS)
    accs = []
    for group, (window, dil) in enumerate(DIL_CONFIGS):
        assert window // dil == BAND
        qg, kg, vtg = _qkv_prep(p3, cos_t, sin_t, _tile_gain(gq), _tile_gain(gk), q_col=q0 + group,
                                k_col=q0 + n_groups + group, v_col=q0 + 2 * n_groups + group, n_ht=1, dil=dil,
                                tt=max(4, dil) * BAND, vblock=BAND, head_rows=BAND_ACC_ROWS, with_kmean=False)
        accs.append(_band_attn(qg, kg, vtg))
    w_out = w_out.astype(BF16)
    x3 = _even_out(x2.reshape(b, s, -1), a, accs, w_out[:GMLP_WIDTH], w_out[GMLP_WIDTH:])
    return _ffn(x3.reshape(b * s, -1), norm_ffn, wg.astype(BF16), wu.astype(BF16), wd.astype(BF16))


def _odd_layer(x2, b, s, cos_t, sin_t, norm_mix, norm_ffn, w_in, gq, gk, conv_w, w_out, router_w, wg, wu, wd,
               *, layer, tm=512, td=256):
    p = _norm_proj(x2, norm_mix, w_in.astype(BF16))
    p3 = p.reshape(b, s, p.shape[1])
    tiles = MOBA_WIDTH // LANE_TILE
    qp, kp, vtp, kmean = _qkv_prep(p3, cos_t, sin_t, _tile_gain(gq), _tile_gain(gk), q_col=0, k_col=tiles,
                                   v_col=2 * tiles, n_ht=tiles, dil=1, tt=8 * MOBA_BLOCK, vblock=MOBA_BLOCK,
                                   head_rows=MOBA_ACC_ROWS, with_kmean=True)
    c = _moba(qp, kp, vtp, kmean)
    dconv = _short_conv(p3, conv_w, col0=3 * MOBA_WIDTH // CONV_WIDTH)
    w_out = w_out.astype(BF16)
    router_pad = jnp.pad(router_w, ((0, 0), (0, ROUTE_LANES - N_EXPERTS)))
    x2, h2, route = _odd_out(x2, c.reshape(b * s, MOBA_WIDTH), dconv.reshape(b * s, CONV_WIDTH),
                             w_out[:MOBA_WIDTH], w_out[MOBA_WIDTH:], norm_ffn, router_pad)
    pos, pos3, pad_tiles, tile_expert, tile_first, n_live, n_rows = _route_plan(route, tm=tm, td=td)
    xs = _dispatch(h2, pos3, pad_tiles, n_rows, td=td, tm=tm)
    ys = _expert_ffn(xs, tile_expert, tile_first, n_live, wg, wu, wd, layer=layer, tm=tm)
    return _combine(x2, route, ys, pos, td=td)


def kernel(x, norm_mix, norm_ffn, w_in_ab, gmlp_v_gain, gmlp_ws, gmlp_bs, dil_q_gain, dil_k_gain, w_out_ab,
           ffn_w_gate, ffn_w_up, ffn_w_down, w_in_cd, moba_q_gain, moba_k_gain, conv_w, w_out_cd, router_w,
           moe_w_gate, moe_w_up, moe_w_down):
    b, s, d = x.shape
    depth = norm_mix.shape[0]
    cos_t, sin_t = _rope_tables(s)
    x2 = x.reshape(b * s, d)
    for layer in range(depth):
        i = layer // 2
        if layer % 2 == 0:
            x2 = _even_layer(x2, b, s, cos_t, sin_t, norm_mix[layer], norm_ffn[layer], w_in_ab[i], gmlp_v_gain[i],
                             gmlp_ws[i], gmlp_bs[i], dil_q_gain[i], dil_k_gain[i], w_out_ab[i], ffn_w_gate[i],
                             ffn_w_up[i], ffn_w_down[i])
        else:
            x2 = _odd_layer(x2, b, s, cos_t, sin_t, norm_mix[layer], norm_ffn[layer], w_in_cd[i], moba_q_gain[i],
                            moba_k_gain[i], conv_w[i], w_out_cd[i], router_w[i], moe_w_gate, moe_w_up,
                            moe_w_down, layer=i)
    return x2.reshape(b, s, d)
```

```python
import functools

import jax
import jax.numpy as jnp
from jax import lax
from jax.experimental import pallas as pl
from jax.experimental.pallas import tpu as pltpu

F32 = jnp.float32
BF16 = jnp.bfloat16
I32 = jnp.int32

D_MODEL = 1024
HEAD_DIM = 64
ROPE_THETA = 10000.0
NORM_EPS = 1e-6
LANE_TILE = 256
HEADS_PER_TILE = LANE_TILE // HEAD_DIM
GMLP_WIDTH = 256
GMLP_CHUNK = 128
DIL_CONFIGS = ((128, 1), (512, 4), (2048, 16))
DIL_WIDTH = 768
BAND = 128
BAND_ACC_ROWS = 128
BAND_L_ROW = HEAD_DIM
BAND_M_ROW = HEAD_DIM + 1
BAND_LOOKAHEAD = 5
STRIDE_LANES = 128
MOBA_WIDTH = 768
MOBA_BLOCK = 256
MOBA_TOPK = 3
MOBA_ACC_ROWS = HEAD_DIM + 16
LOG2_E = 1.4426950408889634
MOBA_M_INIT = -1e30
CONV_WIDTH = 256
N_EXPERTS = 8
ROUTE_LANES = 128
NEG_INF = float("-inf")
DMA_UNROLL = 8
ROW_LANES = 128
ROW_SUBLANES = 8

_NT = (((1,), (1,)), ((), ()))


def _params(*sem):
    return pltpu.CompilerParams(dimension_semantics=tuple(sem), vmem_limit_bytes=56 * 1024 * 1024)


def _lane_ids():
    return lax.broadcasted_iota(I32, (1, LANE_TILE), 1)


def _head_block_diag():
    r = lax.broadcasted_iota(I32, (LANE_TILE, LANE_TILE), 0) // HEAD_DIM
    c = lax.broadcasted_iota(I32, (LANE_TILE, LANE_TILE), 1) // HEAD_DIM
    return jnp.where(r == c, 1.0, 0.0).astype(BF16)


def _head_mean_sq(x, bd):
    s = x * x
    hi = s.astype(BF16)
    lo = (s - hi.astype(F32)).astype(BF16)
    tot = jnp.dot(hi, bd, preferred_element_type=F32) + jnp.dot(lo, bd, preferred_element_type=F32)
    return tot * (1.0 / HEAD_DIM)


def _head_norm(x, gain, bd):
    return x * lax.rsqrt(_head_mean_sq(x, bd) + NORM_EPS) * gain


def _row_norm(x, gain):
    return x * lax.rsqrt(jnp.mean(x * x, axis=-1, keepdims=True) + NORM_EPS) * gain


def _load_token_tiles(ref):
    n = ref.shape[0] // ROW_SUBLANES
    return jnp.concatenate([ref[pl.ds(c, n, stride=ROW_SUBLANES), :] for c in range(ROW_SUBLANES)], axis=1)


def _store_token_tiles(ref, val):
    n = ref.shape[0] // ROW_SUBLANES
    for c in range(ROW_SUBLANES):
        ref[pl.ds(c, n, stride=ROW_SUBLANES), :] = val[:, c * ROW_LANES:(c + 1) * ROW_LANES]


def _norm_proj_body(x_ref, g_ref, w_ref, o_ref, *, tn):
    hb = _row_norm(x_ref[...], g_ref[...]).astype(BF16)
    for c in range(o_ref.shape[-1] // tn):
        sl = slice(c * tn, (c + 1) * tn)
        o_ref[:, sl] = jnp.dot(hb, w_ref[:, sl], preferred_element_type=F32).astype(o_ref.dtype)


def _norm_proj(x2, gain, w_bf16, *, tm=512, tn=256):
    t, d = x2.shape
    n = w_bf16.shape[1]
    return pl.pallas_call(
        functools.partial(_norm_proj_body, tn=tn),
        out_shape=jax.ShapeDtypeStruct((t, n), BF16),
        grid=(t // tm,),
        in_specs=[
            pl.BlockSpec((tm, d), lambda i: (i, 0)),
            pl.BlockSpec((1, d), lambda i: (0, 0)),
            pl.BlockSpec((d, n), lambda i: (0, 0)),
        ],
        out_specs=pl.BlockSpec((tm, n), lambda i: (i, 0)),
        compiler_params=_params("parallel"),
        name="norm_proj",
    )(x2, gain.reshape(1, d), w_bf16)


def _gmlp_body(u_ref, v_ref, gain_ref, ws_ref, bias_ref, o_ref):
    lane = _lane_ids()
    bd = _head_block_diag()
    r = lax.broadcasted_iota(I32, (GMLP_CHUNK, GMLP_CHUNK), 0)
    c = lax.broadcasted_iota(I32, (GMLP_CHUNK, GMLP_CHUNK), 1)
    w_tril = [jnp.where(r >= c, ws_ref[g], 0.0).astype(BF16) for g in range(HEADS_PER_TILE)]
    for ch in range(o_ref.shape[0] // GMLP_CHUNK):
        sl = slice(ch * GMLP_CHUNK, (ch + 1) * GMLP_CHUNK)
        u = jax.nn.gelu(u_ref[sl, :].astype(F32))
        v = jax.nn.gelu(v_ref[sl, :].astype(F32))
        vn = _head_norm(v, gain_ref[...], bd).astype(BF16)
        y = bias_ref[...]
        for g in range(HEADS_PER_TILE):
            yg = jnp.dot(w_tril[g], vn, preferred_element_type=F32)
            y = y + jnp.where(lane // HEAD_DIM == g, yg, 0.0)
        o_ref[sl, :] = (u * y).astype(o_ref.dtype)


def _gmlp(p3, v_gain, ws, bs, *, tc=512):
    b, s, _ = p3.shape
    groups, chunk = bs.shape
    bias = jnp.repeat(bs.T, HEAD_DIM, axis=1)
    return pl.pallas_call(
        _gmlp_body,
        out_shape=jax.ShapeDtypeStruct((b, s, GMLP_WIDTH), BF16),
        grid=(b, s // tc),
        in_specs=[
            pl.BlockSpec((None, tc, LANE_TILE), lambda bi, i: (bi, i, 0)),
            pl.BlockSpec((None, tc, LANE_TILE), lambda bi, i: (bi, i, 1)),
            pl.BlockSpec((1, GMLP_WIDTH), lambda bi, i: (0, 0)),
            pl.BlockSpec((groups, chunk, chunk), lambda bi, i: (0, 0, 0)),
            pl.BlockSpec((chunk, GMLP_WIDTH), lambda bi, i: (0, 0)),
        ],
        out_specs=pl.BlockSpec((None, tc, GMLP_WIDTH), lambda bi, i: (bi, i, 0)),
        compiler_params=_params("parallel", "parallel"),
        name="gmlp_gate",
    )(p3, p3, v_gain.reshape(1, GMLP_WIDTH), ws, bias)


def _residues(val, scr, dil):
    if dil == 1:
        return [val]
    rows = val.shape[0] // dil
    slabs = LANE_TILE // STRIDE_LANES
    for c in range(slabs):
        scr[c] = val[:, c * STRIDE_LANES:(c + 1) * STRIDE_LANES]
    return [jnp.concatenate([scr[c, pl.ds(r, rows, stride=dil), :] for c in range(slabs)], axis=1)
            for r in range(dil)]


def _norm_rope(x, tab_a, tab_b, bd, lane):
    sq = (x * x).astype(BF16)
    xn = x * lax.rsqrt(jnp.dot(sq, bd, preferred_element_type=F32) * (1.0 / HEAD_DIM) + NORM_EPS)
    half = HEAD_DIM // 2
    partner = jnp.where((lane % HEAD_DIM) < half, pltpu.roll(xn, LANE_TILE - half, 1), pltpu.roll(xn, half, 1))
    return xn * tab_a + partner * tab_b


def _qkv_prep_body(q_ref, k_ref, v_ref, qa_ref, qb_ref, ka_ref, kb_ref, *rest, dil, with_kmean):
    n_out = 4 if with_kmean else 3
    qo_ref, ko_ref, vt_ref = rest[:3]
    scr = list(rest[n_out:]) + [None] * 3
    lane = _lane_ids()
    bd = _head_block_diag()
    qn = _norm_rope(q_ref[...].astype(F32), qa_ref[...], qb_ref[...], bd, lane)
    kn = _norm_rope(k_ref[...].astype(F32), ka_ref[...], kb_ref[...], bd, lane)
    if with_kmean:
        km_ref = rest[3]
        km_ref[...] = jnp.mean(kn.reshape(km_ref.shape[0], MOBA_BLOCK, LANE_TILE), axis=1)
    for r, piece in enumerate(_residues(qn, scr[0], dil)):
        qo_ref[r] = piece.astype(BF16)
    for r, piece in enumerate(_residues(kn, scr[1], dil)):
        ko_ref[r] = piece.astype(BF16)
    n_blocks, acc_w, vblock = vt_ref.shape[1:]
    head_rows = acc_w // HEADS_PER_TILE
    ones = jnp.ones((head_rows - HEAD_DIM, vblock), BF16)
    for r, piece in enumerate(_residues(v_ref[...].astype(F32), scr[2], dil)):
        for c in range(n_blocks):
            vt = piece[c * vblock:(c + 1) * vblock, :].T.astype(BF16)
            for h in range(HEADS_PER_TILE):
                vt_ref[r, c, h * head_rows:h * head_rows + HEAD_DIM, :] = vt[h * HEAD_DIM:(h + 1) * HEAD_DIM, :]
                vt_ref[r, c, h * head_rows + HEAD_DIM:(h + 1) * head_rows, :] = ones


def _rope_gain_tables(cos_t, sin_t, gain, scale):
    half = HEAD_DIM // 2
    swapped = jnp.concatenate([gain[half:], gain[:half]])
    sign = jnp.concatenate([-jnp.ones((half,), F32), jnp.ones((half,), F32)])
    tile = lambda v: jnp.tile(v, HEADS_PER_TILE)[None, :]
    return cos_t * (tile(gain) * scale), sin_t * (tile(sign * swapped) * scale)


def _qkv_prep(p3, cos_t, sin_t, gq, gk, *, q_col, k_col, v_col, n_ht, dil, tt, vblock, head_rows, with_kmean):
    b, s, _ = p3.shape
    length = s // dil
    tr = tt // dil
    acc_w = HEADS_PER_TILE * head_rows

    def col(c0):
        return pl.BlockSpec((None, tt, LANE_TILE), lambda bi, ht, i: (bi, i, c0 + ht))

    tab = pl.BlockSpec((tt, LANE_TILE), lambda bi, ht, i: (i, 0))
    tables = _rope_gain_tables(cos_t, sin_t, gq, HEAD_DIM ** -0.5 * LOG2_E) + _rope_gain_tables(cos_t, sin_t, gk, 1.0)
    qk_shape = jax.ShapeDtypeStruct((b, n_ht, dil, length, LANE_TILE), BF16)
    qk_spec = pl.BlockSpec((None, None, dil, tr, LANE_TILE), lambda bi, ht, i: (bi, ht, 0, i, 0))
    out_shape = [qk_shape, qk_shape,
                 jax.ShapeDtypeStruct((b, n_ht, dil, length // vblock, acc_w, vblock), BF16)]
    out_specs = [qk_spec, qk_spec,
                 pl.BlockSpec((None, None, dil, tr // vblock, acc_w, vblock), lambda bi, ht, i: (bi, ht, 0, i, 0, 0))]
    if with_kmean:
        out_shape.append(jax.ShapeDtypeStruct((b, n_ht, s // MOBA_BLOCK, LANE_TILE), F32))
        out_specs.append(pl.BlockSpec((None, None, tt // MOBA_BLOCK, LANE_TILE), lambda bi, ht, i: (bi, ht, i, 0)))
    scratch = [pltpu.VMEM((LANE_TILE // STRIDE_LANES, tt, STRIDE_LANES), F32)] * (3 if dil > 1 else 0)
    return pl.pallas_call(
        functools.partial(_qkv_prep_body, dil=dil, with_kmean=with_kmean),
        out_shape=out_shape,
        grid=(b, n_ht, s // tt),
        in_specs=[col(q_col), col(k_col), col(v_col), tab, tab, tab, tab],
        out_specs=out_specs,
        scratch_shapes=scratch,
        compiler_params=_params("parallel", "parallel", "parallel"),
        name=f"qkv_prep_d{dil}",
    )(p3, p3, p3, *tables)


def _band_attn_body(q_ref, k_ref, kp_ref, vt_ref, vtp_ref, o_ref):
    i = pl.program_id(2)
    lb = q_ref.shape[0]
    lane = _lane_ids()
    kk = lax.broadcasted_iota(I32, (2 * BAND, BAND), 0)
    qq = lax.broadcasted_iota(I32, (2 * BAND, BAND), 1) + BAND
    in_band = (qq - kk >= 0) & (qq - kk <= BAND)
    first_ok = in_band & ((kk >= BAND) | (i > 0))
    acc_row = lax.broadcasted_iota(I32, (BAND_ACC_ROWS, BAND), 0)
    k_all = jnp.concatenate([kp_ref[...], k_ref[...]], axis=0)
    chains = [(j, h) for j in range(lb // BAND) for h in range(HEADS_PER_TILE)]

    def scores(j, h):
        qj = q_ref[j * BAND:(j + 1) * BAND, :]
        qh = jnp.where(lane // HEAD_DIM == h, qj, jnp.zeros_like(qj))
        return lax.dot_general(k_all[j * BAND:(j + 2) * BAND], qh, _NT, preferred_element_type=F32)

    staged = [scores(*c) for c in chains[:BAND_LOOKAHEAD]]
    tiles = []
    for n, (j, h) in enumerate(chains):
        if n + BAND_LOOKAHEAD < len(chains):
            staged.append(scores(*chains[n + BAND_LOOKAHEAD]))
        rows = slice(h * BAND_ACC_ROWS, (h + 1) * BAND_ACC_ROWS)
        s = jnp.where(first_ok if j == 0 else in_band, staged[n], NEG_INF)
        m = jnp.max(s, axis=0, keepdims=True)
        p = jnp.exp2(s - m).astype(BF16)
        v_prev = vtp_ref[0, rows, :] if j == 0 else vt_ref[j - 1, rows, :]
        vw = jnp.concatenate([v_prev, vt_ref[j, rows, :]], axis=1)
        acc = jnp.dot(vw, p, preferred_element_type=F32)
        tiles.append(jnp.where(acc_row == BAND_M_ROW, m, acc))
        if h == HEADS_PER_TILE - 1:
            o_ref[j * BAND:(j + 1) * BAND, :] = jnp.concatenate(tiles, axis=0).T
            tiles = []


def _band_attn(qg, kg, vtg):
    b, _, dil, length, _ = qg.shape
    lb = min(length, 512)
    sub = lb // BAND
    acc_w = HEADS_PER_TILE * BAND_ACC_ROWS
    before = lambda i: jnp.maximum(i * sub - 1, 0)
    cur = pl.BlockSpec((None, None, None, lb, LANE_TILE), lambda bi, r, i: (bi, 0, r, i, 0))
    prev = pl.BlockSpec((None, None, None, BAND, LANE_TILE), lambda bi, r, i: (bi, 0, r, before(i), 0))
    vt_cur = pl.BlockSpec((None, None, None, sub, acc_w, BAND), lambda bi, r, i: (bi, 0, r, i, 0, 0))
    vt_prev = pl.BlockSpec((None, None, None, 1, acc_w, BAND), lambda bi, r, i: (bi, 0, r, before(i), 0, 0))
    return pl.pallas_call(
        _band_attn_body,
        out_shape=jax.ShapeDtypeStruct((b, dil, length, acc_w), F32),
        grid=(b, dil, length // lb),
        in_specs=[cur, cur, prev, vt_cur, vt_prev],
        out_specs=pl.BlockSpec((None, None, lb, acc_w), lambda bi, r, i: (bi, r, i, 0)),
        compiler_params=_params("parallel", "parallel", "arbitrary"),
        name=f"band_attn_d{dil}",
    )(qg, kg, kg, vtg, vtg)


def _even_out_body(x_ref, a_ref, o0_ref, o1_ref, o2_ref, wa_ref, wb_ref, out_ref, *scratch):
    tm = x_ref.shape[0]

    def natural_order(o_ref, scr):
        dil = o_ref.shape[0]
        if dil == 1:
            return [o_ref[0, :, h * BAND_ACC_ROWS:(h + 1) * BAND_ACC_ROWS] for h in range(HEADS_PER_TILE)]
        for r in range(dil):
            for h in range(HEADS_PER_TILE):
                scr[h, pl.ds(r, tm // dil, stride=dil), :] = o_ref[r, :, h * BAND_ACC_ROWS:(h + 1) * BAND_ACC_ROWS]
        return [scr[h] for h in range(HEADS_PER_TILE)]

    o_refs = (o0_ref, o1_ref, o2_ref)
    spare = list(scratch)
    groups = [natural_order(o, spare.pop(0) if o.shape[0] > 1 else None) for o in o_refs]
    acc = x_ref[...] + jnp.dot(a_ref[...], wa_ref[...], preferred_element_type=F32)
    for h in range(HEADS_PER_TILE):
        tiles = [g[h] for g in groups]
        tops = [t[:, BAND_M_ROW:BAND_M_ROW + 1] for t in tiles]
        top = jnp.maximum(jnp.maximum(tops[0], tops[1]), tops[2])
        ws = [jnp.exp2(m - top) for m in tops]
        num = sum(w * t[:, :HEAD_DIM] for w, t in zip(ws, tiles))
        den = sum(w * t[:, BAND_L_ROW:BAND_L_ROW + 1] for w, t in zip(ws, tiles))
        merged = (num / den).astype(BF16)
        acc = acc + jnp.dot(merged, wb_ref[h * HEAD_DIM:(h + 1) * HEAD_DIM, :], preferred_element_type=F32)
    out_ref[...] = acc


def _even_out(x3, a3, accs, wa, wb, *, tm=512):
    b, s, d = x3.shape
    acc_w = HEADS_PER_TILE * BAND_ACC_ROWS
    full = lambda a: pl.BlockSpec(a.shape, lambda bi, i: (0, 0))

    def acc_spec(a):
        dil = a.shape[1]
        return pl.BlockSpec((None, dil, tm // dil, acc_w), lambda bi, i: (bi, 0, i, 0))

    scratch = [pltpu.VMEM((HEADS_PER_TILE, tm, BAND_ACC_ROWS), F32) for a in accs if a.shape[1] > 1]
    return pl.pallas_call(
        _even_out_body,
        out_shape=jax.ShapeDtypeStruct((b, s, d), F32),
        grid=(b, s // tm),
        in_specs=[pl.BlockSpec((None, tm, d), lambda bi, i: (bi, i, 0)),
                  pl.BlockSpec((None, tm, GMLP_WIDTH), lambda bi, i: (bi, i, 0))]
        + [acc_spec(a) for a in accs] + [full(wa), full(wb)],
        out_specs=pl.BlockSpec((None, tm, d), lambda bi, i: (bi, i, 0)),
        scratch_shapes=scratch,
        compiler_params=_params("parallel", "parallel"),
        name="even_out_proj",
    )(x3, a3, *accs, wa, wb)


def _swiglu_step(h_bf16, wg, wu, wd):
    g = jnp.dot(h_bf16, wg, preferred_element_type=F32)
    u = jnp.dot(h_bf16, wu, preferred_element_type=F32)
    a = (g * jax.nn.sigmoid(g) * u).astype(BF16)
    return jnp.dot(a, wd, preferred_element_type=F32)


def _ffn_body(x_ref, g_ref, wg_ref, wu_ref, wd_ref, o_ref, *, tf):
    x = x_ref[...]
    h = _row_norm(x, g_ref[...]).astype(BF16)
    acc = x
    for c in range(wg_ref.shape[1] // tf):
        sl = slice(c * tf, (c + 1) * tf)
        acc = acc + _swiglu_step(h, wg_ref[:, sl], wu_ref[:, sl], wd_ref[sl, :])
    o_ref[...] = acc


def _ffn(x2, gain, wg, wu, wd, *, tm=512, tf=256):
    t, d = x2.shape
    resident = lambda a: pl.BlockSpec(a.shape, lambda i: (0, 0), pipeline_mode=pl.Buffered(1))
    return pl.pallas_call(
        functools.partial(_ffn_body, tf=tf),
        out_shape=jax.ShapeDtypeStruct((t, d), F32),
        grid=(t // tm,),
        in_specs=[
            pl.BlockSpec((tm, d), lambda i: (i, 0)),
            pl.BlockSpec((1, d), lambda i: (0, 0)),
            resident(wg), resident(wu), resident(wd),
        ],
        out_specs=pl.BlockSpec((tm, d), lambda i: (i, 0)),
        compiler_params=_params("parallel"),
        name="ffn_swiglu",
    )(x2, gain.reshape(1, d), wg, wu, wd)


def _expert_body(te_ref, first_ref, nv_ref, x_ref, wg_hbm, wu_hbm, wd_hbm, o_ref,
                 cg_s, cu_s, cd_s, sg_s, su_s, sd_s, sems, *, layer, tf):
    i = pl.program_id(0)
    nf = cg_s.shape[0]
    e = te_ref[i]

    def chunk_copies(f, slot):
        cols = pl.ds(f * tf, tf)
        return (pltpu.make_async_copy(wg_hbm.at[layer, e, :, cols], sg_s.at[slot], sems.at[0, slot]),
                pltpu.make_async_copy(wu_hbm.at[layer, e, :, cols], su_s.at[slot], sems.at[1, slot]),
                pltpu.make_async_copy(wd_hbm.at[layer, e, cols, :], sd_s.at[slot], sems.at[2, slot]))

    def compute(load_chunk):
        h = _load_token_tiles(x_ref).astype(BF16)
        acc = jnp.zeros(h.shape, F32)
        for f in range(nf):
            load_chunk(f)
            acc = acc + _swiglu_step(h, cg_s[f], cu_s[f], cd_s[f])
        _store_token_tiles(o_ref, acc)

    @pl.when(first_ref[i] == 1)
    def _():
        def load_chunk(f):
            if f + 1 < nf:
                for cp in chunk_copies(f + 1, (f + 1) % 2):
                    cp.start()
            for cp in chunk_copies(f, f % 2):
                cp.wait()
            cg_s[f] = sg_s[f % 2].astype(BF16)
            cu_s[f] = su_s[f % 2].astype(BF16)
            cd_s[f] = sd_s[f % 2].astype(BF16)

        for cp in chunk_copies(0, 0):
            cp.start()
        compute(load_chunk)

    @pl.when((first_ref[i] == 0) & (i < nv_ref[0]))
    def _():
        compute(lambda f: None)

    @pl.when(i >= nv_ref[0])
    def _():
        o_ref[...] = jnp.zeros_like(o_ref)


def _expert_ffn(xs, tile_expert, tile_first, n_live, wg, wu, wd, *, layer, tm, tf=512):
    p = xs.shape[0] // ROW_SUBLANES
    d, ff = wg.shape[-2:]
    nf = ff // tf
    rows = pl.BlockSpec((tm * ROW_SUBLANES, ROW_LANES), lambda i, te, first, nv: (i, 0))
    return pl.pallas_call(
        functools.partial(_expert_body, layer=layer, tf=tf),
        out_shape=jax.ShapeDtypeStruct(xs.shape, F32),
        grid_spec=pltpu.PrefetchScalarGridSpec(
            num_scalar_prefetch=3,
            grid=(p // tm,),
            in_specs=[
                rows,
                pl.BlockSpec(memory_space=pl.ANY),
                pl.BlockSpec(memory_space=pl.ANY),
                pl.BlockSpec(memory_space=pl.ANY),
            ],
            out_specs=rows,
            scratch_shapes=[
                pltpu.VMEM((nf, d, tf), BF16),
                pltpu.VMEM((nf, d, tf), BF16),
                pltpu.VMEM((nf, tf, d), BF16),
                pltpu.VMEM((2, d, tf), F32),
                pltpu.VMEM((2, d, tf), F32),
                pltpu.VMEM((2, tf, d), F32),
                pltpu.SemaphoreType.DMA((3, 2)),
            ],
        ),
        compiler_params=_params("arbitrary"),
        name="expert_swiglu",
    )(tile_expert, tile_first, n_live, xs, wg, wu, wd)


def _moba_body(q_ref, k_ref, vt_ref, km_ref, o_ref, sel_s, qh_s, sa_s, sb_s, m_s, out_s):
    i = pl.program_id(2)
    nblk = km_ref.shape[0]
    lane = _lane_ids()
    qn = q_ref[...]
    blk = lax.broadcasted_iota(I32, (nblk, MOBA_BLOCK), 0).astype(F32)
    first_blk = i.astype(F32)
    km = km_ref[...]
    krow = lax.broadcasted_iota(I32, (MOBA_BLOCK, MOBA_BLOCK), 0)
    qcol = lax.broadcasted_iota(I32, (MOBA_BLOCK, MOBA_BLOCK), 1)
    causal = krow <= qcol

    km_hi = km.astype(BF16)
    km_lo = (km - km_hi.astype(F32)).astype(BF16)
    acc_rows = [slice(h * MOBA_ACC_ROWS, (h + 1) * MOBA_ACC_ROWS) for h in range(HEADS_PER_TILE)]

    def keys(j):
        return k_ref[pl.ds(pl.multiple_of(j * MOBA_BLOCK, MOBA_BLOCK), MOBA_BLOCK), :]

    heads = range(HEADS_PER_TILE)
    qh = [jnp.where(lane // HEAD_DIM == h, qn, jnp.zeros_like(qn)) for h in heads]
    for h in heads:
        qh_s[h] = qh[h]
    gates = [lax.dot_general(km_hi, qh[h], _NT, preferred_element_type=F32)
             + lax.dot_general(km_lo, qh[h], _NT, preferred_element_type=F32) for h in heads]
    k_own = keys(i)
    own = [lax.dot_general(k_own, qh[h], _NT, preferred_element_type=F32) for h in heads]
    k_first = keys(0)
    for h in heads:
        sa_s[h] = lax.dot_general(k_first, qh[h], _NT, preferred_element_type=F32)

    for h in heads:
        gate = jnp.where(blk < first_blk, gates[h], NEG_INF)
        sel = jnp.zeros((nblk, MOBA_BLOCK), F32)
        for _ in range(MOBA_TOPK):
            top = jnp.max(gate, axis=0, keepdims=True)
            idx = jnp.min(jnp.where(gate == top, blk, float(nblk)), axis=0, keepdims=True)
            hit = blk == idx
            sel = jnp.where(hit & (top > NEG_INF), 1.0, sel)
            gate = jnp.where(hit, NEG_INF, gate)
        sel_s[h] = sel

    for h in heads:
        s = jnp.where(causal, own[h], NEG_INF)
        m0 = jnp.max(s, axis=0, keepdims=True)
        m_s[h] = m0
        out_s[acc_rows[h], :] = jnp.dot(vt_ref[i, acc_rows[h], :], jnp.exp2(s - m0).astype(BF16),
                                        preferred_element_type=F32)

    def stage(j_next, buf_next, j, buf):
        k_next = keys(jnp.minimum(j_next, nblk - 1))
        jc = jnp.minimum(j, nblk - 1)
        bar = jnp.where(j < i, 0.0, 2.0)
        for h in heads:
            buf_next[h] = lax.dot_general(k_next, qh_s[h], _NT, preferred_element_type=F32)
        for h in heads:
            bias = jnp.where(sel_s[h, pl.ds(jc, 1), :] > bar, 0.0, NEG_INF)
            s = buf[h]
            m = m_s[h]
            m_new = jnp.maximum(m, jnp.max(s, axis=0, keepdims=True) + bias)
            p = jnp.exp2(s + (bias - m_new)).astype(BF16)
            m_s[h] = m_new
            out_s[acc_rows[h], :] = (jnp.exp2(m - m_new) * out_s[acc_rows[h], :]
                                     + jnp.dot(vt_ref[jc, acc_rows[h], :], p, preferred_element_type=F32))

    def step(t, carry):
        stage(2 * t + 1, sb_s, 2 * t, sa_s)
        stage(2 * t + 2, sa_s, 2 * t + 1, sb_s)
        return carry

    lax.fori_loop(0, (i + 1) // 2, step, 0)

    outs = []
    for h in heads:
        acc = out_s[acc_rows[h], :]
        outs.append(acc[:HEAD_DIM, :] / acc[HEAD_DIM:HEAD_DIM + 1, :])
    o_ref[...] = jnp.concatenate(outs, axis=0).T.astype(o_ref.dtype)


def _moba(qp, kp, vtp, kmean):
    b, tiles, _, s, _ = qp.shape
    nblk = s // MOBA_BLOCK
    acc_w = HEADS_PER_TILE * MOBA_ACC_ROWS
    return pl.pallas_call(
        _moba_body,
        out_shape=jax.ShapeDtypeStruct((b, s, MOBA_WIDTH), BF16),
        grid=(b, tiles, nblk),
        in_specs=[
            pl.BlockSpec((None, None, None, MOBA_BLOCK, LANE_TILE), lambda bi, hg, i: (bi, hg, 0, i, 0)),
            pl.BlockSpec((None, None, None, s, LANE_TILE), lambda bi, hg, i: (bi, hg, 0, 0, 0)),
            pl.BlockSpec((None, None, None, nblk, acc_w, MOBA_BLOCK), lambda bi, hg, i: (bi, hg, 0, 0, 0, 0)),
            pl.BlockSpec((None, None, nblk, LANE_TILE), lambda bi, hg, i: (bi, hg, 0, 0)),
        ],
        out_specs=pl.BlockSpec((None, MOBA_BLOCK, LANE_TILE), lambda bi, hg, i: (bi, i, hg)),
        scratch_shapes=[
            pltpu.VMEM((HEADS_PER_TILE, nblk, MOBA_BLOCK), F32),
            pltpu.VMEM((HEADS_PER_TILE, MOBA_BLOCK, LANE_TILE), BF16),
            pltpu.VMEM((HEADS_PER_TILE, MOBA_BLOCK, MOBA_BLOCK), F32),
            pltpu.VMEM((HEADS_PER_TILE, MOBA_BLOCK, MOBA_BLOCK), F32),
            pltpu.VMEM((HEADS_PER_TILE, 1, MOBA_BLOCK), F32),
            pltpu.VMEM((HEADS_PER_TILE * MOBA_ACC_ROWS, MOBA_BLOCK), F32),
        ],
        compiler_params=_params("parallel", "parallel", "arbitrary"),
        name="moba_attn",
    )(qp, kp, vtp, kmean)


def _conv_body(bg_ref, cg_ref, xz_ref, w_ref, o_ref, *, ts):
    halo = 16
    w = w_ref[...]
    for c in range(o_ref.shape[0] // ts):
        lo = c * ts
        if c == 0:
            z = cg_ref[0:ts, :].astype(F32) * xz_ref[0:ts, :].astype(F32)
            zp = jnp.concatenate([jnp.zeros((halo, CONV_WIDTH), F32), z], axis=0)
        else:
            zp = cg_ref[lo - halo:lo + ts, :].astype(F32) * xz_ref[lo - halo:lo + ts, :].astype(F32)
        z1 = pltpu.roll(zp, 1, 0)[halo:]
        z2 = pltpu.roll(zp, 2, 0)[halo:]
        y = w[2:3, :] * zp[halo:] + w[1:2, :] * z1 + w[0:1, :] * z2
        o_ref[lo:lo + ts, :] = (bg_ref[lo:lo + ts, :].astype(F32) * y).astype(o_ref.dtype)


def _short_conv(p3, conv_w, *, col0, ts=512):
    b, s, _ = p3.shape
    taps = conv_w.shape[0]

    def col(c):
        return pl.BlockSpec((None, s, CONV_WIDTH), lambda bi: (bi, 0, col0 + c))

    return pl.pallas_call(
        functools.partial(_conv_body, ts=ts),
        out_shape=jax.ShapeDtypeStruct((b, s, CONV_WIDTH), BF16),
        grid=(b,),
        in_specs=[col(0), col(1), col(2), pl.BlockSpec((taps, CONV_WIDTH), lambda bi: (0, 0))],
        out_specs=pl.BlockSpec((None, s, CONV_WIDTH), lambda bi: (bi, 0, 0)),
        compiler_params=_params("parallel"),
        name="short_conv",
    )(p3, p3, p3, conv_w)


def _odd_out_body(x_ref, c_ref, d_ref, wc_ref, wd_ref, gain_ref, rw_ref, xo_ref, h_ref, route_ref):
    x = x_ref[...] + jnp.dot(c_ref[...], wc_ref[...], preferred_element_type=F32)
    x = x + jnp.dot(d_ref[...], wd_ref[...], preferred_element_type=F32)
    xo_ref[...] = x
    h = _row_norm(x, gain_ref[...])
    _store_token_tiles(h_ref, h)
    rw = rw_ref[...]
    h_hi = h.astype(BF16)
    h_lo = (h - h_hi.astype(F32)).astype(BF16)
    rw_hi = rw.astype(BF16)
    rw_lo = (rw - rw_hi.astype(F32)).astype(BF16)
    logits = (jnp.dot(h_hi, rw_hi, preferred_element_type=F32) + jnp.dot(h_lo, rw_hi, preferred_element_type=F32)
              + jnp.dot(h_hi, rw_lo, preferred_element_type=F32))
    lane = lax.broadcasted_iota(I32, logits.shape, 1)
    lane_f = lane.astype(F32)
    logits = jnp.where(lane < N_EXPERTS, logits, NEG_INF)
    v1 = jnp.max(logits, axis=-1, keepdims=True)
    i1 = jnp.min(jnp.where(logits == v1, lane_f, float(ROUTE_LANES)), axis=-1, keepdims=True)
    rest = jnp.where(lane_f == i1, NEG_INF, logits)
    v2 = jnp.max(rest, axis=-1, keepdims=True)
    i2 = jnp.min(jnp.where(rest == v2, lane_f, float(ROUTE_LANES)), axis=-1, keepdims=True)
    e = jnp.exp(v2 - v1)
    g1 = 1.0 / (1.0 + e)
    g2 = e / (1.0 + e)
    route = jnp.where(lane == 0, i1, 0.0)
    route = jnp.where(lane == 1, i2, route)
    route = jnp.where(lane == 2, g1, route)
    route_ref[...] = jnp.where(lane == 3, g2, route)


def _odd_out(x2, c2, d2, wc, wd, gain, router_pad, *, tm=512):
    t, d = x2.shape
    row = lambda w: pl.BlockSpec((tm, w), lambda i: (i, 0))
    full = lambda a: pl.BlockSpec(a.shape, lambda i: (0, 0))
    return pl.pallas_call(
        _odd_out_body,
        out_shape=[jax.ShapeDtypeStruct((t, d), F32), jax.ShapeDtypeStruct((t * ROW_SUBLANES, ROW_LANES), F32),
                   jax.ShapeDtypeStruct((t, ROUTE_LANES), F32)],
        grid=(t // tm,),
        in_specs=[row(d), row(MOBA_WIDTH), row(CONV_WIDTH), full(wc), full(wd),
                  pl.BlockSpec((1, d), lambda i: (0, 0)), full(router_pad)],
        out_specs=[row(d), pl.BlockSpec((tm * ROW_SUBLANES, ROW_LANES), lambda i: (i, 0)), row(ROUTE_LANES)],
        compiler_params=_params("parallel"),
        name="odd_out_proj_router",
    )(x2, c2, d2, wc, wd, gain.reshape(1, d), router_pad)


def _token_rows(token, count=1):
    return pl.ds(pl.multiple_of(token * ROW_SUBLANES, ROW_SUBLANES), count * ROW_SUBLANES)


def _row_copy(src_ref, src_row, dst_ref, dst_row, sem):
    return pltpu.make_async_copy(src_ref.at[_token_rows(src_row)], dst_ref.at[_token_rows(dst_row)], sem)


def _dispatch_body(pos_ref, pad_ref, h_ref, xs_ref, zeros_s, sem, pad_sem):
    td = h_ref.shape[0] // ROW_SUBLANES

    @pl.when(pl.program_id(0) == 0)
    def _():
        zeros_s[...] = jnp.zeros_like(zeros_s)
        tm = zeros_s.shape[0] // ROW_SUBLANES
        n_tiles = xs_ref.shape[0] // zeros_s.shape[0]
        n_live = pad_ref[N_EXPERTS]

        def fills():
            for e in range(N_EXPERTS):
                start = jnp.maximum(pad_ref[e], 0)
                yield pad_ref[e] >= 0, pltpu.make_async_copy(zeros_s, xs_ref.at[_token_rows(start, tm)], pad_sem)
            for c in range(N_EXPERTS):
                tile = n_tiles - 1 - c
                yield tile >= n_live, pltpu.make_async_copy(zeros_s, xs_ref.at[_token_rows(tile * tm, tm)], pad_sem)

        for has_tile, cp in fills():
            pl.when(has_tile)(cp.start)
        for has_tile, cp in fills():
            pl.when(has_tile)(cp.wait)

    def copies(r):
        return [_row_copy(h_ref, r, xs_ref, pos_ref[0, 0, k * td + r], sem) for k in range(2)]

    def start(r, c):
        for k, cp in enumerate(copies(r)):
            cp.start(priority=k)
        return c

    lax.fori_loop(0, td, start, 0, unroll=DMA_UNROLL)
    for _ in range(2):
        pltpu.make_async_copy(h_ref, xs_ref.at[_token_rows(0, td)], sem).wait()


def _dispatch(h3, pos3, pad_tiles, n_rows, *, td, tm):
    t = h3.shape[0] // ROW_SUBLANES
    return pl.pallas_call(
        _dispatch_body,
        out_shape=jax.ShapeDtypeStruct((n_rows * ROW_SUBLANES, ROW_LANES), F32),
        grid=(t // td,),
        in_specs=[
            pl.BlockSpec((1, 1, 2 * td), lambda i: (i, 0, 0), memory_space=pltpu.SMEM),
            pl.BlockSpec(memory_space=pltpu.SMEM),
            pl.BlockSpec((td * ROW_SUBLANES, ROW_LANES), lambda i: (i, 0)),
        ],
        out_specs=pl.BlockSpec(memory_space=pl.ANY),
        scratch_shapes=[pltpu.VMEM((tm * ROW_SUBLANES, ROW_LANES), F32), pltpu.SemaphoreType.DMA(()),
                        pltpu.SemaphoreType.DMA(())],
        compiler_params=_params("arbitrary"),
        name="expert_dispatch",
    )(pos3, pad_tiles, h3)


def _combine_body(pos_ref, x_ref, route_ref, y_ref, o_ref, buf, sems):
    i = pl.program_id(0)
    n = pl.num_programs(0)
    td = x_ref.shape[0]
    t = n * td

    def issue(tile, slot):
        def start(r, c):
            for k in range(2):
                _row_copy(y_ref, pos_ref[k * t + tile * td + r], buf.at[slot, k], r,
                          sems.at[slot]).start(priority=k)
            return c

        lax.fori_loop(0, td, start, 0, unroll=DMA_UNROLL)

    @pl.when(i == 0)
    def _():
        issue(0, 0)

    slot = i % 2

    @pl.when(i + 1 < n)
    def _():
        issue(i + 1, 1 - slot)

    for k in range(2):
        pltpu.make_async_copy(y_ref.at[_token_rows(0, td)], buf.at[slot, k], sems.at[slot]).wait()
    route = route_ref[...]
    o_ref[...] = (x_ref[...] + route[:, 2:3] * _load_token_tiles(buf.at[slot, 0])
                  + route[:, 3:4] * _load_token_tiles(buf.at[slot, 1]))


def _combine(x2, route, ys, pos, *, td):
    t, d = x2.shape
    return pl.pallas_call(
        _combine_body,
        out_shape=jax.ShapeDtypeStruct((t, d), F32),
        grid_spec=pltpu.PrefetchScalarGridSpec(
            num_scalar_prefetch=1,
            grid=(t // td,),
            in_specs=[
                pl.BlockSpec((td, d), lambda i, pos: (i, 0)),
                pl.BlockSpec((td, ROUTE_LANES), lambda i, pos: (i, 0)),
                pl.BlockSpec(memory_space=pl.ANY),
            ],
            out_specs=pl.BlockSpec((td, d), lambda i, pos: (i, 0)),
            scratch_shapes=[pltpu.VMEM((2, 2, td * ROW_SUBLANES, ROW_LANES), F32), pltpu.SemaphoreType.DMA((2,))],
        ),
        compiler_params=_params("arbitrary"),
        name="expert_combine",
    )(pos, x2, route, ys)


def _route_plan(route, *, tm, td):
    t = route.shape[0]
    experts = jnp.concatenate([route[:, 0], route[:, 1]]).astype(I32)
    onehot = (experts[:, None] == jnp.arange(N_EXPERTS, dtype=I32)[None, :]).astype(I32)
    running = jnp.cumsum(onehot, axis=0)
    counts = running[-1]
    padded = ((counts + tm - 1) // tm) * tm
    ends = jnp.cumsum(padded)
    starts = ends - padded
    pos = jnp.sum(onehot * (starts[None, :] + running - 1), axis=1)
    n_tiles = (2 * t) // tm + N_EXPERTS
    tile_lo = jnp.arange(n_tiles, dtype=I32) * tm
    tile_expert = jnp.sum((tile_lo[:, None] >= ends[None, :]).astype(I32), axis=1)
    n_live = (ends[-1] // tm).astype(I32)
    last = tile_expert[jnp.maximum(n_live - 1, 0)]
    live = jnp.arange(n_tiles) < n_live
    tile_expert = jnp.where(live, tile_expert, last)
    prev = jnp.concatenate([jnp.full((1,), -1, I32), tile_expert[:-1]])
    tile_first = (live & (tile_expert != prev)).astype(I32)
    pos3 = pos.reshape(2, t // td, td).transpose(1, 0, 2).reshape(t // td, 1, 2 * td)
    pad_tiles = jnp.concatenate([jnp.where(padded > 0, ends - tm, -1), n_live.reshape(1)]).astype(I32)
    return pos, pos3, pad_tiles, tile_expert, tile_first, n_live.reshape(1), n_tiles * tm


def _rope_tables(seq):
    inv = 1.0 / (ROPE_THETA ** (jnp.arange(0, HEAD_DIM, 2, dtype=F32) / HEAD_DIM))
    ang = jnp.arange(seq, dtype=F32)[:, None] * inv[None, :]
    ang = jnp.concatenate([ang, ang], axis=-1)
    tile = lambda a: jnp.tile(a, (1, HEADS_PER_TILE))
    return tile(jnp.cos(ang)), tile(jnp.sin(ang))


def _even_layer(x2, b, s, cos_t, sin_t, norm_mix, norm_ffn, w_in, v_gain, ws, bs, gq, gk, w_out, wg, wu, wd):
    p = _norm_proj(x2, norm_mix, w_in.astype(BF16))
    p3 = p.reshape(b, s, p.shape[1])
    a = _gmlp(p3, v_gain, ws, bs)
    q0 = 2 * GMLP_WIDTH // LANE_TILE
    n_groups = len(DIL_CONFIGS)
    accs = []
    for group, (window, dil) in enumerate(DIL_CONFIGS):
        assert window // dil == BAND
        qg, kg, vtg = _qkv_prep(p3, cos_t, sin_t, gq, gk, q_col=q0 + group,
                                k_col=q0 + n_groups + group, v_col=q0 + 2 * n_groups + group, n_ht=1, dil=dil,
                                tt=max(4, dil) * BAND, vblock=BAND, head_rows=BAND_ACC_ROWS, with_kmean=False)
        accs.append(_band_attn(qg, kg, vtg))
    w_out = w_out.astype(BF16)
    x3 = _even_out(x2.reshape(b, s, -1), a, accs, w_out[:GMLP_WIDTH], w_out[GMLP_WIDTH:])
    return _ffn(x3.reshape(b * s, -1), norm_ffn, wg.astype(BF16), wu.astype(BF16), wd.astype(BF16))


def _odd_layer(x2, b, s, cos_t, sin_t, norm_mix, norm_ffn, w_in, gq, gk, conv_w, w_out, router_w, wg, wu, wd,
               *, layer, tm=512, td=256):
    p = _norm_proj(x2, norm_mix, w_in.astype(BF16))
    p3 = p.reshape(b, s, p.shape[1])
    tiles = MOBA_WIDTH // LANE_TILE
    qp, kp, vtp, kmean = _qkv_prep(p3, cos_t, sin_t, gq, gk, q_col=0, k_col=tiles,
                                   v_col=2 * tiles, n_ht=tiles, dil=1, tt=8 * MOBA_BLOCK, vblock=MOBA_BLOCK,
                                   head_rows=MOBA_ACC_ROWS, with_kmean=True)
    c = _moba(qp, kp, vtp, kmean)
    dconv = _short_conv(p3, conv_w, col0=3 * MOBA_WIDTH // CONV_WIDTH)
    w_out = w_out.astype(BF16)
    router_pad = jnp.pad(router_w, ((0, 0), (0, ROUTE_LANES - N_EXPERTS)))
    x2, h2, route = _odd_out(x2, c.reshape(b * s, MOBA_WIDTH), dconv.reshape(b * s, CONV_WIDTH),
                             w_out[:MOBA_WIDTH], w_out[MOBA_WIDTH:], norm_ffn, router_pad)
    pos, pos3, pad_tiles, tile_expert, tile_first, n_live, n_rows = _route_plan(route, tm=tm, td=td)
    xs = _dispatch(h2, pos3, pad_tiles, n_rows, td=td, tm=tm)
    ys = _expert_ffn(xs, tile_expert, tile_first, n_live, wg, wu, wd, layer=layer, tm=tm)
    return _combine(x2, route, ys, pos, td=td)


def kernel(x, norm_mix, norm_ffn, w_in_ab, gmlp_v_gain, gmlp_ws, gmlp_bs, dil_q_gain, dil_k_gain, w_out_ab,
           ffn_w_gate, ffn_w_up, ffn_w_down, w_in_cd, moba_q_gain, moba_k_gain, conv_w, w_out_cd, router_w,
           moe_w_gate, moe_w_up, moe_w_down):
    b, s, d = x.shape
    depth = norm_mix.shape[0]
    cos_t, sin_t = _rope_tables(s)
    x2 = x.reshape(b * s, d)
    for layer in range(depth):
        i = layer // 2
        if layer % 2 == 0:
            x2 = _even_layer(x2, b, s, cos_t, sin_t, norm_mix[layer], norm_ffn[layer], w_in_ab[i], gmlp_v_gain[i],
                             gmlp_ws[i], gmlp_bs[i], dil_q_gain[i], dil_k_gain[i], w_out_ab[i], ffn_w_gate[i],
                             ffn_w_up[i], ffn_w_down[i])
        else:
            x2 = _odd_layer(x2, b, s, cos_t, sin_t, norm_mix[layer], norm_ffn[layer], w_in_cd[i], moba_q_gain[i],
                            moba_k_gain[i], conv_w[i], w_out_cd[i], router_w[i], moe_w_gate, moe_w_up,
                            moe_w_down, layer=i)
    return x2.reshape(b, s, d)
```

```python
import functools

import jax
import jax.numpy as jnp
from jax import lax
from jax.experimental import pallas as pl
from jax.experimental.pallas import tpu as pltpu

F32 = jnp.float32
BF16 = jnp.bfloat16
I32 = jnp.int32

D_MODEL = 1024
HEAD_DIM = 64
ROPE_THETA = 10000.0
NORM_EPS = 1e-6
LANE_TILE = 256
HEADS_PER_TILE = LANE_TILE // HEAD_DIM
GMLP_WIDTH = 256
GMLP_CHUNK = 128
DIL_CONFIGS = ((128, 1), (512, 4), (2048, 16))
DIL_WIDTH = 768
BAND = 128
BAND_ACC_ROWS = 128
BAND_L_ROW = HEAD_DIM
BAND_M_ROW = HEAD_DIM + 1
BAND_LOOKAHEAD = 5
STRIDE_LANES = 128
MOBA_WIDTH = 768
MOBA_BLOCK = 256
MOBA_TOPK = 3
MOBA_ACC_ROWS = HEAD_DIM + 16
LOG2_E = 1.4426950408889634
MOBA_M_INIT = -1e30
CONV_WIDTH = 256
N_EXPERTS = 8
ROUTE_LANES = 128
NEG_INF = float("-inf")
DMA_UNROLL = 8
ROW_LANES = 128
ROW_SUBLANES = 8

_NT = (((1,), (1,)), ((), ()))


def _params(*sem):
    return pltpu.CompilerParams(dimension_semantics=tuple(sem), vmem_limit_bytes=56 * 1024 * 1024)


def _lane_ids():
    return lax.broadcasted_iota(I32, (1, LANE_TILE), 1)


def _head_block_diag():
    r = lax.broadcasted_iota(I32, (LANE_TILE, LANE_TILE), 0) // HEAD_DIM
    c = lax.broadcasted_iota(I32, (LANE_TILE, LANE_TILE), 1) // HEAD_DIM
    return jnp.where(r == c, 1.0, 0.0).astype(BF16)


def _head_mean_sq(x, bd):
    s = x * x
    hi = s.astype(BF16)
    lo = (s - hi.astype(F32)).astype(BF16)
    tot = jnp.dot(hi, bd, preferred_element_type=F32) + jnp.dot(lo, bd, preferred_element_type=F32)
    return tot * (1.0 / HEAD_DIM)


def _head_norm(x, gain, bd):
    return x * lax.rsqrt(_head_mean_sq(x, bd) + NORM_EPS) * gain


def _row_norm(x, gain):
    return x * lax.rsqrt(jnp.mean(x * x, axis=-1, keepdims=True) + NORM_EPS) * gain


def _load_token_tiles(ref):
    n = ref.shape[0] // ROW_SUBLANES
    return jnp.concatenate([ref[pl.ds(c, n, stride=ROW_SUBLANES), :] for c in range(ROW_SUBLANES)], axis=1)


def _store_token_tiles(ref, val):
    n = ref.shape[0] // ROW_SUBLANES
    for c in range(ROW_SUBLANES):
        ref[pl.ds(c, n, stride=ROW_SUBLANES), :] = val[:, c * ROW_LANES:(c + 1) * ROW_LANES]


def _norm_proj_body(x_ref, g_ref, w_ref, o_ref, *, tn):
    hb = _row_norm(x_ref[...], g_ref[...]).astype(BF16)
    for c in range(o_ref.shape[-1] // tn):
        sl = slice(c * tn, (c + 1) * tn)
        o_ref[:, sl] = jnp.dot(hb, w_ref[:, sl], preferred_element_type=F32).astype(o_ref.dtype)


def _norm_proj(x2, gain, w_bf16, *, tm=512, tn=256):
    t, d = x2.shape
    n = w_bf16.shape[1]
    return pl.pallas_call(
        functools.partial(_norm_proj_body, tn=tn),
        out_shape=jax.ShapeDtypeStruct((t, n), BF16),
        grid=(t // tm,),
        in_specs=[
            pl.BlockSpec((tm, d), lambda i: (i, 0)),
            pl.BlockSpec((1, d), lambda i: (0, 0)),
            pl.BlockSpec((d, n), lambda i: (0, 0)),
        ],
        out_specs=pl.BlockSpec((tm, n), lambda i: (i, 0)),
        compiler_params=_params("parallel"),
        name="norm_proj",
    )(x2, gain.reshape(1, d), w_bf16)


def _gmlp_body(u_ref, v_ref, gain_ref, ws_ref, bias_ref, o_ref):
    lane = _lane_ids()
    bd = _head_block_diag()
    r = lax.broadcasted_iota(I32, (GMLP_CHUNK, GMLP_CHUNK), 0)
    c = lax.broadcasted_iota(I32, (GMLP_CHUNK, GMLP_CHUNK), 1)
    w_tril = [jnp.where(r >= c, ws_ref[g], 0.0).astype(BF16) for g in range(HEADS_PER_TILE)]
    for ch in range(o_ref.shape[0] // GMLP_CHUNK):
        sl = slice(ch * GMLP_CHUNK, (ch + 1) * GMLP_CHUNK)
        u = jax.nn.gelu(u_ref[sl, :].astype(F32))
        v = jax.nn.gelu(v_ref[sl, :].astype(F32))
        vn = _head_norm(v, gain_ref[...], bd).astype(BF16)
        y = bias_ref[...]
        for g in range(HEADS_PER_TILE):
            yg = jnp.dot(w_tril[g], vn, preferred_element_type=F32)
            y = y + jnp.where(lane // HEAD_DIM == g, yg, 0.0)
        o_ref[sl, :] = (u * y).astype(o_ref.dtype)


def _gmlp(p3, v_gain, ws, bs, *, tc=512):
    b, s, _ = p3.shape
    groups, chunk = bs.shape
    bias = jnp.repeat(bs.T, HEAD_DIM, axis=1)
    return pl.pallas_call(
        _gmlp_body,
        out_shape=jax.ShapeDtypeStruct((b, s, GMLP_WIDTH), BF16),
        grid=(b, s // tc),
        in_specs=[
            pl.BlockSpec((None, tc, LANE_TILE), lambda bi, i: (bi, i, 0)),
            pl.BlockSpec((None, tc, LANE_TILE), lambda bi, i: (bi, i, 1)),
            pl.BlockSpec((1, GMLP_WIDTH), lambda bi, i: (0, 0)),
            pl.BlockSpec((groups, chunk, chunk), lambda bi, i: (0, 0, 0)),
            pl.BlockSpec((chunk, GMLP_WIDTH), lambda bi, i: (0, 0)),
        ],
        out_specs=pl.BlockSpec((None, tc, GMLP_WIDTH), lambda bi, i: (bi, i, 0)),
        compiler_params=_params("parallel", "parallel"),
        name="gmlp_gate",
    )(p3, p3, v_gain.reshape(1, GMLP_WIDTH), ws, bias)


def _residues(val, scr, dil):
    if dil == 1:
        return [val]
    rows = val.shape[0] // dil
    slabs = LANE_TILE // STRIDE_LANES
    for c in range(slabs):
        scr[c] = val[:, c * STRIDE_LANES:(c + 1) * STRIDE_LANES]
    return [jnp.concatenate([scr[c, pl.ds(r, rows, stride=dil), :] for c in range(slabs)], axis=1)
            for r in range(dil)]


def _norm_rope(x, tab_a, tab_b, bd, lane):
    sq = (x * x).astype(BF16)
    xn = x * lax.rsqrt(jnp.dot(sq, bd, preferred_element_type=F32) * (1.0 / HEAD_DIM) + NORM_EPS)
    half = HEAD_DIM // 2
    partner = jnp.where((lane % HEAD_DIM) < half, pltpu.roll(xn, LANE_TILE - half, 1), pltpu.roll(xn, half, 1))
    return xn * tab_a + partner * tab_b


def _qkv_prep_body(q_ref, k_ref, v_ref, qa_ref, qb_ref, ka_ref, kb_ref, *rest, dil, with_kmean):
    n_out = 4 if with_kmean else 3
    qo_ref, ko_ref, vt_ref = rest[:3]
    scr = list(rest[n_out:]) + [None] * 3
    lane = _lane_ids()
    bd = _head_block_diag()
    qn = _norm_rope(q_ref[...].astype(F32), qa_ref[...], qb_ref[...], bd, lane)
    kn = _norm_rope(k_ref[...].astype(F32), ka_ref[...], kb_ref[...], bd, lane)
    if with_kmean:
        km_ref = rest[3]
        km_ref[...] = jnp.mean(kn.reshape(km_ref.shape[0], MOBA_BLOCK, LANE_TILE), axis=1)
    for r, piece in enumerate(_residues(qn, scr[0], dil)):
        qo_ref[r] = piece.astype(BF16)
    for r, piece in enumerate(_residues(kn, scr[1], dil)):
        ko_ref[r] = piece.astype(BF16)
    n_blocks, acc_w, vblock = vt_ref.shape[1:]
    head_rows = acc_w // HEADS_PER_TILE
    ones = jnp.ones((head_rows - HEAD_DIM, vblock), BF16)
    for r, piece in enumerate(_residues(v_ref[...].astype(F32), scr[2], dil)):
        for c in range(n_blocks):
            vt = piece[c * vblock:(c + 1) * vblock, :].T.astype(BF16)
            for h in range(HEADS_PER_TILE):
                vt_ref[r, c, h * head_rows:h * head_rows + HEAD_DIM, :] = vt[h * HEAD_DIM:(h + 1) * HEAD_DIM, :]
                vt_ref[r, c, h * head_rows + HEAD_DIM:(h + 1) * head_rows, :] = ones


def _rope_gain_tables(cos_t, sin_t, gain, scale):
    half = HEAD_DIM // 2
    swapped = jnp.concatenate([gain[half:], gain[:half]])
    sign = jnp.concatenate([-jnp.ones((half,), F32), jnp.ones((half,), F32)])
    tile = lambda v: jnp.tile(v, HEADS_PER_TILE)[None, :]
    return cos_t * (tile(gain) * scale), sin_t * (tile(sign * swapped) * scale)


def _qkv_prep(p3, cos_t, sin_t, gq, gk, *, q_col, k_col, v_col, n_ht, dil, tt, vblock, head_rows, with_kmean):
    b, s, _ = p3.shape
    length = s // dil
    tr = tt // dil
    acc_w = HEADS_PER_TILE * head_rows

    def col(c0):
        return pl.BlockSpec((None, tt, LANE_TILE), lambda i, bi, ht: (bi, i, c0 + ht))

    tab = pl.BlockSpec((tt, LANE_TILE), lambda i, bi, ht: (i, 0))
    tables = _rope_gain_tables(cos_t, sin_t, gq, HEAD_DIM ** -0.5 * LOG2_E) + _rope_gain_tables(cos_t, sin_t, gk, 1.0)
    qk_shape = jax.ShapeDtypeStruct((b, n_ht, dil, length, LANE_TILE), BF16)
    qk_spec = pl.BlockSpec((None, None, dil, tr, LANE_TILE), lambda i, bi, ht: (bi, ht, 0, i, 0))
    out_shape = [qk_shape, qk_shape,
                 jax.ShapeDtypeStruct((b, n_ht, dil, length // vblock, acc_w, vblock), BF16)]
    out_specs = [qk_spec, qk_spec,
                 pl.BlockSpec((None, None, dil, tr // vblock, acc_w, vblock), lambda i, bi, ht: (bi, ht, 0, i, 0, 0))]
    if with_kmean:
        out_shape.append(jax.ShapeDtypeStruct((b, n_ht, s // MOBA_BLOCK, LANE_TILE), F32))
        out_specs.append(pl.BlockSpec((None, None, tt // MOBA_BLOCK, LANE_TILE), lambda i, bi, ht: (bi, ht, i, 0)))
    scratch = [pltpu.VMEM((LANE_TILE // STRIDE_LANES, tt, STRIDE_LANES), F32)] * (3 if dil > 1 else 0)
    return pl.pallas_call(
        functools.partial(_qkv_prep_body, dil=dil, with_kmean=with_kmean),
        out_shape=out_shape,
        grid=(s // tt, b, n_ht),
        in_specs=[col(q_col), col(k_col), col(v_col), tab, tab, tab, tab],
        out_specs=out_specs,
        scratch_shapes=scratch,
        compiler_params=_params("parallel", "parallel", "parallel"),
        name=f"qkv_prep_d{dil}",
    )(p3, p3, p3, *tables)


def _band_attn_body(q_ref, k_ref, kp_ref, vt_ref, vtp_ref, o_ref):
    i = pl.program_id(2)
    lb = q_ref.shape[0]
    lane = _lane_ids()
    kk = lax.broadcasted_iota(I32, (2 * BAND, BAND), 0)
    qq = lax.broadcasted_iota(I32, (2 * BAND, BAND), 1) + BAND
    in_band = (qq - kk >= 0) & (qq - kk <= BAND)
    first_ok = in_band & ((kk >= BAND) | (i > 0))
    acc_row = lax.broadcasted_iota(I32, (BAND_ACC_ROWS, BAND), 0)
    k_all = jnp.concatenate([kp_ref[...], k_ref[...]], axis=0)
    chains = [(j, h) for j in range(lb // BAND) for h in range(HEADS_PER_TILE)]

    def scores(j, h):
        qj = q_ref[j * BAND:(j + 1) * BAND, :]
        qh = jnp.where(lane // HEAD_DIM == h, qj, jnp.zeros_like(qj))
        return lax.dot_general(k_all[j * BAND:(j + 2) * BAND], qh, _NT, preferred_element_type=F32)

    staged = [scores(*c) for c in chains[:BAND_LOOKAHEAD]]
    tiles = []
    for n, (j, h) in enumerate(chains):
        if n + BAND_LOOKAHEAD < len(chains):
            staged.append(scores(*chains[n + BAND_LOOKAHEAD]))
        rows = slice(h * BAND_ACC_ROWS, (h + 1) * BAND_ACC_ROWS)
        s = jnp.where(first_ok if j == 0 else in_band, staged[n], NEG_INF)
        m = jnp.max(s, axis=0, keepdims=True)
        p = jnp.exp2(s - m).astype(BF16)
        v_prev = vtp_ref[0, rows, :] if j == 0 else vt_ref[j - 1, rows, :]
        vw = jnp.concatenate([v_prev, vt_ref[j, rows, :]], axis=1)
        acc = jnp.dot(vw, p, preferred_element_type=F32)
        tiles.append(jnp.where(acc_row == BAND_M_ROW, m, acc))
        if h == HEADS_PER_TILE - 1:
            o_ref[j * BAND:(j + 1) * BAND, :] = jnp.concatenate(tiles, axis=0).T
            tiles = []


def _band_attn(qg, kg, vtg):
    b, _, dil, length, _ = qg.shape
    lb = min(length, 512)
    sub = lb // BAND
    acc_w = HEADS_PER_TILE * BAND_ACC_ROWS
    before = lambda i: jnp.maximum(i * sub - 1, 0)
    cur = pl.BlockSpec((None, None, None, lb, LANE_TILE), lambda bi, r, i: (bi, 0, r, i, 0))
    prev = pl.BlockSpec((None, None, None, BAND, LANE_TILE), lambda bi, r, i: (bi, 0, r, before(i), 0))
    vt_cur = pl.BlockSpec((None, None, None, sub, acc_w, BAND), lambda bi, r, i: (bi, 0, r, i, 0, 0))
    vt_prev = pl.BlockSpec((None, None, None, 1, acc_w, BAND), lambda bi, r, i: (bi, 0, r, before(i), 0, 0))
    return pl.pallas_call(
        _band_attn_body,
        out_shape=jax.ShapeDtypeStruct((b, dil, length, acc_w), F32),
        grid=(b, dil, length // lb),
        in_specs=[cur, cur, prev, vt_cur, vt_prev],
        out_specs=pl.BlockSpec((None, None, lb, acc_w), lambda bi, r, i: (bi, r, i, 0)),
        compiler_params=_params("parallel", "parallel", "arbitrary"),
        name=f"band_attn_d{dil}",
    )(qg, kg, kg, vtg, vtg)


def _even_out_body(x_ref, a_ref, o0_ref, o1_ref, o2_ref, wa_ref, wb_ref, out_ref, *scratch):
    tm = x_ref.shape[0]

    def natural_order(o_ref, scr):
        dil = o_ref.shape[0]
        if dil == 1:
            return [o_ref[0, :, h * BAND_ACC_ROWS:(h + 1) * BAND_ACC_ROWS] for h in range(HEADS_PER_TILE)]
        for r in range(dil):
            for h in range(HEADS_PER_TILE):
                scr[h, pl.ds(r, tm // dil, stride=dil), :] = o_ref[r, :, h * BAND_ACC_ROWS:(h + 1) * BAND_ACC_ROWS]
        return [scr[h] for h in range(HEADS_PER_TILE)]

    o_refs = (o0_ref, o1_ref, o2_ref)
    spare = list(scratch)
    groups = [natural_order(o, spare.pop(0) if o.shape[0] > 1 else None) for o in o_refs]
    acc = x_ref[...] + jnp.dot(a_ref[...], wa_ref[...], preferred_element_type=F32)
    for h in range(HEADS_PER_TILE):
        tiles = [g[h] for g in groups]
        tops = [t[:, BAND_M_ROW:BAND_M_ROW + 1] for t in tiles]
        top = jnp.maximum(jnp.maximum(tops[0], tops[1]), tops[2])
        ws = [jnp.exp2(m - top) for m in tops]
        num = sum(w * t[:, :HEAD_DIM] for w, t in zip(ws, tiles))
        den = sum(w * t[:, BAND_L_ROW:BAND_L_ROW + 1] for w, t in zip(ws, tiles))
        merged = (num / den).astype(BF16)
        acc = acc + jnp.dot(merged, wb_ref[h * HEAD_DIM:(h + 1) * HEAD_DIM, :], preferred_element_type=F32)
    out_ref[...] = acc


def _even_out(x3, a3, accs, wa, wb, *, tm=512):
    b, s, d = x3.shape
    acc_w = HEADS_PER_TILE * BAND_ACC_ROWS
    full = lambda a: pl.BlockSpec(a.shape, lambda bi, i: (0, 0))

    def acc_spec(a):
        dil = a.shape[1]
        return pl.BlockSpec((None, dil, tm // dil, acc_w), lambda bi, i: (bi, 0, i, 0))

    scratch = [pltpu.VMEM((HEADS_PER_TILE, tm, BAND_ACC_ROWS), F32) for a in accs if a.shape[1] > 1]
    return pl.pallas_call(
        _even_out_body,
        out_shape=jax.ShapeDtypeStruct((b, s, d), F32),
        grid=(b, s // tm),
        in_specs=[pl.BlockSpec((None, tm, d), lambda bi, i: (bi, i, 0)),
                  pl.BlockSpec((None, tm, GMLP_WIDTH), lambda bi, i: (bi, i, 0))]
        + [acc_spec(a) for a in accs] + [full(wa), full(wb)],
        out_specs=pl.BlockSpec((None, tm, d), lambda bi, i: (bi, i, 0)),
        scratch_shapes=scratch,
        compiler_params=_params("parallel", "parallel"),
        name="even_out_proj",
    )(x3, a3, *accs, wa, wb)


def _swiglu_step(h_bf16, wg, wu, wd):
    g = jnp.dot(h_bf16, wg, preferred_element_type=F32)
    u = jnp.dot(h_bf16, wu, preferred_element_type=F32)
    a = (g * jax.nn.sigmoid(g) * u).astype(BF16)
    return jnp.dot(a, wd, preferred_element_type=F32)


def _ffn_body(x_ref, g_ref, wg_ref, wu_ref, wd_ref, o_ref, *, tf):
    x = x_ref[...]
    h = _row_norm(x, g_ref[...]).astype(BF16)
    acc = x
    for c in range(wg_ref.shape[1] // tf):
        sl = slice(c * tf, (c + 1) * tf)
        acc = acc + _swiglu_step(h, wg_ref[:, sl], wu_ref[:, sl], wd_ref[sl, :])
    o_ref[...] = acc


def _ffn(x2, gain, wg, wu, wd, *, tm=512, tf=256):
    t, d = x2.shape
    resident = lambda a: pl.BlockSpec(a.shape, lambda i: (0, 0), pipeline_mode=pl.Buffered(1))
    return pl.pallas_call(
        functools.partial(_ffn_body, tf=tf),
        out_shape=jax.ShapeDtypeStruct((t, d), F32),
        grid=(t // tm,),
        in_specs=[
            pl.BlockSpec((tm, d), lambda i: (i, 0)),
            pl.BlockSpec((1, d), lambda i: (0, 0)),
            resident(wg), resident(wu), resident(wd),
        ],
        out_specs=pl.BlockSpec((tm, d), lambda i: (i, 0)),
        compiler_params=_params("parallel"),
        name="ffn_swiglu",
    )(x2, gain.reshape(1, d), wg, wu, wd)


def _expert_body(te_ref, first_ref, nv_ref, x_ref, wg_hbm, wu_hbm, wd_hbm, o_ref,
                 cg_s, cu_s, cd_s, sg_s, su_s, sd_s, sems, *, layer, tf):
    i = pl.program_id(0)
    nf = cg_s.shape[0]
    e = te_ref[i]

    def chunk_copies(f, slot):
        cols = pl.ds(f * tf, tf)
        return (pltpu.make_async_copy(wg_hbm.at[layer, e, :, cols], sg_s.at[slot], sems.at[0, slot]),
                pltpu.make_async_copy(wu_hbm.at[layer, e, :, cols], su_s.at[slot], sems.at[1, slot]),
                pltpu.make_async_copy(wd_hbm.at[layer, e, cols, :], sd_s.at[slot], sems.at[2, slot]))

    def compute(load_chunk):
        h = _load_token_tiles(x_ref).astype(BF16)
        acc = jnp.zeros(h.shape, F32)
        for f in range(nf):
            load_chunk(f)
            acc = acc + _swiglu_step(h, cg_s[f], cu_s[f], cd_s[f])
        _store_token_tiles(o_ref, acc)

    @pl.when(first_ref[i] == 1)
    def _():
        def load_chunk(f):
            if f + 1 < nf:
                for cp in chunk_copies(f + 1, (f + 1) % 2):
                    cp.start()
            for cp in chunk_copies(f, f % 2):
                cp.wait()
            cg_s[f] = sg_s[f % 2].astype(BF16)
            cu_s[f] = su_s[f % 2].astype(BF16)
            cd_s[f] = sd_s[f % 2].astype(BF16)

        for cp in chunk_copies(0, 0):
            cp.start()
        compute(load_chunk)

    @pl.when((first_ref[i] == 0) & (i < nv_ref[0]))
    def _():
        compute(lambda f: None)

    @pl.when(i >= nv_ref[0])
    def _():
        o_ref[...] = jnp.zeros_like(o_ref)


def _expert_ffn(xs, tile_expert, tile_first, n_live, wg, wu, wd, *, layer, tm, tf=512):
    p = xs.shape[0] // ROW_SUBLANES
    d, ff = wg.shape[-2:]
    nf = ff // tf
    rows = pl.BlockSpec((tm * ROW_SUBLANES, ROW_LANES), lambda i, te, first, nv: (i, 0))
    return pl.pallas_call(
        functools.partial(_expert_body, layer=layer, tf=tf),
        out_shape=jax.ShapeDtypeStruct(xs.shape, F32),
        grid_spec=pltpu.PrefetchScalarGridSpec(
            num_scalar_prefetch=3,
            grid=(p // tm,),
            in_specs=[
                rows,
                pl.BlockSpec(memory_space=pl.ANY),
                pl.BlockSpec(memory_space=pl.ANY),
                pl.BlockSpec(memory_space=pl.ANY),
            ],
            out_specs=rows,
            scratch_shapes=[
                pltpu.VMEM((nf, d, tf), BF16),
                pltpu.VMEM((nf, d, tf), BF16),
                pltpu.VMEM((nf, tf, d), BF16),
                pltpu.VMEM((2, d, tf), F32),
                pltpu.VMEM((2, d, tf), F32),
                pltpu.VMEM((2, tf, d), F32),
                pltpu.SemaphoreType.DMA((3, 2)),
            ],
        ),
        compiler_params=_params("arbitrary"),
        name="expert_swiglu",
    )(tile_expert, tile_first, n_live, xs, wg, wu, wd)


def _moba_body(q_ref, k_ref, vt_ref, km_ref, o_ref, sel_s, qh_s, sa_s, sb_s, m_s, out_s):
    i = pl.program_id(2)
    nblk = km_ref.shape[0]
    lane = _lane_ids()
    qn = q_ref[...]
    blk = lax.broadcasted_iota(I32, (nblk, MOBA_BLOCK), 0).astype(F32)
    first_blk = i.astype(F32)
    km = km_ref[...]
    krow = lax.broadcasted_iota(I32, (MOBA_BLOCK, MOBA_BLOCK), 0)
    qcol = lax.broadcasted_iota(I32, (MOBA_BLOCK, MOBA_BLOCK), 1)
    causal = krow <= qcol

    km_hi = km.astype(BF16)
    km_lo = (km - km_hi.astype(F32)).astype(BF16)
    acc_rows = [slice(h * MOBA_ACC_ROWS, (h + 1) * MOBA_ACC_ROWS) for h in range(HEADS_PER_TILE)]

    def keys(j):
        return k_ref[pl.ds(pl.multiple_of(j * MOBA_BLOCK, MOBA_BLOCK), MOBA_BLOCK), :]

    heads = range(HEADS_PER_TILE)
    qh = [jnp.where(lane // HEAD_DIM == h, qn, jnp.zeros_like(qn)) for h in heads]
    for h in heads:
        qh_s[h] = qh[h]
    gates = [lax.dot_general(km_hi, qh[h], _NT, preferred_element_type=F32)
             + lax.dot_general(km_lo, qh[h], _NT, preferred_element_type=F32) for h in heads]
    k_own = keys(i)
    own = [lax.dot_general(k_own, qh[h], _NT, preferred_element_type=F32) for h in heads]
    k_first = keys(0)
    for h in heads:
        sa_s[h] = lax.dot_general(k_first, qh[h], _NT, preferred_element_type=F32)

    for h in heads:
        gate = jnp.where(blk < first_blk, gates[h], NEG_INF)
        sel = jnp.zeros((nblk, MOBA_BLOCK), F32)
        for _ in range(MOBA_TOPK):
            top = jnp.max(gate, axis=0, keepdims=True)
            idx = jnp.min(jnp.where(gate == top, blk, float(nblk)), axis=0, keepdims=True)
            hit = blk == idx
            sel = jnp.where(hit & (top > NEG_INF), 1.0, sel)
            gate = jnp.where(hit, NEG_INF, gate)
        sel_s[h] = sel

    for h in heads:
        s = jnp.where(causal, own[h], NEG_INF)
        m0 = jnp.max(s, axis=0, keepdims=True)
        m_s[h] = m0
        out_s[acc_rows[h], :] = jnp.dot(vt_ref[i, acc_rows[h], :], jnp.exp2(s - m0).astype(BF16),
                                        preferred_element_type=F32)

    def stage(j_next, buf_next, j, buf):
        k_next = keys(jnp.minimum(j_next, nblk - 1))
        jc = jnp.minimum(j, nblk - 1)
        bar = jnp.where(j < i, 0.0, 2.0)
        for h in heads:
            buf_next[h] = lax.dot_general(k_next, qh_s[h], _NT, preferred_element_type=F32)
        for h in heads:
            bias = jnp.where(sel_s[h, pl.ds(jc, 1), :] > bar, 0.0, NEG_INF)
            s = buf[h]
            m = m_s[h]
            m_new = jnp.maximum(m, jnp.max(s, axis=0, keepdims=True) + bias)
            p = jnp.exp2(s + (bias - m_new)).astype(BF16)
            m_s[h] = m_new
            out_s[acc_rows[h], :] = (jnp.exp2(m - m_new) * out_s[acc_rows[h], :]
                                     + jnp.dot(vt_ref[jc, acc_rows[h], :], p, preferred_element_type=F32))

    def step(t, carry):
        stage(2 * t + 1, sb_s, 2 * t, sa_s)
        stage(2 * t + 2, sa_s, 2 * t + 1, sb_s)
        return carry

    lax.fori_loop(0, (i + 1) // 2, step, 0)

    outs = []
    for h in heads:
        acc = out_s[acc_rows[h], :]
        outs.append(acc[:HEAD_DIM, :] / acc[HEAD_DIM:HEAD_DIM + 1, :])
    o_ref[...] = jnp.concatenate(outs, axis=0).T.astype(o_ref.dtype)


def _moba(qp, kp, vtp, kmean):
    b, tiles, _, s, _ = qp.shape
    nblk = s // MOBA_BLOCK
    acc_w = HEADS_PER_TILE * MOBA_ACC_ROWS
    return pl.pallas_call(
        _moba_body,
        out_shape=jax.ShapeDtypeStruct((b, s, MOBA_WIDTH), BF16),
        grid=(b, tiles, nblk),
        in_specs=[
            pl.BlockSpec((None, None, None, MOBA_BLOCK, LANE_TILE), lambda bi, hg, i: (bi, hg, 0, i, 0)),
            pl.BlockSpec((None, None, None, s, LANE_TILE), lambda bi, hg, i: (bi, hg, 0, 0, 0)),
            pl.BlockSpec((None, None, None, nblk, acc_w, MOBA_BLOCK), lambda bi, hg, i: (bi, hg, 0, 0, 0, 0)),
            pl.BlockSpec((None, None, nblk, LANE_TILE), lambda bi, hg, i: (bi, hg, 0, 0)),
        ],
        out_specs=pl.BlockSpec((None, MOBA_BLOCK, LANE_TILE), lambda bi, hg, i: (bi, i, hg)),
        scratch_shapes=[
            pltpu.VMEM((HEADS_PER_TILE, nblk, MOBA_BLOCK), F32),
            pltpu.VMEM((HEADS_PER_TILE, MOBA_BLOCK, LANE_TILE), BF16),
            pltpu.VMEM((HEADS_PER_TILE, MOBA_BLOCK, MOBA_BLOCK), F32),
            pltpu.VMEM((HEADS_PER_TILE, MOBA_BLOCK, MOBA_BLOCK), F32),
            pltpu.VMEM((HEADS_PER_TILE, 1, MOBA_BLOCK), F32),
            pltpu.VMEM((HEADS_PER_TILE * MOBA_ACC_ROWS, MOBA_BLOCK), F32),
        ],
        compiler_params=_params("parallel", "parallel", "arbitrary"),
        name="moba_attn",
    )(qp, kp, vtp, kmean)


def _conv_body(bg_ref, cg_ref, xz_ref, w_ref, o_ref, *, ts):
    halo = 16
    w = w_ref[...]
    for c in range(o_ref.shape[0] // ts):
        lo = c * ts
        if c == 0:
            z = cg_ref[0:ts, :].astype(F32) * xz_ref[0:ts, :].astype(F32)
            zp = jnp.concatenate([jnp.zeros((halo, CONV_WIDTH), F32), z], axis=0)
        else:
            zp = cg_ref[lo - halo:lo + ts, :].astype(F32) * xz_ref[lo - halo:lo + ts, :].astype(F32)
        z1 = pltpu.roll(zp, 1, 0)[halo:]
        z2 = pltpu.roll(zp, 2, 0)[halo:]
        y = w[2:3, :] * zp[halo:] + w[1:2, :] * z1 + w[0:1, :] * z2
        o_ref[lo:lo + ts, :] = (bg_ref[lo:lo + ts, :].astype(F32) * y).astype(o_ref.dtype)


def _short_conv(p3, conv_w, *, col0, ts=512):
    b, s, _ = p3.shape
    taps = conv_w.shape[0]

    def col(c):
        return pl.BlockSpec((None, s, CONV_WIDTH), lambda bi: (bi, 0, col0 + c))

    return pl.pallas_call(
        functools.partial(_conv_body, ts=ts),
        out_shape=jax.ShapeDtypeStruct((b, s, CONV_WIDTH), BF16),
        grid=(b,),
        in_specs=[col(0), col(1), col(2), pl.BlockSpec((taps, CONV_WIDTH), lambda bi: (0, 0))],
        out_specs=pl.BlockSpec((None, s, CONV_WIDTH), lambda bi: (bi, 0, 0)),
        compiler_params=_params("parallel"),
        name="short_conv",
    )(p3, p3, p3, conv_w)


def _odd_out_body(x_ref, c_ref, d_ref, wc_ref, wd_ref, gain_ref, rw_ref, xo_ref, h_ref, route_ref):
    x = x_ref[...] + jnp.dot(c_ref[...], wc_ref[...], preferred_element_type=F32)
    x = x + jnp.dot(d_ref[...], wd_ref[...], preferred_element_type=F32)
    xo_ref[...] = x
    h = _row_norm(x, gain_ref[...])
    _store_token_tiles(h_ref, h)
    rw = rw_ref[...]
    h_hi = h.astype(BF16)
    h_lo = (h - h_hi.astype(F32)).astype(BF16)
    rw_hi = rw.astype(BF16)
    rw_lo = (rw - rw_hi.astype(F32)).astype(BF16)
    logits = (jnp.dot(h_hi, rw_hi, preferred_element_type=F32) + jnp.dot(h_lo, rw_hi, preferred_element_type=F32)
              + jnp.dot(h_hi, rw_lo, preferred_element_type=F32))
    lane = lax.broadcasted_iota(I32, logits.shape, 1)
    lane_f = lane.astype(F32)
    logits = jnp.where(lane < N_EXPERTS, logits, NEG_INF)
    v1 = jnp.max(logits, axis=-1, keepdims=True)
    i1 = jnp.min(jnp.where(logits == v1, lane_f, float(ROUTE_LANES)), axis=-1, keepdims=True)
    rest = jnp.where(lane_f == i1, NEG_INF, logits)
    v2 = jnp.max(rest, axis=-1, keepdims=True)
    i2 = jnp.min(jnp.where(rest == v2, lane_f, float(ROUTE_LANES)), axis=-1, keepdims=True)
    e = jnp.exp(v2 - v1)
    g1 = 1.0 / (1.0 + e)
    g2 = e / (1.0 + e)
    route = jnp.where(lane == 0, i1, 0.0)
    route = jnp.where(lane == 1, i2, route)
    route = jnp.where(lane == 2, g1, route)
    route_ref[...] = jnp.where(lane == 3, g2, route)


def _odd_out(x2, c2, d2, wc, wd, gain, router_pad, *, tm=512):
    t, d = x2.shape
    row = lambda w: pl.BlockSpec((tm, w), lambda i: (i, 0))
    full = lambda a: pl.BlockSpec(a.shape, lambda i: (0, 0))
    return pl.pallas_call(
        _odd_out_body,
        out_shape=[jax.ShapeDtypeStruct((t, d), F32), jax.ShapeDtypeStruct((t * ROW_SUBLANES, ROW_LANES), F32),
                   jax.ShapeDtypeStruct((t, ROUTE_LANES), F32)],
        grid=(t // tm,),
        in_specs=[row(d), row(MOBA_WIDTH), row(CONV_WIDTH), full(wc), full(wd),
                  pl.BlockSpec((1, d), lambda i: (0, 0)), full(router_pad)],
        out_specs=[row(d), pl.BlockSpec((tm * ROW_SUBLANES, ROW_LANES), lambda i: (i, 0)), row(ROUTE_LANES)],
        compiler_params=_params("parallel"),
        name="odd_out_proj_router",
    )(x2, c2, d2, wc, wd, gain.reshape(1, d), router_pad)


def _token_rows(token, count=1):
    return pl.ds(pl.multiple_of(token * ROW_SUBLANES, ROW_SUBLANES), count * ROW_SUBLANES)


def _row_copy(src_ref, src_row, dst_ref, dst_row, sem):
    return pltpu.make_async_copy(src_ref.at[_token_rows(src_row)], dst_ref.at[_token_rows(dst_row)], sem)


def _dispatch_body(pos_ref, pad_ref, h_ref, xs_ref, zeros_s, sem, pad_sem):
    td = h_ref.shape[0] // ROW_SUBLANES

    @pl.when(pl.program_id(0) == 0)
    def _():
        zeros_s[...] = jnp.zeros_like(zeros_s)
        tm = zeros_s.shape[0] // ROW_SUBLANES
        n_tiles = xs_ref.shape[0] // zeros_s.shape[0]
        n_live = pad_ref[N_EXPERTS]

        def fills():
            for e in range(N_EXPERTS):
                start = jnp.maximum(pad_ref[e], 0)
                yield pad_ref[e] >= 0, pltpu.make_async_copy(zeros_s, xs_ref.at[_token_rows(start, tm)], pad_sem)
            for c in range(N_EXPERTS):
                tile = n_tiles - 1 - c
                yield tile >= n_live, pltpu.make_async_copy(zeros_s, xs_ref.at[_token_rows(tile * tm, tm)], pad_sem)

        for has_tile, cp in fills():
            pl.when(has_tile)(cp.start)
        for has_tile, cp in fills():
            pl.when(has_tile)(cp.wait)

    def copies(r):
        return [_row_copy(h_ref, r, xs_ref, pos_ref[0, 0, k * td + r], sem) for k in range(2)]

    def start(r, c):
        for k, cp in enumerate(copies(r)):
            cp.start(priority=k)
        return c

    lax.fori_loop(0, td, start, 0, unroll=DMA_UNROLL)
    for _ in range(2):
        pltpu.make_async_copy(h_ref, xs_ref.at[_token_rows(0, td)], sem).wait()


def _dispatch(h3, pos3, pad_tiles, n_rows, *, td, tm):
    t = h3.shape[0] // ROW_SUBLANES
    return pl.pallas_call(
        _dispatch_body,
        out_shape=jax.ShapeDtypeStruct((n_rows * ROW_SUBLANES, ROW_LANES), F32),
        grid=(t // td,),
        in_specs=[
            pl.BlockSpec((1, 1, 2 * td), lambda i: (i, 0, 0), memory_space=pltpu.SMEM),
            pl.BlockSpec(memory_space=pltpu.SMEM),
            pl.BlockSpec((td * ROW_SUBLANES, ROW_LANES), lambda i: (i, 0)),
        ],
        out_specs=pl.BlockSpec(memory_space=pl.ANY),
        scratch_shapes=[pltpu.VMEM((tm * ROW_SUBLANES, ROW_LANES), F32), pltpu.SemaphoreType.DMA(()),
                        pltpu.SemaphoreType.DMA(())],
        compiler_params=_params("arbitrary"),
        name="expert_dispatch",
    )(pos3, pad_tiles, h3)


def _combine_body(pos_ref, x_ref, route_ref, y_ref, o_ref, buf, sems):
    i = pl.program_id(0)
    n = pl.num_programs(0)
    td = x_ref.shape[0]
    t = n * td

    def issue(tile, slot):
        def start(r, c):
            for k in range(2):
                _row_copy(y_ref, pos_ref[k * t + tile * td + r], buf.at[slot, k], r,
                          sems.at[slot]).start(priority=k)
            return c

        lax.fori_loop(0, td, start, 0, unroll=DMA_UNROLL)

    @pl.when(i == 0)
    def _():
        issue(0, 0)

    slot = i % 2

    @pl.when(i + 1 < n)
    def _():
        issue(i + 1, 1 - slot)

    for k in range(2):
        pltpu.make_async_copy(y_ref.at[_token_rows(0, td)], buf.at[slot, k], sems.at[slot]).wait()
    route = route_ref[...]
    o_ref[...] = (x_ref[...] + route[:, 2:3] * _load_token_tiles(buf.at[slot, 0])
                  + route[:, 3:4] * _load_token_tiles(buf.at[slot, 1]))


def _combine(x2, route, ys, pos, *, td):
    t, d = x2.shape
    return pl.pallas_call(
        _combine_body,
        out_shape=jax.ShapeDtypeStruct((t, d), F32),
        grid_spec=pltpu.PrefetchScalarGridSpec(
            num_scalar_prefetch=1,
            grid=(t // td,),
            in_specs=[
                pl.BlockSpec((td, d), lambda i, pos: (i, 0)),
                pl.BlockSpec((td, ROUTE_LANES), lambda i, pos: (i, 0)),
                pl.BlockSpec(memory_space=pl.ANY),
            ],
            out_specs=pl.BlockSpec((td, d), lambda i, pos: (i, 0)),
            scratch_shapes=[pltpu.VMEM((2, 2, td * ROW_SUBLANES, ROW_LANES), F32), pltpu.SemaphoreType.DMA((2,))],
        ),
        compiler_params=_params("arbitrary"),
        name="expert_combine",
    )(pos, x2, route, ys)


def _route_plan(route, *, tm, td):
    t = route.shape[0]
    experts = jnp.concatenate([route[:, 0], route[:, 1]]).astype(I32)
    onehot = (experts[:, None] == jnp.arange(N_EXPERTS, dtype=I32)[None, :]).astype(I32)
    running = jnp.cumsum(onehot, axis=0)
    counts = running[-1]
    padded = ((counts + tm - 1) // tm) * tm
    ends = jnp.cumsum(padded)
    starts = ends - padded
    pos = jnp.sum(onehot * (starts[None, :] + running - 1), axis=1)
    n_tiles = (2 * t) // tm + N_EXPERTS
    tile_lo = jnp.arange(n_tiles, dtype=I32) * tm
    tile_expert = jnp.sum((tile_lo[:, None] >= ends[None, :]).astype(I32), axis=1)
    n_live = (ends[-1] // tm).astype(I32)
    last = tile_expert[jnp.maximum(n_live - 1, 0)]
    live = jnp.arange(n_tiles) < n_live
    tile_expert = jnp.where(live, tile_expert, last)
    prev = jnp.concatenate([jnp.full((1,), -1, I32), tile_expert[:-1]])
    tile_first = (live & (tile_expert != prev)).astype(I32)
    pos3 = pos.reshape(2, t // td, td).transpose(1, 0, 2).reshape(t // td, 1, 2 * td)
    pad_tiles = jnp.concatenate([jnp.where(padded > 0, ends - tm, -1), n_live.reshape(1)]).astype(I32)
    return pos, pos3, pad_tiles, tile_expert, tile_first, n_live.reshape(1), n_tiles * tm


def _rope_tables(seq):
    inv = 1.0 / (ROPE_THETA ** (jnp.arange(0, HEAD_DIM, 2, dtype=F32) / HEAD_DIM))
    ang = jnp.arange(seq, dtype=F32)[:, None] * inv[None, :]
    ang = jnp.concatenate([ang, ang], axis=-1)
    tile = lambda a: jnp.tile(a, (1, HEADS_PER_TILE))
    return tile(jnp.cos(ang)), tile(jnp.sin(ang))


def _even_layer(x2, b, s, cos_t, sin_t, norm_mix, norm_ffn, w_in, v_gain, ws, bs, gq, gk, w_out, wg, wu, wd):
    p = _norm_proj(x2, norm_mix, w_in.astype(BF16))
    p3 = p.reshape(b, s, p.shape[1])
    a = _gmlp(p3, v_gain, ws, bs)
    q0 = 2 * GMLP_WIDTH // LANE_TILE
    n_groups = len(DIL_CONFIGS)
    accs = []
    for group, (window, dil) in enumerate(DIL_CONFIGS):
        assert window // dil == BAND
        qg, kg, vtg = _qkv_prep(p3, cos_t, sin_t, gq, gk, q_col=q0 + group,
                                k_col=q0 + n_groups + group, v_col=q0 + 2 * n_groups + group, n_ht=1, dil=dil,
                                tt=max(4, dil) * BAND, vblock=BAND, head_rows=BAND_ACC_ROWS, with_kmean=False)
        accs.append(_band_attn(qg, kg, vtg))
    w_out = w_out.astype(BF16)
    x3 = _even_out(x2.reshape(b, s, -1), a, accs, w_out[:GMLP_WIDTH], w_out[GMLP_WIDTH:])
    return _ffn(x3.reshape(b * s, -1), norm_ffn, wg.astype(BF16), wu.astype(BF16), wd.astype(BF16))


def _odd_layer(x2, b, s, cos_t, sin_t, norm_mix, norm_ffn, w_in, gq, gk, conv_w, w_out, router_w, wg, wu, wd,
               *, layer, tm=512, td=256):
    p = _norm_proj(x2, norm_mix, w_in.astype(BF16))
    p3 = p.reshape(b, s, p.shape[1])
    tiles = MOBA_WIDTH // LANE_TILE
    qp, kp, vtp, kmean = _qkv_prep(p3, cos_t, sin_t, gq, gk, q_col=0, k_col=tiles,
                                   v_col=2 * tiles, n_ht=tiles, dil=1, tt=8 * MOBA_BLOCK, vblock=MOBA_BLOCK,
                                   head_rows=MOBA_ACC_ROWS, with_kmean=True)
    c = _moba(qp, kp, vtp, kmean)
    dconv = _short_conv(p3, conv_w, col0=3 * MOBA_WIDTH // CONV_WIDTH)
    w_out = w_out.astype(BF16)
    router_pad = jnp.pad(router_w, ((0, 0), (0, ROUTE_LANES - N_EXPERTS)))
    x2, h2, route = _odd_out(x2, c.reshape(b * s, MOBA_WIDTH), dconv.reshape(b * s, CONV_WIDTH),
                             w_out[:MOBA_WIDTH], w_out[MOBA_WIDTH:], norm_ffn, router_pad)
    pos, pos3, pad_tiles, tile_expert, tile_first, n_live, n_rows = _route_plan(route, tm=tm, td=td)
    xs = _dispatch(h2, pos3, pad_tiles, n_rows, td=td, tm=tm)
    ys = _expert_ffn(xs, tile_expert, tile_first, n_live, wg, wu, wd, layer=layer, tm=tm)
    return _combine(x2, route, ys, pos, td=td)


def kernel(x, norm_mix, norm_ffn, w_in_ab, gmlp_v_gain, gmlp_ws, gmlp_bs, dil_q_gain, dil_k_gain, w_out_ab,
           ffn_w_gate, ffn_w_up, ffn_w_down, w_in_cd, moba_q_gain, moba_k_gain, conv_w, w_out_cd, router_w,
           moe_w_gate, moe_w_up, moe_w_down):
    b, s, d = x.shape
    depth = norm_mix.shape[0]
    cos_t, sin_t = _rope_tables(s)
    x2 = x.reshape(b * s, d)
    for layer in range(depth):
        i = layer // 2
        if layer % 2 == 0:
            x2 = _even_layer(x2, b, s, cos_t, sin_t, norm_mix[layer], norm_ffn[layer], w_in_ab[i], gmlp_v_gain[i],
                             gmlp_ws[i], gmlp_bs[i], dil_q_gain[i], dil_k_gain[i], w_out_ab[i], ffn_w_gate[i],
                             ffn_w_up[i], ffn_w_down[i])
        else:
            x2 = _odd_layer(x2, b, s, cos_t, sin_t, norm_mix[layer], norm_ffn[layer], w_in_cd[i], moba_q_gain[i],
                            moba_k_gain[i], conv_w[i], w_out_cd[i], router_w[i], moe_w_gate, moe_w_up,
                            moe_w_down, layer=i)
    return x2.reshape(b, s, d)
```

```python
import functools

import jax
import jax.numpy as jnp
from jax import lax
from jax.experimental import pallas as pl
from jax.experimental.pallas import tpu as pltpu

F32 = jnp.float32
BF16 = jnp.bfloat16
I32 = jnp.int32

D_MODEL = 1024
HEAD_DIM = 64
ROPE_THETA = 10000.0
NORM_EPS = 1e-6
LANE_TILE = 256
HEADS_PER_TILE = LANE_TILE // HEAD_DIM
GMLP_WIDTH = 256
GMLP_CHUNK = 128
DIL_CONFIGS = ((128, 1), (512, 4), (2048, 16))
DIL_WIDTH = 768
BAND = 128
BAND_ACC_ROWS = 128
BAND_L_ROW = HEAD_DIM
BAND_M_ROW = HEAD_DIM + 1
BAND_LOOKAHEAD = 5
STRIDE_LANES = 128
MOBA_WIDTH = 768
MOBA_BLOCK = 256
MOBA_TOPK = 3
MOBA_ACC_ROWS = HEAD_DIM + 16
LOG2_E = 1.4426950408889634
MOBA_M_INIT = -1e30
CONV_WIDTH = 256
N_EXPERTS = 8
ROUTE_LANES = 128
NEG_INF = float("-inf")
DMA_UNROLL = 8
ROW_LANES = 128
ROW_SUBLANES = 8

_NT = (((1,), (1,)), ((), ()))


def _params(*sem):
    return pltpu.CompilerParams(dimension_semantics=tuple(sem), vmem_limit_bytes=56 * 1024 * 1024)


def _lane_ids():
    return lax.broadcasted_iota(I32, (1, LANE_TILE), 1)


def _head_block_diag():
    r = lax.broadcasted_iota(I32, (LANE_TILE, LANE_TILE), 0) // HEAD_DIM
    c = lax.broadcasted_iota(I32, (LANE_TILE, LANE_TILE), 1) // HEAD_DIM
    return jnp.where(r == c, 1.0, 0.0).astype(BF16)


def _head_mean_sq(x, bd):
    s = x * x
    hi = s.astype(BF16)
    lo = (s - hi.astype(F32)).astype(BF16)
    tot = jnp.dot(hi, bd, preferred_element_type=F32) + jnp.dot(lo, bd, preferred_element_type=F32)
    return tot * (1.0 / HEAD_DIM)


def _head_norm(x, gain, bd):
    return x * lax.rsqrt(_head_mean_sq(x, bd) + NORM_EPS) * gain


def _row_norm(x, gain):
    return x * lax.rsqrt(jnp.mean(x * x, axis=-1, keepdims=True) + NORM_EPS) * gain


def _load_token_tiles(ref):
    n = ref.shape[0] // ROW_SUBLANES
    return jnp.concatenate([ref[pl.ds(c, n, stride=ROW_SUBLANES), :] for c in range(ROW_SUBLANES)], axis=1)


def _store_token_tiles(ref, val):
    n = ref.shape[0] // ROW_SUBLANES
    for c in range(ROW_SUBLANES):
        ref[pl.ds(c, n, stride=ROW_SUBLANES), :] = val[:, c * ROW_LANES:(c + 1) * ROW_LANES]


def _norm_proj_body(x_ref, g_ref, w_ref, o_ref, *, tn):
    hb = _row_norm(x_ref[...], g_ref[...]).astype(BF16)
    for c in range(o_ref.shape[-1] // tn):
        sl = slice(c * tn, (c + 1) * tn)
        o_ref[:, sl] = jnp.dot(hb, w_ref[:, sl], preferred_element_type=F32).astype(o_ref.dtype)


def _norm_proj(x2, gain, w_bf16, *, tm=512, tn=256):
    t, d = x2.shape
    n = w_bf16.shape[1]
    return pl.pallas_call(
        functools.partial(_norm_proj_body, tn=tn),
        out_shape=jax.ShapeDtypeStruct((t, n), BF16),
        grid=(t // tm,),
        in_specs=[
            pl.BlockSpec((tm, d), lambda i: (i, 0)),
            pl.BlockSpec((1, d), lambda i: (0, 0)),
            pl.BlockSpec((d, n), lambda i: (0, 0)),
        ],
        out_specs=pl.BlockSpec((tm, n), lambda i: (i, 0)),
        compiler_params=_params("parallel"),
        name="norm_proj",
    )(x2, gain.reshape(1, d), w_bf16)


def _gmlp_body(u_ref, v_ref, gain_ref, ws_ref, bias_ref, o_ref):
    lane = _lane_ids()
    bd = _head_block_diag()
    r = lax.broadcasted_iota(I32, (GMLP_CHUNK, GMLP_CHUNK), 0)
    c = lax.broadcasted_iota(I32, (GMLP_CHUNK, GMLP_CHUNK), 1)
    w_tril = [jnp.where(r >= c, ws_ref[g], 0.0).astype(BF16) for g in range(HEADS_PER_TILE)]
    for ch in range(o_ref.shape[0] // GMLP_CHUNK):
        sl = slice(ch * GMLP_CHUNK, (ch + 1) * GMLP_CHUNK)
        u = jax.nn.gelu(u_ref[sl, :].astype(F32))
        v = jax.nn.gelu(v_ref[sl, :].astype(F32))
        vn = _head_norm(v, gain_ref[...], bd).astype(BF16)
        y = bias_ref[...]
        for g in range(HEADS_PER_TILE):
            yg = jnp.dot(w_tril[g], vn, preferred_element_type=F32)
            y = y + jnp.where(lane // HEAD_DIM == g, yg, 0.0)
        o_ref[sl, :] = (u * y).astype(o_ref.dtype)


def _gmlp(p3, v_gain, ws, bs, *, tc=512):
    b, s, _ = p3.shape
    groups, chunk = bs.shape
    bias = jnp.repeat(bs.T, HEAD_DIM, axis=1)
    return pl.pallas_call(
        _gmlp_body,
        out_shape=jax.ShapeDtypeStruct((b, s, GMLP_WIDTH), BF16),
        grid=(b, s // tc),
        in_specs=[
            pl.BlockSpec((None, tc, LANE_TILE), lambda bi, i: (bi, i, 0)),
            pl.BlockSpec((None, tc, LANE_TILE), lambda bi, i: (bi, i, 1)),
            pl.BlockSpec((1, GMLP_WIDTH), lambda bi, i: (0, 0)),
            pl.BlockSpec((groups, chunk, chunk), lambda bi, i: (0, 0, 0)),
            pl.BlockSpec((chunk, GMLP_WIDTH), lambda bi, i: (0, 0)),
        ],
        out_specs=pl.BlockSpec((None, tc, GMLP_WIDTH), lambda bi, i: (bi, i, 0)),
        compiler_params=_params("parallel", "parallel"),
        name="gmlp_gate",
    )(p3, p3, v_gain.reshape(1, GMLP_WIDTH), ws, bias)


def _residues(val, scr, dil):
    if dil == 1:
        return [val]
    rows = val.shape[0] // dil
    slabs = LANE_TILE // STRIDE_LANES
    for c in range(slabs):
        scr[c] = val[:, c * STRIDE_LANES:(c + 1) * STRIDE_LANES]
    return [jnp.concatenate([scr[c, pl.ds(r, rows, stride=dil), :] for c in range(slabs)], axis=1)
            for r in range(dil)]


def _norm_rope(x, tab_a, tab_b, bd, lane):
    sq = (x * x).astype(BF16)
    xn = x * lax.rsqrt(jnp.dot(sq, bd, preferred_element_type=F32) * (1.0 / HEAD_DIM) + NORM_EPS)
    half = HEAD_DIM // 2
    partner = jnp.where((lane % HEAD_DIM) < half, pltpu.roll(xn, LANE_TILE - half, 1), pltpu.roll(xn, half, 1))
    return xn * tab_a + partner * tab_b


def _qkv_prep_body(q_ref, k_ref, v_ref, qa_ref, qb_ref, ka_ref, kb_ref, *rest, dil, with_kmean):
    n_out = 4 if with_kmean else 3
    qo_ref, ko_ref, vt_ref = rest[:3]
    scr = list(rest[n_out:]) + [None] * 3
    lane = _lane_ids()
    bd = _head_block_diag()
    qn = _norm_rope(q_ref[...].astype(F32), qa_ref[...], qb_ref[...], bd, lane)
    kn = _norm_rope(k_ref[...].astype(F32), ka_ref[...], kb_ref[...], bd, lane)
    if with_kmean:
        km_ref = rest[3]
        km_ref[...] = jnp.mean(kn.reshape(km_ref.shape[0], MOBA_BLOCK, LANE_TILE), axis=1)
    for r, piece in enumerate(_residues(qn, scr[0], dil)):
        qo_ref[r] = piece.astype(BF16)
    for r, piece in enumerate(_residues(kn, scr[1], dil)):
        ko_ref[r] = piece.astype(BF16)
    n_blocks, acc_w, vblock = vt_ref.shape[1:]
    head_rows = acc_w // HEADS_PER_TILE
    ones = jnp.ones((head_rows - HEAD_DIM, vblock), BF16)
    for r, piece in enumerate(_residues(v_ref[...].astype(F32), scr[2], dil)):
        for c in range(n_blocks):
            vt = piece[c * vblock:(c + 1) * vblock, :].T.astype(BF16)
            for h in range(HEADS_PER_TILE):
                vt_ref[r, c, h * head_rows:h * head_rows + HEAD_DIM, :] = vt[h * HEAD_DIM:(h + 1) * HEAD_DIM, :]
                vt_ref[r, c, h * head_rows + HEAD_DIM:(h + 1) * head_rows, :] = ones


def _rope_gain_tables(cos_t, sin_t, gain, scale):
    half = HEAD_DIM // 2
    swapped = jnp.concatenate([gain[half:], gain[:half]])
    sign = jnp.concatenate([-jnp.ones((half,), F32), jnp.ones((half,), F32)])
    tile = lambda v: jnp.tile(v, HEADS_PER_TILE)[None, :]
    return cos_t * (tile(gain) * scale), sin_t * (tile(sign * swapped) * scale)


def _qkv_prep(p3, cos_t, sin_t, gq, gk, *, q_col, k_col, v_col, n_ht, dil, tt, vblock, head_rows, with_kmean):
    b, s, _ = p3.shape
    length = s // dil
    tr = tt // dil
    acc_w = HEADS_PER_TILE * head_rows

    def col(c0):
        return pl.BlockSpec((None, tt, LANE_TILE), lambda i, bi, ht: (bi, i, c0 + ht))

    tab = pl.BlockSpec((tt, LANE_TILE), lambda i, bi, ht: (i, 0))
    tables = _rope_gain_tables(cos_t, sin_t, gq, HEAD_DIM ** -0.5 * LOG2_E) + _rope_gain_tables(cos_t, sin_t, gk, 1.0)
    qk_shape = jax.ShapeDtypeStruct((b, n_ht, dil, length, LANE_TILE), BF16)
    qk_spec = pl.BlockSpec((None, None, dil, tr, LANE_TILE), lambda i, bi, ht: (bi, ht, 0, i, 0))
    out_shape = [qk_shape, qk_shape,
                 jax.ShapeDtypeStruct((b, n_ht, dil, length // vblock, acc_w, vblock), BF16)]
    out_specs = [qk_spec, qk_spec,
                 pl.BlockSpec((None, None, dil, tr // vblock, acc_w, vblock), lambda i, bi, ht: (bi, ht, 0, i, 0, 0))]
    if with_kmean:
        out_shape.append(jax.ShapeDtypeStruct((b, n_ht, s // MOBA_BLOCK, LANE_TILE), F32))
        out_specs.append(pl.BlockSpec((None, None, tt // MOBA_BLOCK, LANE_TILE), lambda i, bi, ht: (bi, ht, i, 0)))
    scratch = [pltpu.VMEM((LANE_TILE // STRIDE_LANES, tt, STRIDE_LANES), F32)] * (3 if dil > 1 else 0)
    return pl.pallas_call(
        functools.partial(_qkv_prep_body, dil=dil, with_kmean=with_kmean),
        out_shape=out_shape,
        grid=(s // tt, b, n_ht),
        in_specs=[col(q_col), col(k_col), col(v_col), tab, tab, tab, tab],
        out_specs=out_specs,
        scratch_shapes=scratch,
        compiler_params=_params("parallel", "parallel", "parallel"),
        name=f"qkv_prep_d{dil}",
    )(p3, p3, p3, *tables)


def _band_attn_body(q_ref, k_ref, kp_ref, vt_ref, vtp_ref, o_ref):
    i = pl.program_id(2)
    lb = q_ref.shape[0]
    lane = _lane_ids()
    kk = lax.broadcasted_iota(I32, (2 * BAND, BAND), 0)
    qq = lax.broadcasted_iota(I32, (2 * BAND, BAND), 1) + BAND
    in_band = (qq - kk >= 0) & (qq - kk <= BAND)
    first_ok = in_band & ((kk >= BAND) | (i > 0))
    acc_row = lax.broadcasted_iota(I32, (BAND_ACC_ROWS, BAND), 0)
    k_all = jnp.concatenate([kp_ref[...], k_ref[...]], axis=0)
    chains = [(j, h) for j in range(lb // BAND) for h in range(HEADS_PER_TILE)]

    def scores(j, h):
        qj = q_ref[j * BAND:(j + 1) * BAND, :]
        qh = jnp.where(lane // HEAD_DIM == h, qj, jnp.zeros_like(qj))
        return lax.dot_general(k_all[j * BAND:(j + 2) * BAND], qh, _NT, preferred_element_type=F32)

    staged = [scores(*c) for c in chains[:BAND_LOOKAHEAD]]
    tiles = []
    for n, (j, h) in enumerate(chains):
        if n + BAND_LOOKAHEAD < len(chains):
            staged.append(scores(*chains[n + BAND_LOOKAHEAD]))
        rows = slice(h * BAND_ACC_ROWS, (h + 1) * BAND_ACC_ROWS)
        s = jnp.where(first_ok if j == 0 else in_band, staged[n], NEG_INF)
        m = jnp.max(s, axis=0, keepdims=True)
        p = jnp.exp2(s - m).astype(BF16)
        v_prev = vtp_ref[0, rows, :] if j == 0 else vt_ref[j - 1, rows, :]
        vw = jnp.concatenate([v_prev, vt_ref[j, rows, :]], axis=1)
        acc = jnp.dot(vw, p, preferred_element_type=F32)
        tiles.append(jnp.where(acc_row == BAND_M_ROW, m, acc))
        if h == HEADS_PER_TILE - 1:
            o_ref[j * BAND:(j + 1) * BAND, :] = jnp.concatenate(tiles, axis=0).T
            tiles = []


def _band_attn(qg, kg, vtg):
    b, _, dil, length, _ = qg.shape
    lb = min(length, 512)
    sub = lb // BAND
    acc_w = HEADS_PER_TILE * BAND_ACC_ROWS
    before = lambda i: jnp.maximum(i * sub - 1, 0)
    cur = pl.BlockSpec((None, None, None, lb, LANE_TILE), lambda bi, r, i: (bi, 0, r, i, 0))
    prev = pl.BlockSpec((None, None, None, BAND, LANE_TILE), lambda bi, r, i: (bi, 0, r, before(i), 0))
    vt_cur = pl.BlockSpec((None, None, None, sub, acc_w, BAND), lambda bi, r, i: (bi, 0, r, i, 0, 0))
    vt_prev = pl.BlockSpec((None, None, None, 1, acc_w, BAND), lambda bi, r, i: (bi, 0, r, before(i), 0, 0))
    return pl.pallas_call(
        _band_attn_body,
        out_shape=jax.ShapeDtypeStruct((b, dil, length, acc_w), F32),
        grid=(b, dil, length // lb),
        in_specs=[cur, cur, prev, vt_cur, vt_prev],
        out_specs=pl.BlockSpec((None, None, lb, acc_w), lambda bi, r, i: (bi, r, i, 0)),
        compiler_params=_params("parallel", "parallel", "arbitrary"),
        name=f"band_attn_d{dil}",
    )(qg, kg, kg, vtg, vtg)


def _even_out_ffn_body(x_ref, a_ref, o0_ref, o1_ref, o2_ref, wa_ref, wb_ref, g_ref, wg_ref, wu_ref, wd_ref,
                       out_ref, *scratch, tf):
    tm = x_ref.shape[0]

    def natural_order(o_ref, scr):
        dil = o_ref.shape[0]
        if dil == 1:
            return [o_ref[0, :, h * BAND_ACC_ROWS:(h + 1) * BAND_ACC_ROWS] for h in range(HEADS_PER_TILE)]
        for r in range(dil):
            for h in range(HEADS_PER_TILE):
                scr[h, pl.ds(r, tm // dil, stride=dil), :] = o_ref[r, :, h * BAND_ACC_ROWS:(h + 1) * BAND_ACC_ROWS]
        return [scr[h] for h in range(HEADS_PER_TILE)]

    o_refs = (o0_ref, o1_ref, o2_ref)
    spare = list(scratch)
    groups = [natural_order(o, spare.pop(0) if o.shape[0] > 1 else None) for o in o_refs]
    acc = x_ref[...] + jnp.dot(a_ref[...], wa_ref[...], preferred_element_type=F32)
    for h in range(HEADS_PER_TILE):
        tiles = [g[h] for g in groups]
        tops = [t[:, BAND_M_ROW:BAND_M_ROW + 1] for t in tiles]
        top = jnp.maximum(jnp.maximum(tops[0], tops[1]), tops[2])
        ws = [jnp.exp2(m - top) for m in tops]
        num = sum(w * t[:, :HEAD_DIM] for w, t in zip(ws, tiles))
        den = sum(w * t[:, BAND_L_ROW:BAND_L_ROW + 1] for w, t in zip(ws, tiles))
        merged = (num / den).astype(BF16)
        acc = acc + jnp.dot(merged, wb_ref[h * HEAD_DIM:(h + 1) * HEAD_DIM, :], preferred_element_type=F32)
    out_ref[...] = acc
    out_ref[...] = _ffn_value(out_ref[...], g_ref, wg_ref, wu_ref, wd_ref, tf)


def _even_out_ffn(x3, a3, accs, wa, wb, gain, wg, wu, wd, *, tm=512, tf=256):
    b, s, d = x3.shape
    acc_w = HEADS_PER_TILE * BAND_ACC_ROWS
    nt = s // tm
    full = lambda a: pl.BlockSpec(a.shape, lambda n: (0, 0))
    resident = lambda a: pl.BlockSpec(a.shape, lambda n: (0, 0), pipeline_mode=pl.Buffered(1))

    def acc_spec(a):
        dil = a.shape[1]
        return pl.BlockSpec((None, dil, tm // dil, acc_w), lambda n: (n // nt, 0, n % nt, 0))

    scratch = [pltpu.VMEM((HEADS_PER_TILE, tm, BAND_ACC_ROWS), F32) for a in accs if a.shape[1] > 1]
    return pl.pallas_call(
        functools.partial(_even_out_ffn_body, tf=tf),
        out_shape=jax.ShapeDtypeStruct((b, s, d), F32),
        grid=(b * nt,),
        in_specs=[pl.BlockSpec((None, tm, d), lambda n: (n // nt, n % nt, 0)),
                  pl.BlockSpec((None, tm, GMLP_WIDTH), lambda n: (n // nt, n % nt, 0))]
        + [acc_spec(a) for a in accs]
        + [full(wa), full(wb), pl.BlockSpec((1, d), lambda n: (0, 0)),
           resident(wg), resident(wu), resident(wd)],
        out_specs=pl.BlockSpec((None, tm, d), lambda n: (n // nt, n % nt, 0)),
        scratch_shapes=scratch,
        compiler_params=_params("parallel"),
        name="even_out_ffn",
    )(x3, a3, *accs, wa, wb, gain.reshape(1, d), wg, wu, wd)


def _swiglu_step(h_bf16, wg, wu, wd):
    g = jnp.dot(h_bf16, wg, preferred_element_type=F32)
    u = jnp.dot(h_bf16, wu, preferred_element_type=F32)
    a = (g * jax.nn.sigmoid(g) * u).astype(BF16)
    return jnp.dot(a, wd, preferred_element_type=F32)


def _ffn_value(x, g_ref, wg_ref, wu_ref, wd_ref, tf):
    h = _row_norm(x, g_ref[...]).astype(BF16)
    acc = x
    for c in range(wg_ref.shape[1] // tf):
        sl = slice(c * tf, (c + 1) * tf)
        acc = acc + _swiglu_step(h, wg_ref[:, sl], wu_ref[:, sl], wd_ref[sl, :])
    return acc


def _expert_body(te_ref, first_ref, nv_ref, x_ref, wg_hbm, wu_hbm, wd_hbm, o_ref,
                 cg_s, cu_s, cd_s, sg_s, su_s, sd_s, sems, *, layer, tf):
    i = pl.program_id(0)
    nf = cg_s.shape[0]
    e = te_ref[i]

    def chunk_copies(f, slot):
        cols = pl.ds(f * tf, tf)
        return (pltpu.make_async_copy(wg_hbm.at[layer, e, :, cols], sg_s.at[slot], sems.at[0, slot]),
                pltpu.make_async_copy(wu_hbm.at[layer, e, :, cols], su_s.at[slot], sems.at[1, slot]),
                pltpu.make_async_copy(wd_hbm.at[layer, e, cols, :], sd_s.at[slot], sems.at[2, slot]))

    def compute(load_chunk):
        h = _load_token_tiles(x_ref).astype(BF16)
        acc = jnp.zeros(h.shape, F32)
        for f in range(nf):
            load_chunk(f)
            acc = acc + _swiglu_step(h, cg_s[f], cu_s[f], cd_s[f])
        _store_token_tiles(o_ref, acc)

    @pl.when(first_ref[i] == 1)
    def _():
        def load_chunk(f):
            if f + 1 < nf:
                for cp in chunk_copies(f + 1, (f + 1) % 2):
                    cp.start()
            for cp in chunk_copies(f, f % 2):
                cp.wait()
            cg_s[f] = sg_s[f % 2].astype(BF16)
            cu_s[f] = su_s[f % 2].astype(BF16)
            cd_s[f] = sd_s[f % 2].astype(BF16)

        for cp in chunk_copies(0, 0):
            cp.start()
        compute(load_chunk)

    @pl.when((first_ref[i] == 0) & (i < nv_ref[0]))
    def _():
        compute(lambda f: None)

    @pl.when(i >= nv_ref[0])
    def _():
        o_ref[...] = jnp.zeros_like(o_ref)


def _expert_ffn(xs, tile_expert, tile_first, n_live, wg, wu, wd, *, layer, tm, tf=512):
    p = xs.shape[0] // ROW_SUBLANES
    d, ff = wg.shape[-2:]
    nf = ff // tf
    rows = pl.BlockSpec((tm * ROW_SUBLANES, ROW_LANES), lambda i, te, first, nv: (i, 0))
    return pl.pallas_call(
        functools.partial(_expert_body, layer=layer, tf=tf),
        out_shape=jax.ShapeDtypeStruct(xs.shape, F32),
        grid_spec=pltpu.PrefetchScalarGridSpec(
            num_scalar_prefetch=3,
            grid=(p // tm,),
            in_specs=[
                rows,
                pl.BlockSpec(memory_space=pl.ANY),
                pl.BlockSpec(memory_space=pl.ANY),
                pl.BlockSpec(memory_space=pl.ANY),
            ],
            out_specs=rows,
            scratch_shapes=[
                pltpu.VMEM((nf, d, tf), BF16),
                pltpu.VMEM((nf, d, tf), BF16),
                pltpu.VMEM((nf, tf, d), BF16),
                pltpu.VMEM((2, d, tf), F32),
                pltpu.VMEM((2, d, tf), F32),
                pltpu.VMEM((2, tf, d), F32),
                pltpu.SemaphoreType.DMA((3, 2)),
            ],
        ),
        compiler_params=_params("arbitrary"),
        name="expert_swiglu",
    )(tile_expert, tile_first, n_live, xs, wg, wu, wd)


def _moba_body(q_ref, k_ref, vt_ref, km_ref, o_ref, sel_s, qh_s, sa_s, sb_s, m_s, out_s):
    i = pl.program_id(2)
    nblk = km_ref.shape[0]
    lane = _lane_ids()
    qn = q_ref[...]
    blk = lax.broadcasted_iota(I32, (nblk, MOBA_BLOCK), 0).astype(F32)
    first_blk = i.astype(F32)
    km = km_ref[...]
    krow = lax.broadcasted_iota(I32, (MOBA_BLOCK, MOBA_BLOCK), 0)
    qcol = lax.broadcasted_iota(I32, (MOBA_BLOCK, MOBA_BLOCK), 1)
    causal = krow <= qcol

    km_hi = km.astype(BF16)
    km_lo = (km - km_hi.astype(F32)).astype(BF16)
    acc_rows = [slice(h * MOBA_ACC_ROWS, (h + 1) * MOBA_ACC_ROWS) for h in range(HEADS_PER_TILE)]

    def keys(j):
        return k_ref[pl.ds(pl.multiple_of(j * MOBA_BLOCK, MOBA_BLOCK), MOBA_BLOCK), :]

    heads = range(HEADS_PER_TILE)
    qh = [jnp.where(lane // HEAD_DIM == h, qn, jnp.zeros_like(qn)) for h in heads]
    for h in heads:
        qh_s[h] = qh[h]
    gates = [lax.dot_general(km_hi, qh[h], _NT, preferred_element_type=F32)
             + lax.dot_general(km_lo, qh[h], _NT, preferred_element_type=F32) for h in heads]
    k_own = keys(i)
    own = [lax.dot_general(k_own, qh[h], _NT, preferred_element_type=F32) for h in heads]
    k_first = keys(0)
    for h in heads:
        sa_s[h] = lax.dot_general(k_first, qh[h], _NT, preferred_element_type=F32)

    for h in heads:
        gate = jnp.where(blk < first_blk, gates[h], NEG_INF)
        sel = jnp.zeros((nblk, MOBA_BLOCK), F32)
        for _ in range(MOBA_TOPK):
            top = jnp.max(gate, axis=0, keepdims=True)
            idx = jnp.min(jnp.where(gate == top, blk, float(nblk)), axis=0, keepdims=True)
            hit = blk == idx
            sel = jnp.where(hit & (top > NEG_INF), 1.0, sel)
            gate = jnp.where(hit, NEG_INF, gate)
        sel_s[h] = sel

    for h in heads:
        s = jnp.where(causal, own[h], NEG_INF)
        m0 = jnp.max(s, axis=0, keepdims=True)
        m_s[h] = m0
        out_s[acc_rows[h], :] = jnp.dot(vt_ref[i, acc_rows[h], :], jnp.exp2(s - m0).astype(BF16),
                                        preferred_element_type=F32)

    def stage(j_next, buf_next, j, buf):
        k_next = keys(jnp.minimum(j_next, nblk - 1))
        jc = jnp.minimum(j, nblk - 1)
        bar = jnp.where(j < i, 0.0, 2.0)
        for h in heads:
            buf_next[h] = lax.dot_general(k_next, qh_s[h], _NT, preferred_element_type=F32)
        for h in heads:
            bias = jnp.where(sel_s[h, pl.ds(jc, 1), :] > bar, 0.0, NEG_INF)
            s = buf[h]
            m = m_s[h]
            m_new = jnp.maximum(m, jnp.max(s, axis=0, keepdims=True) + bias)
            p = jnp.exp2(s + (bias - m_new)).astype(BF16)
            m_s[h] = m_new
            out_s[acc_rows[h], :] = (jnp.exp2(m - m_new) * out_s[acc_rows[h], :]
                                     + jnp.dot(vt_ref[jc, acc_rows[h], :], p, preferred_element_type=F32))

    def step(t, carry):
        stage(2 * t + 1, sb_s, 2 * t, sa_s)
        stage(2 * t + 2, sa_s, 2 * t + 1, sb_s)
        return carry

    lax.fori_loop(0, (i + 1) // 2, step, 0)

    outs = []
    for h in heads:
        acc = out_s[acc_rows[h], :]
        outs.append(acc[:HEAD_DIM, :] / acc[HEAD_DIM:HEAD_DIM + 1, :])
    o_ref[...] = jnp.concatenate(outs, axis=0).T.astype(o_ref.dtype)


def _moba(qp, kp, vtp, kmean):
    b, tiles, _, s, _ = qp.shape
    nblk = s // MOBA_BLOCK
    acc_w = HEADS_PER_TILE * MOBA_ACC_ROWS
    return pl.pallas_call(
        _moba_body,
        out_shape=jax.ShapeDtypeStruct((b, s, MOBA_WIDTH), BF16),
        grid=(b, tiles, nblk),
        in_specs=[
            pl.BlockSpec((None, None, None, MOBA_BLOCK, LANE_TILE), lambda bi, hg, i: (bi, hg, 0, i, 0)),
            pl.BlockSpec((None, None, None, s, LANE_TILE), lambda bi, hg, i: (bi, hg, 0, 0, 0)),
            pl.BlockSpec((None, None, None, nblk, acc_w, MOBA_BLOCK), lambda bi, hg, i: (bi, hg, 0, 0, 0, 0)),
            pl.BlockSpec((None, None, nblk, LANE_TILE), lambda bi, hg, i: (bi, hg, 0, 0)),
        ],
        out_specs=pl.BlockSpec((None, MOBA_BLOCK, LANE_TILE), lambda bi, hg, i: (bi, i, hg)),
        scratch_shapes=[
            pltpu.VMEM((HEADS_PER_TILE, nblk, MOBA_BLOCK), F32),
            pltpu.VMEM((HEADS_PER_TILE, MOBA_BLOCK, LANE_TILE), BF16),
            pltpu.VMEM((HEADS_PER_TILE, MOBA_BLOCK, MOBA_BLOCK), F32),
            pltpu.VMEM((HEADS_PER_TILE, MOBA_BLOCK, MOBA_BLOCK), F32),
            pltpu.VMEM((HEADS_PER_TILE, 1, MOBA_BLOCK), F32),
            pltpu.VMEM((HEADS_PER_TILE * MOBA_ACC_ROWS, MOBA_BLOCK), F32),
        ],
        compiler_params=_params("parallel", "parallel", "arbitrary"),
        name="moba_attn",
    )(qp, kp, vtp, kmean)


def _conv_body(bg_ref, cg_ref, xz_ref, w_ref, o_ref, *, ts):
    halo = 16
    w = w_ref[...]
    for c in range(o_ref.shape[0] // ts):
        lo = c * ts
        if c == 0:
            z = cg_ref[0:ts, :].astype(F32) * xz_ref[0:ts, :].astype(F32)
            zp = jnp.concatenate([jnp.zeros((halo, CONV_WIDTH), F32), z], axis=0)
        else:
            zp = cg_ref[lo - halo:lo + ts, :].astype(F32) * xz_ref[lo - halo:lo + ts, :].astype(F32)
        z1 = pltpu.roll(zp, 1, 0)[halo:]
        z2 = pltpu.roll(zp, 2, 0)[halo:]
        y = w[2:3, :] * zp[halo:] + w[1:2, :] * z1 + w[0:1, :] * z2
        o_ref[lo:lo + ts, :] = (bg_ref[lo:lo + ts, :].astype(F32) * y).astype(o_ref.dtype)


def _short_conv(p3, conv_w, *, col0, ts=512):
    b, s, _ = p3.shape
    taps = conv_w.shape[0]

    def col(c):
        return pl.BlockSpec((None, s, CONV_WIDTH), lambda bi: (bi, 0, col0 + c))

    return pl.pallas_call(
        functools.partial(_conv_body, ts=ts),
        out_shape=jax.ShapeDtypeStruct((b, s, CONV_WIDTH), BF16),
        grid=(b,),
        in_specs=[col(0), col(1), col(2), pl.BlockSpec((taps, CONV_WIDTH), lambda bi: (0, 0))],
        out_specs=pl.BlockSpec((None, s, CONV_WIDTH), lambda bi: (bi, 0, 0)),
        compiler_params=_params("parallel"),
        name="short_conv",
    )(p3, p3, p3, conv_w)


def _odd_out_body(x_ref, c_ref, d_ref, wc_ref, wd_ref, gain_ref, rw_ref, xo_ref, h_ref, route_ref):
    x = x_ref[...] + jnp.dot(c_ref[...], wc_ref[...], preferred_element_type=F32)
    x = x + jnp.dot(d_ref[...], wd_ref[...], preferred_element_type=F32)
    xo_ref[...] = x
    h = _row_norm(x, gain_ref[...])
    _store_token_tiles(h_ref, h)
    rw = rw_ref[...]
    h_hi = h.astype(BF16)
    h_lo = (h - h_hi.astype(F32)).astype(BF16)
    rw_hi = rw.astype(BF16)
    rw_lo = (rw - rw_hi.astype(F32)).astype(BF16)
    logits = (jnp.dot(h_hi, rw_hi, preferred_element_type=F32) + jnp.dot(h_lo, rw_hi, preferred_element_type=F32)
              + jnp.dot(h_hi, rw_lo, preferred_element_type=F32))
    lane = lax.broadcasted_iota(I32, logits.shape, 1)
    lane_f = lane.astype(F32)
    logits = jnp.where(lane < N_EXPERTS, logits, NEG_INF)
    v1 = jnp.max(logits, axis=-1, keepdims=True)
    i1 = jnp.min(jnp.where(logits == v1, lane_f, float(ROUTE_LANES)), axis=-1, keepdims=True)
    rest = jnp.where(lane_f == i1, NEG_INF, logits)
    v2 = jnp.max(rest, axis=-1, keepdims=True)
    i2 = jnp.min(jnp.where(rest == v2, lane_f, float(ROUTE_LANES)), axis=-1, keepdims=True)
    e = jnp.exp(v2 - v1)
    g1 = 1.0 / (1.0 + e)
    g2 = e / (1.0 + e)
    route = jnp.where(lane == 0, i1, 0.0)
    route = jnp.where(lane == 1, i2, route)
    route = jnp.where(lane == 2, g1, route)
    route_ref[...] = jnp.where(lane == 3, g2, route)


def _odd_out(x2, c2, d2, wc, wd, gain, router_pad, *, tm=512):
    t, d = x2.shape
    row = lambda w: pl.BlockSpec((tm, w), lambda i: (i, 0))
    full = lambda a: pl.BlockSpec(a.shape, lambda i: (0, 0))
    return pl.pallas_call(
        _odd_out_body,
        out_shape=[jax.ShapeDtypeStruct((t, d), F32), jax.ShapeDtypeStruct((t * ROW_SUBLANES, ROW_LANES), F32),
                   jax.ShapeDtypeStruct((t, ROUTE_LANES), F32)],
        grid=(t // tm,),
        in_specs=[row(d), row(MOBA_WIDTH), row(CONV_WIDTH), full(wc), full(wd),
                  pl.BlockSpec((1, d), lambda i: (0, 0)), full(router_pad)],
        out_specs=[row(d), pl.BlockSpec((tm * ROW_SUBLANES, ROW_LANES), lambda i: (i, 0)), row(ROUTE_LANES)],
        compiler_params=_params("parallel"),
        name="odd_out_proj_router",
    )(x2, c2, d2, wc, wd, gain.reshape(1, d), router_pad)


def _token_rows(token, count=1):
    return pl.ds(pl.multiple_of(token * ROW_SUBLANES, ROW_SUBLANES), count * ROW_SUBLANES)


def _row_copy(src_ref, src_row, dst_ref, dst_row, sem):
    return pltpu.make_async_copy(src_ref.at[_token_rows(src_row)], dst_ref.at[_token_rows(dst_row)], sem)


def _dispatch_body(pos_ref, pad_ref, h_ref, xs_ref, zeros_s, sem, pad_sem):
    td = h_ref.shape[0] // ROW_SUBLANES

    @pl.when(pl.program_id(0) == 0)
    def _():
        zeros_s[...] = jnp.zeros_like(zeros_s)
        tm = zeros_s.shape[0] // ROW_SUBLANES
        n_tiles = xs_ref.shape[0] // zeros_s.shape[0]
        n_live = pad_ref[N_EXPERTS]

        def fills():
            for e in range(N_EXPERTS):
                start = jnp.maximum(pad_ref[e], 0)
                yield pad_ref[e] >= 0, pltpu.make_async_copy(zeros_s, xs_ref.at[_token_rows(start, tm)], pad_sem)
            for c in range(N_EXPERTS):
                tile = n_tiles - 1 - c
                yield tile >= n_live, pltpu.make_async_copy(zeros_s, xs_ref.at[_token_rows(tile * tm, tm)], pad_sem)

        for has_tile, cp in fills():
            pl.when(has_tile)(cp.start)
        for has_tile, cp in fills():
            pl.when(has_tile)(cp.wait)

    def copies(r):
        return [_row_copy(h_ref, r, xs_ref, pos_ref[0, 0, k * td + r], sem) for k in range(2)]

    def start(r, c):
        for k, cp in enumerate(copies(r)):
            cp.start(priority=k)
        return c

    lax.fori_loop(0, td, start, 0, unroll=DMA_UNROLL)
    for _ in range(2):
        pltpu.make_async_copy(h_ref, xs_ref.at[_token_rows(0, td)], sem).wait()


def _dispatch(h3, pos3, pad_tiles, n_rows, *, td, tm):
    t = h3.shape[0] // ROW_SUBLANES
    return pl.pallas_call(
        _dispatch_body,
        out_shape=jax.ShapeDtypeStruct((n_rows * ROW_SUBLANES, ROW_LANES), F32),
        grid=(t // td,),
        in_specs=[
            pl.BlockSpec((1, 1, 2 * td), lambda i: (i, 0, 0), memory_space=pltpu.SMEM),
            pl.BlockSpec(memory_space=pltpu.SMEM),
            pl.BlockSpec((td * ROW_SUBLANES, ROW_LANES), lambda i: (i, 0)),
        ],
        out_specs=pl.BlockSpec(memory_space=pl.ANY),
        scratch_shapes=[pltpu.VMEM((tm * ROW_SUBLANES, ROW_LANES), F32), pltpu.SemaphoreType.DMA(()),
                        pltpu.SemaphoreType.DMA(())],
        compiler_params=_params("arbitrary"),
        name="expert_dispatch",
    )(pos3, pad_tiles, h3)


def _combine_body(pos_ref, x_ref, route_ref, y_ref, o_ref, buf, sems):
    i = pl.program_id(0)
    n = pl.num_programs(0)
    td = x_ref.shape[0]
    t = n * td

    def issue(tile, slot):
        def start(r, c):
            for k in range(2):
                _row_copy(y_ref, pos_ref[k * t + tile * td + r], buf.at[slot, k], r,
                          sems.at[slot]).start(priority=k)
            return c

        lax.fori_loop(0, td, start, 0, unroll=DMA_UNROLL)

    @pl.when(i == 0)
    def _():
        issue(0, 0)

    slot = i % 2

    @pl.when(i + 1 < n)
    def _():
        issue(i + 1, 1 - slot)

    for k in range(2):
        pltpu.make_async_copy(y_ref.at[_token_rows(0, td)], buf.at[slot, k], sems.at[slot]).wait()
    route = route_ref[...]
    o_ref[...] = (x_ref[...] + route[:, 2:3] * _load_token_tiles(buf.at[slot, 0])
                  + route[:, 3:4] * _load_token_tiles(buf.at[slot, 1]))


def _combine(x2, route, ys, pos, *, td):
    t, d = x2.shape
    return pl.pallas_call(
        _combine_body,
        out_shape=jax.ShapeDtypeStruct((t, d), F32),
        grid_spec=pltpu.PrefetchScalarGridSpec(
            num_scalar_prefetch=1,
            grid=(t // td,),
            in_specs=[
                pl.BlockSpec((td, d), lambda i, pos: (i, 0)),
                pl.BlockSpec((td, ROUTE_LANES), lambda i, pos: (i, 0)),
                pl.BlockSpec(memory_space=pl.ANY),
            ],
            out_specs=pl.BlockSpec((td, d), lambda i, pos: (i, 0)),
            scratch_shapes=[pltpu.VMEM((2, 2, td * ROW_SUBLANES, ROW_LANES), F32), pltpu.SemaphoreType.DMA((2,))],
        ),
        compiler_params=_params("arbitrary"),
        name="expert_combine",
    )(pos, x2, route, ys)


def _route_plan(route, *, tm, td):
    t = route.shape[0]
    experts = jnp.concatenate([route[:, 0], route[:, 1]]).astype(I32)
    onehot = (experts[:, None] == jnp.arange(N_EXPERTS, dtype=I32)[None, :]).astype(I32)
    running = jnp.cumsum(onehot, axis=0)
    counts = running[-1]
    padded = ((counts + tm - 1) // tm) * tm
    ends = jnp.cumsum(padded)
    starts = ends - padded
    pos = jnp.sum(onehot * (starts[None, :] + running - 1), axis=1)
    n_tiles = (2 * t) // tm + N_EXPERTS
    tile_lo = jnp.arange(n_tiles, dtype=I32) * tm
    tile_expert = jnp.sum((tile_lo[:, None] >= ends[None, :]).astype(I32), axis=1)
    n_live = (ends[-1] // tm).astype(I32)
    last = tile_expert[jnp.maximum(n_live - 1, 0)]
    live = jnp.arange(n_tiles) < n_live
    tile_expert = jnp.where(live, tile_expert, last)
    prev = jnp.concatenate([jnp.full((1,), -1, I32), tile_expert[:-1]])
    tile_first = (live & (tile_expert != prev)).astype(I32)
    pos3 = pos.reshape(2, t // td, td).transpose(1, 0, 2).reshape(t // td, 1, 2 * td)
    pad_tiles = jnp.concatenate([jnp.where(padded > 0, ends - tm, -1), n_live.reshape(1)]).astype(I32)
    return pos, pos3, pad_tiles, tile_expert, tile_first, n_live.reshape(1), n_tiles * tm


def _rope_tables(seq):
    inv = 1.0 / (ROPE_THETA ** (jnp.arange(0, HEAD_DIM, 2, dtype=F32) / HEAD_DIM))
    ang = jnp.arange(seq, dtype=F32)[:, None] * inv[None, :]
    ang = jnp.concatenate([ang, ang], axis=-1)
    tile = lambda a: jnp.tile(a, (1, HEADS_PER_TILE))
    return tile(jnp.cos(ang)), tile(jnp.sin(ang))


def _even_layer(x2, b, s, cos_t, sin_t, norm_mix, norm_ffn, w_in, v_gain, ws, bs, gq, gk, w_out, wg, wu, wd):
    p = _norm_proj(x2, norm_mix, w_in.astype(BF16))
    p3 = p.reshape(b, s, p.shape[1])
    a = _gmlp(p3, v_gain, ws, bs)
    q0 = 2 * GMLP_WIDTH // LANE_TILE
    n_groups = len(DIL_CONFIGS)
    accs = []
    for group, (window, dil) in enumerate(DIL_CONFIGS):
        assert window // dil == BAND
        qg, kg, vtg = _qkv_prep(p3, cos_t, sin_t, gq, gk, q_col=q0 + group,
                                k_col=q0 + n_groups + group, v_col=q0 + 2 * n_groups + group, n_ht=1, dil=dil,
                                tt=max(4, dil) * BAND, vblock=BAND, head_rows=BAND_ACC_ROWS, with_kmean=False)
        accs.append(_band_attn(qg, kg, vtg))
    w_out = w_out.astype(BF16)
    x3 = _even_out_ffn(x2.reshape(b, s, -1), a, accs, w_out[:GMLP_WIDTH], w_out[GMLP_WIDTH:], norm_ffn,
                       wg.astype(BF16), wu.astype(BF16), wd.astype(BF16))
    return x3.reshape(b * s, -1)


def _odd_layer(x2, b, s, cos_t, sin_t, norm_mix, norm_ffn, w_in, gq, gk, conv_w, w_out, router_w, wg, wu, wd,
               *, layer, tm=512, td=256):
    p = _norm_proj(x2, norm_mix, w_in.astype(BF16))
    p3 = p.reshape(b, s, p.shape[1])
    tiles = MOBA_WIDTH // LANE_TILE
    qp, kp, vtp, kmean = _qkv_prep(p3, cos_t, sin_t, gq, gk, q_col=0, k_col=tiles,
                                   v_col=2 * tiles, n_ht=tiles, dil=1, tt=8 * MOBA_BLOCK, vblock=MOBA_BLOCK,
                                   head_rows=MOBA_ACC_ROWS, with_kmean=True)
    c = _moba(qp, kp, vtp, kmean)
    dconv = _short_conv(p3, conv_w, col0=3 * MOBA_WIDTH // CONV_WIDTH)
    w_out = w_out.astype(BF16)
    router_pad = jnp.pad(router_w, ((0, 0), (0, ROUTE_LANES - N_EXPERTS)))
    x2, h2, route = _odd_out(x2, c.reshape(b * s, MOBA_WIDTH), dconv.reshape(b * s, CONV_WIDTH),
                             w_out[:MOBA_WIDTH], w_out[MOBA_WIDTH:], norm_ffn, router_pad)
    pos, pos3, pad_tiles, tile_expert, tile_first, n_live, n_rows = _route_plan(route, tm=tm, td=td)
    xs = _dispatch(h2, pos3, pad_tiles, n_rows, td=td, tm=tm)
    ys = _expert_ffn(xs, tile_expert, tile_first, n_live, wg, wu, wd, layer=layer, tm=tm)
    return _combine(x2, route, ys, pos, td=td)


def kernel(x, norm_mix, norm_ffn, w_in_ab, gmlp_v_gain, gmlp_ws, gmlp_bs, dil_q_gain, dil_k_gain, w_out_ab,
           ffn_w_gate, ffn_w_up, ffn_w_down, w_in_cd, moba_q_gain, moba_k_gain, conv_w, w_out_cd, router_w,
           moe_w_gate, moe_w_up, moe_w_down):
    b, s, d = x.shape
    depth = norm_mix.shape[0]
    cos_t, sin_t = _rope_tables(s)
    x2 = x.reshape(b * s, d)
    for layer in range(depth):
        i = layer // 2
        if layer % 2 == 0:
            x2 = _even_layer(x2, b, s, cos_t, sin_t, norm_mix[layer], norm_ffn[layer], w_in_ab[i], gmlp_v_gain[i],
                             gmlp_ws[i], gmlp_bs[i], dil_q_gain[i], dil_k_gain[i], w_out_ab[i], ffn_w_gate[i],
                             ffn_w_up[i], ffn_w_down[i])
        else:
            x2 = _odd_layer(x2, b, s, cos_t, sin_t, norm_mix[layer], norm_ffn[layer], w_in_cd[i], moba_q_gain[i],
                            moba_k_gain[i], conv_w[i], w_out_cd[i], router_w[i], moe_w_gate, moe_w_up,
                            moe_w_down, layer=i)
    return x2.reshape(b, s, d)
```

```python
import functools

import jax
import jax.numpy as jnp
from jax import lax
from jax.experimental import pallas as pl
from jax.experimental.pallas import tpu as pltpu

F32 = jnp.float32
BF16 = jnp.bfloat16
I32 = jnp.int32

D_MODEL = 1024
HEAD_DIM = 64
ROPE_THETA = 10000.0
NORM_EPS = 1e-6
LANE_TILE = 256
HEADS_PER_TILE = LANE_TILE // HEAD_DIM
GMLP_WIDTH = 256
GMLP_CHUNK = 128
DIL_CONFIGS = ((128, 1), (512, 4), (2048, 16))
DIL_WIDTH = 768
BAND = 128
BAND_ACC_ROWS = 128
BAND_L_ROW = HEAD_DIM
BAND_M_ROW = HEAD_DIM + 1
BAND_LOOKAHEAD = 5
STRIDE_LANES = 128
MOBA_WIDTH = 768
MOBA_BLOCK = 256
MOBA_TOPK = 3
MOBA_ACC_ROWS = HEAD_DIM + 16
LOG2_E = 1.4426950408889634
MOBA_M_INIT = -1e30
CONV_WIDTH = 256
N_EXPERTS = 8
ROUTE_LANES = 128
NEG_INF = float("-inf")
DMA_UNROLL = 8
ROW_LANES = 128
ROW_SUBLANES = 8

_NT = (((1,), (1,)), ((), ()))


def _params(*sem):
    return pltpu.CompilerParams(dimension_semantics=tuple(sem), vmem_limit_bytes=56 * 1024 * 1024)


def _lane_ids():
    return lax.broadcasted_iota(I32, (1, LANE_TILE), 1)


def _head_block_diag():
    r = lax.broadcasted_iota(I32, (LANE_TILE, LANE_TILE), 0) // HEAD_DIM
    c = lax.broadcasted_iota(I32, (LANE_TILE, LANE_TILE), 1) // HEAD_DIM
    return jnp.where(r == c, 1.0, 0.0).astype(BF16)


def _head_mean_sq(x, bd):
    s = x * x
    hi = s.astype(BF16)
    lo = (s - hi.astype(F32)).astype(BF16)
    tot = jnp.dot(hi, bd, preferred_element_type=F32) + jnp.dot(lo, bd, preferred_element_type=F32)
    return tot * (1.0 / HEAD_DIM)


def _head_norm(x, gain, bd):
    return x * lax.rsqrt(_head_mean_sq(x, bd) + NORM_EPS) * gain


def _row_norm(x, gain):
    return x * lax.rsqrt(jnp.mean(x * x, axis=-1, keepdims=True) + NORM_EPS) * gain


def _load_token_tiles(ref):
    n = ref.shape[0] // ROW_SUBLANES
    return jnp.concatenate([ref[pl.ds(c, n, stride=ROW_SUBLANES), :] for c in range(ROW_SUBLANES)], axis=1)


def _store_token_tiles(ref, val):
    n = ref.shape[0] // ROW_SUBLANES
    for c in range(ROW_SUBLANES):
        ref[pl.ds(c, n, stride=ROW_SUBLANES), :] = val[:, c * ROW_LANES:(c + 1) * ROW_LANES]


def _norm_proj_body(x_ref, g_ref, w_ref, o_ref, *, tn):
    hb = _row_norm(x_ref[...], g_ref[...]).astype(BF16)
    for c in range(o_ref.shape[-1] // tn):
        sl = slice(c * tn, (c + 1) * tn)
        o_ref[:, sl] = jnp.dot(hb, w_ref[:, sl], preferred_element_type=F32).astype(o_ref.dtype)


def _norm_proj(x2, gain, w_bf16, *, tm=1024, tn=256):
    t, d = x2.shape
    n = w_bf16.shape[1]
    return pl.pallas_call(
        functools.partial(_norm_proj_body, tn=tn),
        out_shape=jax.ShapeDtypeStruct((t, n), BF16),
        grid=(t // tm,),
        in_specs=[
            pl.BlockSpec((tm, d), lambda i: (i, 0)),
            pl.BlockSpec((1, d), lambda i: (0, 0)),
            pl.BlockSpec((d, n), lambda i: (0, 0)),
        ],
        out_specs=pl.BlockSpec((tm, n), lambda i: (i, 0)),
        compiler_params=_params("parallel"),
        name="norm_proj",
    )(x2, gain.reshape(1, d), w_bf16)


def _gmlp_body(u_ref, v_ref, gain_ref, ws_ref, bias_ref, o_ref):
    lane = _lane_ids()
    bd = _head_block_diag()
    r = lax.broadcasted_iota(I32, (GMLP_CHUNK, GMLP_CHUNK), 0)
    c = lax.broadcasted_iota(I32, (GMLP_CHUNK, GMLP_CHUNK), 1)
    w_tril = [jnp.where(r >= c, ws_ref[g], 0.0).astype(BF16) for g in range(HEADS_PER_TILE)]
    for ch in range(o_ref.shape[0] // GMLP_CHUNK):
        sl = slice(ch * GMLP_CHUNK, (ch + 1) * GMLP_CHUNK)
        u = jax.nn.gelu(u_ref[sl, :].astype(F32))
        v = jax.nn.gelu(v_ref[sl, :].astype(F32))
        vn = _head_norm(v, gain_ref[...], bd).astype(BF16)
        y = bias_ref[...]
        for g in range(HEADS_PER_TILE):
            yg = jnp.dot(w_tril[g], vn, preferred_element_type=F32)
            y = y + jnp.where(lane // HEAD_DIM == g, yg, 0.0)
        o_ref[sl, :] = (u * y).astype(o_ref.dtype)


def _gmlp(p3, v_gain, ws, bs, *, tc=512):
    b, s, _ = p3.shape
    groups, chunk = bs.shape
    bias = jnp.repeat(bs.T, HEAD_DIM, axis=1)
    return pl.pallas_call(
        _gmlp_body,
        out_shape=jax.ShapeDtypeStruct((b, s, GMLP_WIDTH), BF16),
        grid=(b, s // tc),
        in_specs=[
            pl.BlockSpec((None, tc, LANE_TILE), lambda bi, i: (bi, i, 0)),
            pl.BlockSpec((None, tc, LANE_TILE), lambda bi, i: (bi, i, 1)),
            pl.BlockSpec((1, GMLP_WIDTH), lambda bi, i: (0, 0)),
            pl.BlockSpec((groups, chunk, chunk), lambda bi, i: (0, 0, 0)),
            pl.BlockSpec((chunk, GMLP_WIDTH), lambda bi, i: (0, 0)),
        ],
        out_specs=pl.BlockSpec((None, tc, GMLP_WIDTH), lambda bi, i: (bi, i, 0)),
        compiler_params=_params("parallel", "parallel"),
        name="gmlp_gate",
    )(p3, p3, v_gain.reshape(1, GMLP_WIDTH), ws, bias)


def _residues(val, scr, dil):
    if dil == 1:
        return [val]
    rows = val.shape[0] // dil
    slabs = LANE_TILE // STRIDE_LANES
    for c in range(slabs):
        scr[c] = val[:, c * STRIDE_LANES:(c + 1) * STRIDE_LANES]
    return [jnp.concatenate([scr[c, pl.ds(r, rows, stride=dil), :] for c in range(slabs)], axis=1)
            for r in range(dil)]


def _norm_rope(x, tab_a, tab_b, bd, lane):
    sq = (x * x).astype(BF16)
    xn = x * lax.rsqrt(jnp.dot(sq, bd, preferred_element_type=F32) * (1.0 / HEAD_DIM) + NORM_EPS)
    half = HEAD_DIM // 2
    partner = jnp.where((lane % HEAD_DIM) < half, pltpu.roll(xn, LANE_TILE - half, 1), pltpu.roll(xn, half, 1))
    return xn * tab_a + partner * tab_b


def _qkv_prep_body(q_ref, k_ref, v_ref, qa_ref, qb_ref, ka_ref, kb_ref, *rest, dil, with_kmean):
    n_out = 4 if with_kmean else 3
    qo_ref, ko_ref, vt_ref = rest[:3]
    scr = list(rest[n_out:]) + [None] * 3
    lane = _lane_ids()
    bd = _head_block_diag()
    qn = _norm_rope(q_ref[...].astype(F32), qa_ref[...], qb_ref[...], bd, lane)
    kn = _norm_rope(k_ref[...].astype(F32), ka_ref[...], kb_ref[...], bd, lane)
    if with_kmean:
        km_ref = rest[3]
        km_ref[...] = jnp.mean(kn.reshape(km_ref.shape[0], MOBA_BLOCK, LANE_TILE), axis=1)
    for r, piece in enumerate(_residues(qn, scr[0], dil)):
        qo_ref[r] = piece.astype(BF16)
    for r, piece in enumerate(_residues(kn, scr[1], dil)):
        ko_ref[r] = piece.astype(BF16)
    n_blocks, acc_w, vblock = vt_ref.shape[1:]
    head_rows = acc_w // HEADS_PER_TILE
    ones = jnp.ones((head_rows - HEAD_DIM, vblock), BF16)
    for r, piece in enumerate(_residues(v_ref[...].astype(F32), scr[2], dil)):
        for c in range(n_blocks):
            vt = piece[c * vblock:(c + 1) * vblock, :].T.astype(BF16)
            for h in range(HEADS_PER_TILE):
                vt_ref[r, c, h * head_rows:h * head_rows + HEAD_DIM, :] = vt[h * HEAD_DIM:(h + 1) * HEAD_DIM, :]
                vt_ref[r, c, h * head_rows + HEAD_DIM:(h + 1) * head_rows, :] = ones


def _rope_gain_tables(cos_t, sin_t, gain, scale):
    half = HEAD_DIM // 2
    swapped = jnp.concatenate([gain[half:], gain[:half]])
    sign = jnp.concatenate([-jnp.ones((half,), F32), jnp.ones((half,), F32)])
    tile = lambda v: jnp.tile(v, HEADS_PER_TILE)[None, :]
    return cos_t * (tile(gain) * scale), sin_t * (tile(sign * swapped) * scale)


def _qkv_prep(p3, cos_t, sin_t, gq, gk, *, q_col, k_col, v_col, n_ht, dil, tt, vblock, head_rows, with_kmean):
    b, s, _ = p3.shape
    length = s // dil
    tr = tt // dil
    acc_w = HEADS_PER_TILE * head_rows

    def col(c0):
        return pl.BlockSpec((None, tt, LANE_TILE), lambda i, bi, ht: (bi, i, c0 + ht))

    tab = pl.BlockSpec((tt, LANE_TILE), lambda i, bi, ht: (i, 0))
    tables = _rope_gain_tables(cos_t, sin_t, gq, HEAD_DIM ** -0.5 * LOG2_E) + _rope_gain_tables(cos_t, sin_t, gk, 1.0)
    qk_shape = jax.ShapeDtypeStruct((b, n_ht, dil, length, LANE_TILE), BF16)
    qk_spec = pl.BlockSpec((None, None, dil, tr, LANE_TILE), lambda i, bi, ht: (bi, ht, 0, i, 0))
    out_shape = [qk_shape, qk_shape,
                 jax.ShapeDtypeStruct((b, n_ht, dil, length // vblock, acc_w, vblock), BF16)]
    out_specs = [qk_spec, qk_spec,
                 pl.BlockSpec((None, None, dil, tr // vblock, acc_w, vblock), lambda i, bi, ht: (bi, ht, 0, i, 0, 0))]
    if with_kmean:
        out_shape.append(jax.ShapeDtypeStruct((b, n_ht, s // MOBA_BLOCK, LANE_TILE), F32))
        out_specs.append(pl.BlockSpec((None, None, tt // MOBA_BLOCK, LANE_TILE), lambda i, bi, ht: (bi, ht, i, 0)))
    scratch = [pltpu.VMEM((LANE_TILE // STRIDE_LANES, tt, STRIDE_LANES), F32)] * (3 if dil > 1 else 0)
    return pl.pallas_call(
        functools.partial(_qkv_prep_body, dil=dil, with_kmean=with_kmean),
        out_shape=out_shape,
        grid=(s // tt, b, n_ht),
        in_specs=[col(q_col), col(k_col), col(v_col), tab, tab, tab, tab],
        out_specs=out_specs,
        scratch_shapes=scratch,
        compiler_params=_params("parallel", "parallel", "parallel"),
        name=f"qkv_prep_d{dil}",
    )(p3, p3, p3, *tables)


def _band_attn_body(q_ref, k_ref, kp_ref, vt_ref, vtp_ref, o_ref):
    i = pl.program_id(2)
    lb = q_ref.shape[0]
    lane = _lane_ids()
    kk = lax.broadcasted_iota(I32, (2 * BAND, BAND), 0)
    qq = lax.broadcasted_iota(I32, (2 * BAND, BAND), 1) + BAND
    in_band = (qq - kk >= 0) & (qq - kk <= BAND)
    first_ok = in_band & ((kk >= BAND) | (i > 0))
    acc_row = lax.broadcasted_iota(I32, (BAND_ACC_ROWS, BAND), 0)
    k_all = jnp.concatenate([kp_ref[...], k_ref[...]], axis=0)
    chains = [(j, h) for j in range(lb // BAND) for h in range(HEADS_PER_TILE)]

    def scores(j, h):
        qj = q_ref[j * BAND:(j + 1) * BAND, :]
        qh = jnp.where(lane // HEAD_DIM == h, qj, jnp.zeros_like(qj))
        return lax.dot_general(k_all[j * BAND:(j + 2) * BAND], qh, _NT, preferred_element_type=F32)

    staged = [scores(*c) for c in chains[:BAND_LOOKAHEAD]]
    tiles = []
    for n, (j, h) in enumerate(chains):
        if n + BAND_LOOKAHEAD < len(chains):
            staged.append(scores(*chains[n + BAND_LOOKAHEAD]))
        rows = slice(h * BAND_ACC_ROWS, (h + 1) * BAND_ACC_ROWS)
        s = jnp.where(first_ok if j == 0 else in_band, staged[n], NEG_INF)
        m = jnp.max(s, axis=0, keepdims=True)
        p = jnp.exp2(s - m).astype(BF16)
        v_prev = vtp_ref[0, rows, :] if j == 0 else vt_ref[j - 1, rows, :]
        vw = jnp.concatenate([v_prev, vt_ref[j, rows, :]], axis=1)
        acc = jnp.dot(vw, p, preferred_element_type=F32)
        tiles.append(jnp.where(acc_row == BAND_M_ROW, m, acc))
        if h == HEADS_PER_TILE - 1:
            o_ref[j * BAND:(j + 1) * BAND, :] = jnp.concatenate(tiles, axis=0).T
            tiles = []


def _band_attn(qg, kg, vtg):
    b, _, dil, length, _ = qg.shape
    lb = min(length, 512)
    sub = lb // BAND
    acc_w = HEADS_PER_TILE * BAND_ACC_ROWS
    before = lambda i: jnp.maximum(i * sub - 1, 0)
    cur = pl.BlockSpec((None, None, None, lb, LANE_TILE), lambda bi, r, i: (bi, 0, r, i, 0))
    prev = pl.BlockSpec((None, None, None, BAND, LANE_TILE), lambda bi, r, i: (bi, 0, r, before(i), 0))
    vt_cur = pl.BlockSpec((None, None, None, sub, acc_w, BAND), lambda bi, r, i: (bi, 0, r, i, 0, 0))
    vt_prev = pl.BlockSpec((None, None, None, 1, acc_w, BAND), lambda bi, r, i: (bi, 0, r, before(i), 0, 0))
    return pl.pallas_call(
        _band_attn_body,
        out_shape=jax.ShapeDtypeStruct((b, dil, length, acc_w), F32),
        grid=(b, dil, length // lb),
        in_specs=[cur, cur, prev, vt_cur, vt_prev],
        out_specs=pl.BlockSpec((None, None, lb, acc_w), lambda bi, r, i: (bi, r, i, 0)),
        compiler_params=_params("parallel", "parallel", "arbitrary"),
        name=f"band_attn_d{dil}",
    )(qg, kg, kg, vtg, vtg)


def _even_out_body(x_ref, a_ref, o0_ref, o1_ref, o2_ref, wa_ref, wb_ref, out_ref, *scratch):
    tm = x_ref.shape[0]

    def natural_order(o_ref, scr):
        dil = o_ref.shape[0]
        if dil == 1:
            return [o_ref[0, :, h * BAND_ACC_ROWS:(h + 1) * BAND_ACC_ROWS] for h in range(HEADS_PER_TILE)]
        for r in range(dil):
            for h in range(HEADS_PER_TILE):
                scr[h, pl.ds(r, tm // dil, stride=dil), :] = o_ref[r, :, h * BAND_ACC_ROWS:(h + 1) * BAND_ACC_ROWS]
        return [scr[h] for h in range(HEADS_PER_TILE)]

    o_refs = (o0_ref, o1_ref, o2_ref)
    spare = list(scratch)
    groups = [natural_order(o, spare.pop(0) if o.shape[0] > 1 else None) for o in o_refs]
    acc = x_ref[...] + jnp.dot(a_ref[...], wa_ref[...], preferred_element_type=F32)
    for h in range(HEADS_PER_TILE):
        tiles = [g[h] for g in groups]
        tops = [t[:, BAND_M_ROW:BAND_M_ROW + 1] for t in tiles]
        top = jnp.maximum(jnp.maximum(tops[0], tops[1]), tops[2])
        ws = [jnp.exp2(m - top) for m in tops]
        num = sum(w * t[:, :HEAD_DIM] for w, t in zip(ws, tiles))
        den = sum(w * t[:, BAND_L_ROW:BAND_L_ROW + 1] for w, t in zip(ws, tiles))
        merged = (num / den).astype(BF16)
        acc = acc + jnp.dot(merged, wb_ref[h * HEAD_DIM:(h + 1) * HEAD_DIM, :], preferred_element_type=F32)
    out_ref[...] = acc


def _even_out(x3, a3, accs, wa, wb, *, tm=512):
    b, s, d = x3.shape
    acc_w = HEADS_PER_TILE * BAND_ACC_ROWS
    full = lambda a: pl.BlockSpec(a.shape, lambda bi, i: (0, 0))

    def acc_spec(a):
        dil = a.shape[1]
        return pl.BlockSpec((None, dil, tm // dil, acc_w), lambda bi, i: (bi, 0, i, 0))

    scratch = [pltpu.VMEM((HEADS_PER_TILE, tm, BAND_ACC_ROWS), F32) for a in accs if a.shape[1] > 1]
    return pl.pallas_call(
        _even_out_body,
        out_shape=jax.ShapeDtypeStruct((b, s, d), F32),
        grid=(b, s // tm),
        in_specs=[pl.BlockSpec((None, tm, d), lambda bi, i: (bi, i, 0)),
                  pl.BlockSpec((None, tm, GMLP_WIDTH), lambda bi, i: (bi, i, 0))]
        + [acc_spec(a) for a in accs] + [full(wa), full(wb)],
        out_specs=pl.BlockSpec((None, tm, d), lambda bi, i: (bi, i, 0)),
        scratch_shapes=scratch,
        compiler_params=_params("parallel", "parallel"),
        name="even_out_proj",
    )(x3, a3, *accs, wa, wb)


def _swiglu_step(h_bf16, wg, wu, wd):
    g = jnp.dot(h_bf16, wg, preferred_element_type=F32)
    u = jnp.dot(h_bf16, wu, preferred_element_type=F32)
    a = (g * jax.nn.sigmoid(g) * u).astype(BF16)
    return jnp.dot(a, wd, preferred_element_type=F32)


def _ffn_body(x_ref, g_ref, wg_ref, wu_ref, wd_ref, o_ref, *, tf):
    x = x_ref[...]
    h = _row_norm(x, g_ref[...]).astype(BF16)
    acc = x
    for c in range(wg_ref.shape[1] // tf):
        sl = slice(c * tf, (c + 1) * tf)
        acc = acc + _swiglu_step(h, wg_ref[:, sl], wu_ref[:, sl], wd_ref[sl, :])
    o_ref[...] = acc


def _ffn(x2, gain, wg, wu, wd, *, tm=512, tf=256):
    t, d = x2.shape
    resident = lambda a: pl.BlockSpec(a.shape, lambda i: (0, 0), pipeline_mode=pl.Buffered(1))
    return pl.pallas_call(
        functools.partial(_ffn_body, tf=tf),
        out_shape=jax.ShapeDtypeStruct((t, d), F32),
        grid=(t // tm,),
        in_specs=[
            pl.BlockSpec((tm, d), lambda i: (i, 0)),
            pl.BlockSpec((1, d), lambda i: (0, 0)),
            resident(wg), resident(wu), resident(wd),
        ],
        out_specs=pl.BlockSpec((tm, d), lambda i: (i, 0)),
        compiler_params=_params("parallel"),
        name="ffn_swiglu",
    )(x2, gain.reshape(1, d), wg, wu, wd)


def _expert_body(te_ref, first_ref, nv_ref, x_ref, wg_hbm, wu_hbm, wd_hbm, o_ref,
                 cg_s, cu_s, cd_s, sg_s, su_s, sd_s, sems, *, layer, tf):
    i = pl.program_id(0)
    nf = cg_s.shape[0]
    e = te_ref[i]

    def chunk_copies(f, slot):
        cols = pl.ds(f * tf, tf)
        return (pltpu.make_async_copy(wg_hbm.at[layer, e, :, cols], sg_s.at[slot], sems.at[0, slot]),
                pltpu.make_async_copy(wu_hbm.at[layer, e, :, cols], su_s.at[slot], sems.at[1, slot]),
                pltpu.make_async_copy(wd_hbm.at[layer, e, cols, :], sd_s.at[slot], sems.at[2, slot]))

    def compute(load_chunk):
        h = _load_token_tiles(x_ref).astype(BF16)
        acc = jnp.zeros(h.shape, F32)
        for f in range(nf):
            load_chunk(f)
            acc = acc + _swiglu_step(h, cg_s[f], cu_s[f], cd_s[f])
        _store_token_tiles(o_ref, acc)

    @pl.when(first_ref[i] == 1)
    def _():
        def load_chunk(f):
            if f + 1 < nf:
                for cp in chunk_copies(f + 1, (f + 1) % 2):
                    cp.start()
            for cp in chunk_copies(f, f % 2):
                cp.wait()
            cg_s[f] = sg_s[f % 2].astype(BF16)
            cu_s[f] = su_s[f % 2].astype(BF16)
            cd_s[f] = sd_s[f % 2].astype(BF16)

        for cp in chunk_copies(0, 0):
            cp.start()
        compute(load_chunk)

    @pl.when((first_ref[i] == 0) & (i < nv_ref[0]))
    def _():
        compute(lambda f: None)

    @pl.when(i >= nv_ref[0])
    def _():
        o_ref[...] = jnp.zeros_like(o_ref)


def _expert_ffn(xs, tile_expert, tile_first, n_live, wg, wu, wd, *, layer, tm, tf=512):
    p = xs.shape[0] // ROW_SUBLANES
    d, ff = wg.shape[-2:]
    nf = ff // tf
    rows = pl.BlockSpec((tm * ROW_SUBLANES, ROW_LANES), lambda i, te, first, nv: (i, 0))
    return pl.pallas_call(
        functools.partial(_expert_body, layer=layer, tf=tf),
        out_shape=jax.ShapeDtypeStruct(xs.shape, F32),
        grid_spec=pltpu.PrefetchScalarGridSpec(
            num_scalar_prefetch=3,
            grid=(p // tm,),
            in_specs=[
                rows,
                pl.BlockSpec(memory_space=pl.ANY),
                pl.BlockSpec(memory_space=pl.ANY),
                pl.BlockSpec(memory_space=pl.ANY),
            ],
            out_specs=rows,
            scratch_shapes=[
                pltpu.VMEM((nf, d, tf), BF16),
                pltpu.VMEM((nf, d, tf), BF16),
                pltpu.VMEM((nf, tf, d), BF16),
                pltpu.VMEM((2, d, tf), F32),
                pltpu.VMEM((2, d, tf), F32),
                pltpu.VMEM((2, tf, d), F32),
                pltpu.SemaphoreType.DMA((3, 2)),
            ],
        ),
        compiler_params=_params("arbitrary"),
        name="expert_swiglu",
    )(tile_expert, tile_first, n_live, xs, wg, wu, wd)


def _moba_body(q_ref, k_ref, vt_ref, km_ref, o_ref, sel_s, qh_s, sa_s, sb_s, m_s, out_s):
    i = pl.program_id(2)
    nblk = km_ref.shape[0]
    lane = _lane_ids()
    qn = q_ref[...]
    blk = lax.broadcasted_iota(I32, (nblk, MOBA_BLOCK), 0).astype(F32)
    first_blk = i.astype(F32)
    km = km_ref[...]
    krow = lax.broadcasted_iota(I32, (MOBA_BLOCK, MOBA_BLOCK), 0)
    qcol = lax.broadcasted_iota(I32, (MOBA_BLOCK, MOBA_BLOCK), 1)
    causal = krow <= qcol

    km_hi = km.astype(BF16)
    km_lo = (km - km_hi.astype(F32)).astype(BF16)
    acc_rows = [slice(h * MOBA_ACC_ROWS, (h + 1) * MOBA_ACC_ROWS) for h in range(HEADS_PER_TILE)]

    def keys(j):
        return k_ref[pl.ds(pl.multiple_of(j * MOBA_BLOCK, MOBA_BLOCK), MOBA_BLOCK), :]

    heads = range(HEADS_PER_TILE)
    qh = [jnp.where(lane // HEAD_DIM == h, qn, jnp.zeros_like(qn)) for h in heads]
    for h in heads:
        qh_s[h] = qh[h]
    gates = [lax.dot_general(km_hi, qh[h], _NT, preferred_element_type=F32)
             + lax.dot_general(km_lo, qh[h], _NT, preferred_element_type=F32) for h in heads]
    k_own = keys(i)
    own = [lax.dot_general(k_own, qh[h], _NT, preferred_element_type=F32) for h in heads]
    k_first = keys(0)
    for h in heads:
        sa_s[h] = lax.dot_general(k_first, qh[h], _NT, preferred_element_type=F32)

    for h in heads:
        gate = jnp.where(blk < first_blk, gates[h], NEG_INF)
        sel = jnp.zeros((nblk, MOBA_BLOCK), F32)
        for _ in range(MOBA_TOPK):
            top = jnp.max(gate, axis=0, keepdims=True)
            idx = jnp.min(jnp.where(gate == top, blk, float(nblk)), axis=0, keepdims=True)
            hit = blk == idx
            sel = jnp.where(hit & (top > NEG_INF), 1.0, sel)
            gate = jnp.where(hit, NEG_INF, gate)
        sel_s[h] = sel

    for h in heads:
        s = jnp.where(causal, own[h], NEG_INF)
        m0 = jnp.max(s, axis=0, keepdims=True)
        m_s[h] = m0
        out_s[acc_rows[h], :] = jnp.dot(vt_ref[i, acc_rows[h], :], jnp.exp2(s - m0).astype(BF16),
                                        preferred_element_type=F32)

    def stage(j_next, buf_next, j, buf):
        k_next = keys(jnp.minimum(j_next, nblk - 1))
        jc = jnp.minimum(j, nblk - 1)
        bar = jnp.where(j < i, 0.0, 2.0)
        for h in heads:
            buf_next[h] = lax.dot_general(k_next, qh_s[h], _NT, preferred_element_type=F32)
        for h in heads:
            bias = jnp.where(sel_s[h, pl.ds(jc, 1), :] > bar, 0.0, NEG_INF)
            s = buf[h]
            m = m_s[h]
            m_new = jnp.maximum(m, jnp.max(s, axis=0, keepdims=True) + bias)
            p = jnp.exp2(s + (bias - m_new)).astype(BF16)
            m_s[h] = m_new
            out_s[acc_rows[h], :] = (jnp.exp2(m - m_new) * out_s[acc_rows[h], :]
                                     + jnp.dot(vt_ref[jc, acc_rows[h], :], p, preferred_element_type=F32))

    def step(t, carry):
        stage(2 * t + 1, sb_s, 2 * t, sa_s)
        stage(2 * t + 2, sa_s, 2 * t + 1, sb_s)
        return carry

    lax.fori_loop(0, (i + 1) // 2, step, 0)

    outs = []
    for h in heads:
        acc = out_s[acc_rows[h], :]
        outs.append(acc[:HEAD_DIM, :] / acc[HEAD_DIM:HEAD_DIM + 1, :])
    o_ref[...] = jnp.concatenate(outs, axis=0).T.astype(o_ref.dtype)


def _moba(qp, kp, vtp, kmean):
    b, tiles, _, s, _ = qp.shape
    nblk = s // MOBA_BLOCK
    acc_w = HEADS_PER_TILE * MOBA_ACC_ROWS
    return pl.pallas_call(
        _moba_body,
        out_shape=jax.ShapeDtypeStruct((b, s, MOBA_WIDTH), BF16),
        grid=(b, tiles, nblk),
        in_specs=[
            pl.BlockSpec((None, None, None, MOBA_BLOCK, LANE_TILE), lambda bi, hg, i: (bi, hg, 0, i, 0)),
            pl.BlockSpec((None, None, None, s, LANE_TILE), lambda bi, hg, i: (bi, hg, 0, 0, 0)),
            pl.BlockSpec((None, None, None, nblk, acc_w, MOBA_BLOCK), lambda bi, hg, i: (bi, hg, 0, 0, 0, 0)),
            pl.BlockSpec((None, None, nblk, LANE_TILE), lambda bi, hg, i: (bi, hg, 0, 0)),
        ],
        out_specs=pl.BlockSpec((None, MOBA_BLOCK, LANE_TILE), lambda bi, hg, i: (bi, i, hg)),
        scratch_shapes=[
            pltpu.VMEM((HEADS_PER_TILE, nblk, MOBA_BLOCK), F32),
            pltpu.VMEM((HEADS_PER_TILE, MOBA_BLOCK, LANE_TILE), BF16),
            pltpu.VMEM((HEADS_PER_TILE, MOBA_BLOCK, MOBA_BLOCK), F32),
            pltpu.VMEM((HEADS_PER_TILE, MOBA_BLOCK, MOBA_BLOCK), F32),
            pltpu.VMEM((HEADS_PER_TILE, 1, MOBA_BLOCK), F32),
            pltpu.VMEM((HEADS_PER_TILE * MOBA_ACC_ROWS, MOBA_BLOCK), F32),
        ],
        compiler_params=_params("parallel", "parallel", "arbitrary"),
        name="moba_attn",
    )(qp, kp, vtp, kmean)


def _conv_body(bg_ref, cg_ref, xz_ref, w_ref, o_ref, *, ts):
    halo = 16
    w = w_ref[...]
    for c in range(o_ref.shape[0] // ts):
        lo = c * ts
        if c == 0:
            z = cg_ref[0:ts, :].astype(F32) * xz_ref[0:ts, :].astype(F32)
            zp = jnp.concatenate([jnp.zeros((halo, CONV_WIDTH), F32), z], axis=0)
        else:
            zp = cg_ref[lo - halo:lo + ts, :].astype(F32) * xz_ref[lo - halo:lo + ts, :].astype(F32)
        z1 = pltpu.roll(zp, 1, 0)[halo:]
        z2 = pltpu.roll(zp, 2, 0)[halo:]
        y = w[2:3, :] * zp[halo:] + w[1:2, :] * z1 + w[0:1, :] * z2
        o_ref[lo:lo + ts, :] = (bg_ref[lo:lo + ts, :].astype(F32) * y).astype(o_ref.dtype)


def _short_conv(p3, conv_w, *, col0, ts=512):
    b, s, _ = p3.shape
    taps = conv_w.shape[0]

    def col(c):
        return pl.BlockSpec((None, s, CONV_WIDTH), lambda bi: (bi, 0, col0 + c))

    return pl.pallas_call(
        functools.partial(_conv_body, ts=ts),
        out_shape=jax.ShapeDtypeStruct((b, s, CONV_WIDTH), BF16),
        grid=(b,),
        in_specs=[col(0), col(1), col(2), pl.BlockSpec((taps, CONV_WIDTH), lambda bi: (0, 0))],
        out_specs=pl.BlockSpec((None, s, CONV_WIDTH), lambda bi: (bi, 0, 0)),
        compiler_params=_params("parallel"),
        name="short_conv",
    )(p3, p3, p3, conv_w)


def _odd_out_body(x_ref, c_ref, d_ref, wc_ref, wd_ref, gain_ref, rw_ref, xo_ref, h_ref, route_ref):
    x = x_ref[...] + jnp.dot(c_ref[...], wc_ref[...], preferred_element_type=F32)
    x = x + jnp.dot(d_ref[...], wd_ref[...], preferred_element_type=F32)
    xo_ref[...] = x
    h = _row_norm(x, gain_ref[...])
    _store_token_tiles(h_ref, h)
    rw = rw_ref[...]
    h_hi = h.astype(BF16)
    h_lo = (h - h_hi.astype(F32)).astype(BF16)
    rw_hi = rw.astype(BF16)
    rw_lo = (rw - rw_hi.astype(F32)).astype(BF16)
    logits = (jnp.dot(h_hi, rw_hi, preferred_element_type=F32) + jnp.dot(h_lo, rw_hi, preferred_element_type=F32)
              + jnp.dot(h_hi, rw_lo, preferred_element_type=F32))
    lane = lax.broadcasted_iota(I32, logits.shape, 1)
    lane_f = lane.astype(F32)
    logits = jnp.where(lane < N_EXPERTS, logits, NEG_INF)
    v1 = jnp.max(logits, axis=-1, keepdims=True)
    i1 = jnp.min(jnp.where(logits == v1, lane_f, float(ROUTE_LANES)), axis=-1, keepdims=True)
    rest = jnp.where(lane_f == i1, NEG_INF, logits)
    v2 = jnp.max(rest, axis=-1, keepdims=True)
    i2 = jnp.min(jnp.where(rest == v2, lane_f, float(ROUTE_LANES)), axis=-1, keepdims=True)
    e = jnp.exp(v2 - v1)
    g1 = 1.0 / (1.0 + e)
    g2 = e / (1.0 + e)
    route = jnp.where(lane == 0, i1, 0.0)
    route = jnp.where(lane == 1, i2, route)
    route = jnp.where(lane == 2, g1, route)
    route_ref[...] = jnp.where(lane == 3, g2, route)


def _odd_out(x2, c2, d2, wc, wd, gain, router_pad, *, tm=512):
    t, d = x2.shape
    row = lambda w: pl.BlockSpec((tm, w), lambda i: (i, 0))
    full = lambda a: pl.BlockSpec(a.shape, lambda i: (0, 0))
    return pl.pallas_call(
        _odd_out_body,
        out_shape=[jax.ShapeDtypeStruct((t, d), F32), jax.ShapeDtypeStruct((t * ROW_SUBLANES, ROW_LANES), F32),
                   jax.ShapeDtypeStruct((t, ROUTE_LANES), F32)],
        grid=(t // tm,),
        in_specs=[row(d), row(MOBA_WIDTH), row(CONV_WIDTH), full(wc), full(wd),
                  pl.BlockSpec((1, d), lambda i: (0, 0)), full(router_pad)],
        out_specs=[row(d), pl.BlockSpec((tm * ROW_SUBLANES, ROW_LANES), lambda i: (i, 0)), row(ROUTE_LANES)],
        compiler_params=_params("parallel"),
        name="odd_out_proj_router",
    )(x2, c2, d2, wc, wd, gain.reshape(1, d), router_pad)


def _token_rows(token, count=1):
    return pl.ds(pl.multiple_of(token * ROW_SUBLANES, ROW_SUBLANES), count * ROW_SUBLANES)


def _row_copy(src_ref, src_row, dst_ref, dst_row, sem):
    return pltpu.make_async_copy(src_ref.at[_token_rows(src_row)], dst_ref.at[_token_rows(dst_row)], sem)


def _dispatch_body(pos_ref, pad_ref, h_ref, xs_ref, zeros_s, sem, pad_sem):
    td = h_ref.shape[0] // ROW_SUBLANES

    @pl.when(pl.program_id(0) == 0)
    def _():
        zeros_s[...] = jnp.zeros_like(zeros_s)
        tm = zeros_s.shape[0] // ROW_SUBLANES
        n_tiles = xs_ref.shape[0] // zeros_s.shape[0]
        n_live = pad_ref[N_EXPERTS]

        def fills():
            for e in range(N_EXPERTS):
                start = jnp.maximum(pad_ref[e], 0)
                yield pad_ref[e] >= 0, pltpu.make_async_copy(zeros_s, xs_ref.at[_token_rows(start, tm)], pad_sem)
            for c in range(N_EXPERTS):
                tile = n_tiles - 1 - c
                yield tile >= n_live, pltpu.make_async_copy(zeros_s, xs_ref.at[_token_rows(tile * tm, tm)], pad_sem)

        for has_tile, cp in fills():
            pl.when(has_tile)(cp.start)
        for has_tile, cp in fills():
            pl.when(has_tile)(cp.wait)

    def copies(r):
        return [_row_copy(h_ref, r, xs_ref, pos_ref[0, 0, k * td + r], sem) for k in range(2)]

    def start(r, c):
        for k, cp in enumerate(copies(r)):
            cp.start(priority=k)
        return c

    lax.fori_loop(0, td, start, 0, unroll=DMA_UNROLL)
    for _ in range(2):
        pltpu.make_async_copy(h_ref, xs_ref.at[_token_rows(0, td)], sem).wait()


def _dispatch(h3, pos3, pad_tiles, n_rows, *, td, tm):
    t = h3.shape[0] // ROW_SUBLANES
    return pl.pallas_call(
        _dispatch_body,
        out_shape=jax.ShapeDtypeStruct((n_rows * ROW_SUBLANES, ROW_LANES), F32),
        grid=(t // td,),
        in_specs=[
            pl.BlockSpec((1, 1, 2 * td), lambda i: (i, 0, 0), memory_space=pltpu.SMEM),
            pl.BlockSpec(memory_space=pltpu.SMEM),
            pl.BlockSpec((td * ROW_SUBLANES, ROW_LANES), lambda i: (i, 0)),
        ],
        out_specs=pl.BlockSpec(memory_space=pl.ANY),
        scratch_shapes=[pltpu.VMEM((tm * ROW_SUBLANES, ROW_LANES), F32), pltpu.SemaphoreType.DMA(()),
                        pltpu.SemaphoreType.DMA(())],
        compiler_params=_params("arbitrary"),
        name="expert_dispatch",
    )(pos3, pad_tiles, h3)


def _combine_body(pos_ref, x_ref, route_ref, y_ref, o_ref, buf, sems):
    i = pl.program_id(0)
    n = pl.num_programs(0)
    td = x_ref.shape[0]
    t = n * td

    def issue(tile, slot):
        def start(r, c):
            for k in range(2):
                _row_copy(y_ref, pos_ref[k * t + tile * td + r], buf.at[slot, k], r,
                          sems.at[slot]).start(priority=k)
            return c

        lax.fori_loop(0, td, start, 0, unroll=DMA_UNROLL)

    @pl.when(i == 0)
    def _():
        issue(0, 0)

    slot = i % 2

    @pl.when(i + 1 < n)
    def _():
        issue(i + 1, 1 - slot)

    for k in range(2):
        pltpu.make_async_copy(y_ref.at[_token_rows(0, td)], buf.at[slot, k], sems.at[slot]).wait()
    route = route_ref[...]
    o_ref[...] = (x_ref[...] + route[:, 2:3] * _load_token_tiles(buf.at[slot, 0])
                  + route[:, 3:4] * _load_token_tiles(buf.at[slot, 1]))


def _combine(x2, route, ys, pos, *, td):
    t, d = x2.shape
    return pl.pallas_call(
        _combine_body,
        out_shape=jax.ShapeDtypeStruct((t, d), F32),
        grid_spec=pltpu.PrefetchScalarGridSpec(
            num_scalar_prefetch=1,
            grid=(t // td,),
            in_specs=[
                pl.BlockSpec((td, d), lambda i, pos: (i, 0)),
                pl.BlockSpec((td, ROUTE_LANES), lambda i, pos: (i, 0)),
                pl.BlockSpec(memory_space=pl.ANY),
            ],
            out_specs=pl.BlockSpec((td, d), lambda i, pos: (i, 0)),
            scratch_shapes=[pltpu.VMEM((2, 2, td * ROW_SUBLANES, ROW_LANES), F32), pltpu.SemaphoreType.DMA((2,))],
        ),
        compiler_params=_params("arbitrary"),
        name="expert_combine",
    )(pos, x2, route, ys)


def _route_plan(route, *, tm, td):
    t = route.shape[0]
    experts = jnp.concatenate([route[:, 0], route[:, 1]]).astype(I32)
    onehot = (experts[:, None] == jnp.arange(N_EXPERTS, dtype=I32)[None, :]).astype(I32)
    running = jnp.cumsum(onehot, axis=0)
    counts = running[-1]
    padded = ((counts + tm - 1) // tm) * tm
    ends = jnp.cumsum(padded)
    starts = ends - padded
    pos = jnp.sum(onehot * (starts[None, :] + running - 1), axis=1)
    n_tiles = (2 * t) // tm + N_EXPERTS
    tile_lo = jnp.arange(n_tiles, dtype=I32) * tm
    tile_expert = jnp.sum((tile_lo[:, None] >= ends[None, :]).astype(I32), axis=1)
    n_live = (ends[-1] // tm).astype(I32)
    last = tile_expert[jnp.maximum(n_live - 1, 0)]
    live = jnp.arange(n_tiles) < n_live
    tile_expert = jnp.where(live, tile_expert, last)
    prev = jnp.concatenate([jnp.full((1,), -1, I32), tile_expert[:-1]])
    tile_first = (live & (tile_expert != prev)).astype(I32)
    pos3 = pos.reshape(2, t // td, td).transpose(1, 0, 2).reshape(t // td, 1, 2 * td)
    pad_tiles = jnp.concatenate([jnp.where(padded > 0, ends - tm, -1), n_live.reshape(1)]).astype(I32)
    return pos, pos3, pad_tiles, tile_expert, tile_first, n_live.reshape(1), n_tiles * tm


def _rope_tables(seq):
    inv = 1.0 / (ROPE_THETA ** (jnp.arange(0, HEAD_DIM, 2, dtype=F32) / HEAD_DIM))
    ang = jnp.arange(seq, dtype=F32)[:, None] * inv[None, :]
    ang = jnp.concatenate([ang, ang], axis=-1)
    tile = lambda a: jnp.tile(a, (1, HEADS_PER_TILE))
    return tile(jnp.cos(ang)), tile(jnp.sin(ang))


def _even_layer(x2, b, s, cos_t, sin_t, norm_mix, norm_ffn, w_in, v_gain, ws, bs, gq, gk, w_out, wg, wu, wd):
    p = _norm_proj(x2, norm_mix, w_in.astype(BF16))
    p3 = p.reshape(b, s, p.shape[1])
    a = _gmlp(p3, v_gain, ws, bs)
    q0 = 2 * GMLP_WIDTH // LANE_TILE
    n_groups = len(DIL_CONFIGS)
    accs = []
    for group, (window, dil) in enumerate(DIL_CONFIGS):
        assert window // dil == BAND
        qg, kg, vtg = _qkv_prep(p3, cos_t, sin_t, gq, gk, q_col=q0 + group,
                                k_col=q0 + n_groups + group, v_col=q0 + 2 * n_groups + group, n_ht=1, dil=dil,
                                tt=max(4, dil) * BAND, vblock=BAND, head_rows=BAND_ACC_ROWS, with_kmean=False)
        accs.append(_band_attn(qg, kg, vtg))
    w_out = w_out.astype(BF16)
    x3 = _even_out(x2.reshape(b, s, -1), a, accs, w_out[:GMLP_WIDTH], w_out[GMLP_WIDTH:])
    return _ffn(x3.reshape(b * s, -1), norm_ffn, wg.astype(BF16), wu.astype(BF16), wd.astype(BF16))


def _odd_layer(x2, b, s, cos_t, sin_t, norm_mix, norm_ffn, w_in, gq, gk, conv_w, w_out, router_w, wg, wu, wd,
               *, layer, tm=512, td=512):
    p = _norm_proj(x2, norm_mix, w_in.astype(BF16))
    p3 = p.reshape(b, s, p.shape[1])
    tiles = MOBA_WIDTH // LANE_TILE
    qp, kp, vtp, kmean = _qkv_prep(p3, cos_t, sin_t, gq, gk, q_col=0, k_col=tiles,
                                   v_col=2 * tiles, n_ht=tiles, dil=1, tt=8 * MOBA_BLOCK, vblock=MOBA_BLOCK,
                                   head_rows=MOBA_ACC_ROWS, with_kmean=True)
    c = _moba(qp, kp, vtp, kmean)
    dconv = _short_conv(p3, conv_w, col0=3 * MOBA_WIDTH // CONV_WIDTH)
    w_out = w_out.astype(BF16)
    router_pad = jnp.pad(router_w, ((0, 0), (0, ROUTE_LANES - N_EXPERTS)))
    x2, h2, route = _odd_out(x2, c.reshape(b * s, MOBA_WIDTH), dconv.reshape(b * s, CONV_WIDTH),
                             w_out[:MOBA_WIDTH], w_out[MOBA_WIDTH:], norm_ffn, router_pad)
    pos, pos3, pad_tiles, tile_expert, tile_first, n_live, n_rows = _route_plan(route, tm=tm, td=td)
    xs = _dispatch(h2, pos3, pad_tiles, n_rows, td=td, tm=tm)
    ys = _expert_ffn(xs, tile_expert, tile_first, n_live, wg, wu, wd, layer=layer, tm=tm)
    return _combine(x2, route, ys, pos, td=td)


def kernel(x, norm_mix, norm_ffn, w_in_ab, gmlp_v_gain, gmlp_ws, gmlp_bs, dil_q_gain, dil_k_gain, w_out_ab,
           ffn_w_gate, ffn_w_up, ffn_w_down, w_in_cd, moba_q_gain, moba_k_gain, conv_w, w_out_cd, router_w,
           moe_w_gate, moe_w_up, moe_w_down):
    b, s, d = x.shape
    depth = norm_mix.shape[0]
    cos_t, sin_t = _rope_tables(s)
    x2 = x.reshape(b * s, d)
    for layer in range(depth):
        i = layer // 2
        if layer % 2 == 0:
            x2 = _even_layer(x2, b, s, cos_t, sin_t, norm_mix[layer], norm_ffn[layer], w_in_ab[i], gmlp_v_gain[i],
                             gmlp_ws[i], gmlp_bs[i], dil_q_gain[i], dil_k_gain[i], w_out_ab[i], ffn_w_gate[i],
                             ffn_w_up[i], ffn_w_down[i])
        else:
            x2 = _odd_layer(x2, b, s, cos_t, sin_t, norm_mix[layer], norm_ffn[layer], w_in_cd[i], moba_q_gain[i],
                            moba_k_gain[i], conv_w[i], w_out_cd[i], router_w[i], moe_w_gate, moe_w_up,
                            moe_w_down, layer=i)
    return x2.reshape(b, s, d)
```

```python
import functools

import jax
import jax.numpy as jnp
from jax import lax
from jax.experimental import pallas as pl
from jax.experimental.pallas import tpu as pltpu

F32 = jnp.float32
BF16 = jnp.bfloat16
I32 = jnp.int32

D_MODEL = 1024
HEAD_DIM = 64
ROPE_THETA = 10000.0
NORM_EPS = 1e-6
LANE_TILE = 256
HEADS_PER_TILE = LANE_TILE // HEAD_DIM
GMLP_WIDTH = 256
GMLP_CHUNK = 128
DIL_CONFIGS = ((128, 1), (512, 4), (2048, 16))
DIL_WIDTH = 768
BAND = 128
BAND_ACC_ROWS = 128
BAND_L_ROW = HEAD_DIM
BAND_M_ROW = HEAD_DIM + 1
BAND_LOOKAHEAD = 5
STRIDE_LANES = 128
MOBA_WIDTH = 768
MOBA_BLOCK = 256
MOBA_TOPK = 3
MOBA_ACC_ROWS = HEAD_DIM + 16
LOG2_E = 1.4426950408889634
MOBA_M_INIT = -1e30
CONV_WIDTH = 256
N_EXPERTS = 8
ROUTE_LANES = 128
NEG_INF = float("-inf")
DMA_UNROLL = 8
ROW_LANES = 128
ROW_SUBLANES = 8

_NT = (((1,), (1,)), ((), ()))


def _params(*sem):
    return pltpu.CompilerParams(dimension_semantics=tuple(sem), vmem_limit_bytes=56 * 1024 * 1024)


def _lane_ids():
    return lax.broadcasted_iota(I32, (1, LANE_TILE), 1)


def _head_block_diag():
    r = lax.broadcasted_iota(I32, (LANE_TILE, LANE_TILE), 0) // HEAD_DIM
    c = lax.broadcasted_iota(I32, (LANE_TILE, LANE_TILE), 1) // HEAD_DIM
    return jnp.where(r == c, 1.0, 0.0).astype(BF16)


def _head_mean_sq(x, bd):
    s = x * x
    hi = s.astype(BF16)
    lo = (s - hi.astype(F32)).astype(BF16)
    tot = jnp.dot(hi, bd, preferred_element_type=F32) + jnp.dot(lo, bd, preferred_element_type=F32)
    return tot * (1.0 / HEAD_DIM)


def _head_norm(x, gain, bd):
    return x * lax.rsqrt(_head_mean_sq(x, bd) + NORM_EPS) * gain


def _row_norm(x, gain):
    return x * lax.rsqrt(jnp.mean(x * x, axis=-1, keepdims=True) + NORM_EPS) * gain


def _load_token_tiles(ref):
    n = ref.shape[0] // ROW_SUBLANES
    return jnp.concatenate([ref[pl.ds(c, n, stride=ROW_SUBLANES), :] for c in range(ROW_SUBLANES)], axis=1)


def _store_token_tiles(ref, val):
    n = ref.shape[0] // ROW_SUBLANES
    for c in range(ROW_SUBLANES):
        ref[pl.ds(c, n, stride=ROW_SUBLANES), :] = val[:, c * ROW_LANES:(c + 1) * ROW_LANES]


def _norm_proj_body(x_ref, g_ref, w_ref, o_ref, *, tn):
    hb = _row_norm(x_ref[...], g_ref[...]).astype(BF16)
    for c in range(o_ref.shape[-1] // tn):
        sl = slice(c * tn, (c + 1) * tn)
        o_ref[:, sl] = jnp.dot(hb, w_ref[:, sl], preferred_element_type=F32).astype(o_ref.dtype)


def _norm_proj(x2, gain, w_bf16, *, tm=1024, tn=256):
    t, d = x2.shape
    n = w_bf16.shape[1]
    return pl.pallas_call(
        functools.partial(_norm_proj_body, tn=tn),
        out_shape=jax.ShapeDtypeStruct((t, n), BF16),
        grid=(t // tm,),
        in_specs=[
            pl.BlockSpec((tm, d), lambda i: (i, 0)),
            pl.BlockSpec((1, d), lambda i: (0, 0)),
            pl.BlockSpec((d, n), lambda i: (0, 0)),
        ],
        out_specs=pl.BlockSpec((tm, n), lambda i: (i, 0)),
        compiler_params=_params("parallel"),
        name="norm_proj",
    )(x2, gain.reshape(1, d), w_bf16)


def _gmlp_body(u_ref, v_ref, gain_ref, ws_ref, bias_ref, o_ref):
    lane = _lane_ids()
    bd = _head_block_diag()
    r = lax.broadcasted_iota(I32, (GMLP_CHUNK, GMLP_CHUNK), 0)
    c = lax.broadcasted_iota(I32, (GMLP_CHUNK, GMLP_CHUNK), 1)
    w_tril = [jnp.where(r >= c, ws_ref[g], 0.0).astype(BF16) for g in range(HEADS_PER_TILE)]
    for ch in range(o_ref.shape[0] // GMLP_CHUNK):
        sl = slice(ch * GMLP_CHUNK, (ch + 1) * GMLP_CHUNK)
        u = jax.nn.gelu(u_ref[sl, :].astype(F32))
        v = jax.nn.gelu(v_ref[sl, :].astype(F32))
        vn = _head_norm(v, gain_ref[...], bd).astype(BF16)
        y = bias_ref[...]
        for g in range(HEADS_PER_TILE):
            yg = jnp.dot(w_tril[g], vn, preferred_element_type=F32)
            y = y + jnp.where(lane // HEAD_DIM == g, yg, 0.0)
        o_ref[sl, :] = (u * y).astype(o_ref.dtype)


def _gmlp(p3, v_gain, ws, bs, *, tc=512):
    b, s, _ = p3.shape
    groups, chunk = bs.shape
    bias = jnp.repeat(bs.T, HEAD_DIM, axis=1)
    return pl.pallas_call(
        _gmlp_body,
        out_shape=jax.ShapeDtypeStruct((b, s, GMLP_WIDTH), BF16),
        grid=(b, s // tc),
        in_specs=[
            pl.BlockSpec((None, tc, LANE_TILE), lambda bi, i: (bi, i, 0)),
            pl.BlockSpec((None, tc, LANE_TILE), lambda bi, i: (bi, i, 1)),
            pl.BlockSpec((1, GMLP_WIDTH), lambda bi, i: (0, 0)),
            pl.BlockSpec((groups, chunk, chunk), lambda bi, i: (0, 0, 0)),
            pl.BlockSpec((chunk, GMLP_WIDTH), lambda bi, i: (0, 0)),
        ],
        out_specs=pl.BlockSpec((None, tc, GMLP_WIDTH), lambda bi, i: (bi, i, 0)),
        compiler_params=_params("parallel", "parallel"),
        name="gmlp_gate",
    )(p3, p3, v_gain.reshape(1, GMLP_WIDTH), ws, bias)


def _residues(val, scr, dil):
    if dil == 1:
        return [val]
    rows = val.shape[0] // dil
    slabs = LANE_TILE // STRIDE_LANES
    for c in range(slabs):
        scr[c] = val[:, c * STRIDE_LANES:(c + 1) * STRIDE_LANES]
    return [jnp.concatenate([scr[c, pl.ds(r, rows, stride=dil), :] for c in range(slabs)], axis=1)
            for r in range(dil)]


def _norm_rope(x, tab_a, tab_b, bd, lane):
    sq = (x * x).astype(BF16)
    xn = x * lax.rsqrt(jnp.dot(sq, bd, preferred_element_type=F32) * (1.0 / HEAD_DIM) + NORM_EPS)
    half = HEAD_DIM // 2
    partner = jnp.where((lane % HEAD_DIM) < half, pltpu.roll(xn, LANE_TILE - half, 1), pltpu.roll(xn, half, 1))
    return xn * tab_a + partner * tab_b


def _qkv_prep_body(q_ref, k_ref, v_ref, qa_ref, qb_ref, ka_ref, kb_ref, *rest, dil, with_kmean):
    n_out = 4 if with_kmean else 3
    qo_ref, ko_ref, vt_ref = rest[:3]
    scr = list(rest[n_out:]) + [None] * 3
    lane = _lane_ids()
    bd = _head_block_diag()
    qn = _norm_rope(q_ref[...].astype(F32), qa_ref[...], qb_ref[...], bd, lane)
    kn = _norm_rope(k_ref[...].astype(F32), ka_ref[...], kb_ref[...], bd, lane)
    if with_kmean:
        km_ref = rest[3]
        km_ref[...] = jnp.mean(kn.reshape(km_ref.shape[0], MOBA_BLOCK, LANE_TILE), axis=1)
    for r, piece in enumerate(_residues(qn, scr[0], dil)):
        qo_ref[r] = piece.astype(BF16)
    for r, piece in enumerate(_residues(kn, scr[1], dil)):
        ko_ref[r] = piece.astype(BF16)
    n_blocks, acc_w, vblock = vt_ref.shape[1:]
    head_rows = acc_w // HEADS_PER_TILE
    ones = jnp.ones((head_rows - HEAD_DIM, vblock), BF16)
    for r, piece in enumerate(_residues(v_ref[...].astype(F32), scr[2], dil)):
        for c in range(n_blocks):
            vt = piece[c * vblock:(c + 1) * vblock, :].T.astype(BF16)
            for h in range(HEADS_PER_TILE):
                vt_ref[r, c, h * head_rows:h * head_rows + HEAD_DIM, :] = vt[h * HEAD_DIM:(h + 1) * HEAD_DIM, :]
                vt_ref[r, c, h * head_rows + HEAD_DIM:(h + 1) * head_rows, :] = ones


def _rope_gain_tables(cos_t, sin_t, gain, scale):
    half = HEAD_DIM // 2
    swapped = jnp.concatenate([gain[half:], gain[:half]])
    sign = jnp.concatenate([-jnp.ones((half,), F32), jnp.ones((half,), F32)])
    tile = lambda v: jnp.tile(v, HEADS_PER_TILE)[None, :]
    return cos_t * (tile(gain) * scale), sin_t * (tile(sign * swapped) * scale)


def _qkv_prep(p3, cos_t, sin_t, gq, gk, *, q_col, k_col, v_col, n_ht, dil, tt, vblock, head_rows, with_kmean):
    b, s, _ = p3.shape
    length = s // dil
    tr = tt // dil
    acc_w = HEADS_PER_TILE * head_rows

    def col(c0):
        return pl.BlockSpec((None, tt, LANE_TILE), lambda i, bi, ht: (bi, i, c0 + ht))

    tab = pl.BlockSpec((tt, LANE_TILE), lambda i, bi, ht: (i, 0))
    tables = _rope_gain_tables(cos_t, sin_t, gq, HEAD_DIM ** -0.5 * LOG2_E) + _rope_gain_tables(cos_t, sin_t, gk, 1.0)
    qk_shape = jax.ShapeDtypeStruct((b, n_ht, dil, length, LANE_TILE), BF16)
    qk_spec = pl.BlockSpec((None, None, dil, tr, LANE_TILE), lambda i, bi, ht: (bi, ht, 0, i, 0))
    out_shape = [qk_shape, qk_shape,
                 jax.ShapeDtypeStruct((b, n_ht, dil, length // vblock, acc_w, vblock), BF16)]
    out_specs = [qk_spec, qk_spec,
                 pl.BlockSpec((None, None, dil, tr // vblock, acc_w, vblock), lambda i, bi, ht: (bi, ht, 0, i, 0, 0))]
    if with_kmean:
        out_shape.append(jax.ShapeDtypeStruct((b, n_ht, s // MOBA_BLOCK, LANE_TILE), F32))
        out_specs.append(pl.BlockSpec((None, None, tt // MOBA_BLOCK, LANE_TILE), lambda i, bi, ht: (bi, ht, i, 0)))
    scratch = [pltpu.VMEM((LANE_TILE // STRIDE_LANES, tt, STRIDE_LANES), F32)] * (3 if dil > 1 else 0)
    return pl.pallas_call(
        functools.partial(_qkv_prep_body, dil=dil, with_kmean=with_kmean),
        out_shape=out_shape,
        grid=(s // tt, b, n_ht),
        in_specs=[col(q_col), col(k_col), col(v_col), tab, tab, tab, tab],
        out_specs=out_specs,
        scratch_shapes=scratch,
        compiler_params=_params("parallel", "parallel", "parallel"),
        name=f"qkv_prep_d{dil}",
    )(p3, p3, p3, *tables)


def _band_attn_body(q_ref, k_ref, kp_ref, vt_ref, vtp_ref, o_ref):
    i = pl.program_id(2)
    lb = q_ref.shape[0]
    lane = _lane_ids()
    kk = lax.broadcasted_iota(I32, (2 * BAND, BAND), 0)
    qq = lax.broadcasted_iota(I32, (2 * BAND, BAND), 1) + BAND
    in_band = (qq - kk >= 0) & (qq - kk <= BAND)
    first_ok = in_band & ((kk >= BAND) | (i > 0))
    acc_row = lax.broadcasted_iota(I32, (BAND_ACC_ROWS, BAND), 0)
    k_all = jnp.concatenate([kp_ref[...], k_ref[...]], axis=0)
    chains = [(j, h) for j in range(lb // BAND) for h in range(HEADS_PER_TILE)]

    def scores(j, h):
        qj = q_ref[j * BAND:(j + 1) * BAND, :]
        qh = jnp.where(lane // HEAD_DIM == h, qj, jnp.zeros_like(qj))
        return lax.dot_general(k_all[j * BAND:(j + 2) * BAND], qh, _NT, preferred_element_type=F32)

    staged = [scores(*c) for c in chains[:BAND_LOOKAHEAD]]
    tiles = []
    for n, (j, h) in enumerate(chains):
        if n + BAND_LOOKAHEAD < len(chains):
            staged.append(scores(*chains[n + BAND_LOOKAHEAD]))
        rows = slice(h * BAND_ACC_ROWS, (h + 1) * BAND_ACC_ROWS)
        s = jnp.where(first_ok if j == 0 else in_band, staged[n], NEG_INF)
        m = jnp.max(s, axis=0, keepdims=True)
        p = jnp.exp2(s - m).astype(BF16)
        v_prev = vtp_ref[0, rows, :] if j == 0 else vt_ref[j - 1, rows, :]
        vw = jnp.concatenate([v_prev, vt_ref[j, rows, :]], axis=1)
        acc = jnp.dot(vw, p, preferred_element_type=F32)
        tiles.append(jnp.where(acc_row == BAND_M_ROW, m, acc))
        if h == HEADS_PER_TILE - 1:
            o_ref[j * BAND:(j + 1) * BAND, :] = jnp.concatenate(tiles, axis=0).T
            tiles = []


def _band_attn(qg, kg, vtg):
    b, _, dil, length, _ = qg.shape
    lb = min(length, 512)
    sub = lb // BAND
    acc_w = HEADS_PER_TILE * BAND_ACC_ROWS
    before = lambda i: jnp.maximum(i * sub - 1, 0)
    cur = pl.BlockSpec((None, None, None, lb, LANE_TILE), lambda bi, r, i: (bi, 0, r, i, 0))
    prev = pl.BlockSpec((None, None, None, BAND, LANE_TILE), lambda bi, r, i: (bi, 0, r, before(i), 0))
    vt_cur = pl.BlockSpec((None, None, None, sub, acc_w, BAND), lambda bi, r, i: (bi, 0, r, i, 0, 0))
    vt_prev = pl.BlockSpec((None, None, None, 1, acc_w, BAND), lambda bi, r, i: (bi, 0, r, before(i), 0, 0))
    return pl.pallas_call(
        _band_attn_body,
        out_shape=jax.ShapeDtypeStruct((b, dil, length, acc_w), F32),
        grid=(b, dil, length // lb),
        in_specs=[cur, cur, prev, vt_cur, vt_prev],
        out_specs=pl.BlockSpec((None, None, lb, acc_w), lambda bi, r, i: (bi, r, i, 0)),
        compiler_params=_params("parallel", "parallel", "arbitrary"),
        name=f"band_attn_d{dil}",
    )(qg, kg, kg, vtg, vtg)


def _even_out_body(x_ref, a_ref, o0_ref, o1_ref, o2_ref, wa_ref, wb_ref, out_ref, *scratch):
    tm = x_ref.shape[0]

    def natural_order(o_ref, scr):
        dil = o_ref.shape[0]
        if dil == 1:
            return [o_ref[0, :, h * BAND_ACC_ROWS:(h + 1) * BAND_ACC_ROWS] for h in range(HEADS_PER_TILE)]
        for r in range(dil):
            for h in range(HEADS_PER_TILE):
                scr[h, pl.ds(r, tm // dil, stride=dil), :] = o_ref[r, :, h * BAND_ACC_ROWS:(h + 1) * BAND_ACC_ROWS]
        return [scr[h] for h in range(HEADS_PER_TILE)]

    o_refs = (o0_ref, o1_ref, o2_ref)
    spare = list(scratch)
    groups = [natural_order(o, spare.pop(0) if o.shape[0] > 1 else None) for o in o_refs]
    acc = x_ref[...] + jnp.dot(a_ref[...], wa_ref[...], preferred_element_type=F32)
    for h in range(HEADS_PER_TILE):
        tiles = [g[h] for g in groups]
        tops = [t[:, BAND_M_ROW:BAND_M_ROW + 1] for t in tiles]
        top = jnp.maximum(jnp.maximum(tops[0], tops[1]), tops[2])
        ws = [jnp.exp2(m - top) for m in tops]
        num = sum(w * t[:, :HEAD_DIM] for w, t in zip(ws, tiles))
        den = sum(w * t[:, BAND_L_ROW:BAND_L_ROW + 1] for w, t in zip(ws, tiles))
        merged = (num / den).astype(BF16)
        acc = acc + jnp.dot(merged, wb_ref[h * HEAD_DIM:(h + 1) * HEAD_DIM, :], preferred_element_type=F32)
    out_ref[...] = acc


def _even_out(x3, a3, accs, wa, wb, *, tm=512):
    b, s, d = x3.shape
    acc_w = HEADS_PER_TILE * BAND_ACC_ROWS
    full = lambda a: pl.BlockSpec(a.shape, lambda bi, i: (0, 0))

    def acc_spec(a):
        dil = a.shape[1]
        return pl.BlockSpec((None, dil, tm // dil, acc_w), lambda bi, i: (bi, 0, i, 0))

    scratch = [pltpu.VMEM((HEADS_PER_TILE, tm, BAND_ACC_ROWS), F32) for a in accs if a.shape[1] > 1]
    return pl.pallas_call(
        _even_out_body,
        out_shape=jax.ShapeDtypeStruct((b, s, d), F32),
        grid=(b, s // tm),
        in_specs=[pl.BlockSpec((None, tm, d), lambda bi, i: (bi, i, 0)),
                  pl.BlockSpec((None, tm, GMLP_WIDTH), lambda bi, i: (bi, i, 0))]
        + [acc_spec(a) for a in accs] + [full(wa), full(wb)],
        out_specs=pl.BlockSpec((None, tm, d), lambda bi, i: (bi, i, 0)),
        scratch_shapes=scratch,
        compiler_params=_params("parallel", "parallel"),
        name="even_out_proj",
    )(x3, a3, *accs, wa, wb)


def _swiglu_step(h_bf16, wg, wu, wd):
    g = jnp.dot(h_bf16, wg, preferred_element_type=F32)
    u = jnp.dot(h_bf16, wu, preferred_element_type=F32)
    a = (g * jax.nn.sigmoid(g) * u).astype(BF16)
    return jnp.dot(a, wd, preferred_element_type=F32)


def _ffn_body(x_ref, g_ref, wg_ref, wu_ref, wd_ref, o_ref, *, tf):
    x = x_ref[...]
    h = _row_norm(x, g_ref[...]).astype(BF16)
    acc = x
    for c in range(wg_ref.shape[1] // tf):
        sl = slice(c * tf, (c + 1) * tf)
        acc = acc + _swiglu_step(h, wg_ref[:, sl], wu_ref[:, sl], wd_ref[sl, :])
    o_ref[...] = acc


def _ffn(x2, gain, wg, wu, wd, *, tm=512, tf=256):
    t, d = x2.shape
    resident = lambda a: pl.BlockSpec(a.shape, lambda i: (0, 0), pipeline_mode=pl.Buffered(1))
    return pl.pallas_call(
        functools.partial(_ffn_body, tf=tf),
        out_shape=jax.ShapeDtypeStruct((t, d), F32),
        grid=(t // tm,),
        in_specs=[
            pl.BlockSpec((tm, d), lambda i: (i, 0)),
            pl.BlockSpec((1, d), lambda i: (0, 0)),
            resident(wg), resident(wu), resident(wd),
        ],
        out_specs=pl.BlockSpec((tm, d), lambda i: (i, 0)),
        compiler_params=_params("parallel"),
        name="ffn_swiglu",
    )(x2, gain.reshape(1, d), wg, wu, wd)


def _expert_body(te_ref, first_ref, nv_ref, x_ref, wg_hbm, wu_hbm, wd_hbm, o_ref,
                 cg_s, cu_s, cd_s, sg_s, su_s, sd_s, sems, *, layer, tf):
    i = pl.program_id(0)
    nf = cg_s.shape[0]
    e = te_ref[i]

    def chunk_copies(f, slot):
        cols = pl.ds(f * tf, tf)
        return (pltpu.make_async_copy(wg_hbm.at[layer, e, :, cols], sg_s.at[slot], sems.at[0, slot]),
                pltpu.make_async_copy(wu_hbm.at[layer, e, :, cols], su_s.at[slot], sems.at[1, slot]),
                pltpu.make_async_copy(wd_hbm.at[layer, e, cols, :], sd_s.at[slot], sems.at[2, slot]))

    def compute(load_chunk):
        h = _load_token_tiles(x_ref).astype(BF16)
        acc = jnp.zeros(h.shape, F32)
        for f in range(nf):
            load_chunk(f)
            acc = acc + _swiglu_step(h, cg_s[f], cu_s[f], cd_s[f])
        _store_token_tiles(o_ref, acc)

    @pl.when(first_ref[i] == 1)
    def _():
        def load_chunk(f):
            if f + 1 < nf:
                for cp in chunk_copies(f + 1, (f + 1) % 2):
                    cp.start()
            for cp in chunk_copies(f, f % 2):
                cp.wait()
            cg_s[f] = sg_s[f % 2].astype(BF16)
            cu_s[f] = su_s[f % 2].astype(BF16)
            cd_s[f] = sd_s[f % 2].astype(BF16)

        for cp in chunk_copies(0, 0):
            cp.start()
        compute(load_chunk)

    @pl.when((first_ref[i] == 0) & (i < nv_ref[0]))
    def _():
        compute(lambda f: None)

    @pl.when(i >= nv_ref[0])
    def _():
        o_ref[...] = jnp.zeros_like(o_ref)


def _expert_ffn(xs, tile_expert, tile_first, n_live, wg, wu, wd, *, layer, tm, tf=512):
    p = xs.shape[0] // ROW_SUBLANES
    d, ff = wg.shape[-2:]
    nf = ff // tf
    rows = pl.BlockSpec((tm * ROW_SUBLANES, ROW_LANES), lambda i, te, first, nv: (i, 0))
    return pl.pallas_call(
        functools.partial(_expert_body, layer=layer, tf=tf),
        out_shape=jax.ShapeDtypeStruct(xs.shape, F32),
        grid_spec=pltpu.PrefetchScalarGridSpec(
            num_scalar_prefetch=3,
            grid=(p // tm,),
            in_specs=[
                rows,
                pl.BlockSpec(memory_space=pl.ANY),
                pl.BlockSpec(memory_space=pl.ANY),
                pl.BlockSpec(memory_space=pl.ANY),
            ],
            out_specs=rows,
            scratch_shapes=[
                pltpu.VMEM((nf, d, tf), BF16),
                pltpu.VMEM((nf, d, tf), BF16),
                pltpu.VMEM((nf, tf, d), BF16),
                pltpu.VMEM((2, d, tf), F32),
                pltpu.VMEM((2, d, tf), F32),
                pltpu.VMEM((2, tf, d), F32),
                pltpu.SemaphoreType.DMA((3, 2)),
            ],
        ),
        compiler_params=_params("arbitrary"),
        name="expert_swiglu",
    )(tile_expert, tile_first, n_live, xs, wg, wu, wd)


def _moba_body(q_ref, k_ref, vt_ref, km_ref, o_ref, sel_s, qh_s, sa_s, sb_s, m_s, out_s):
    i = pl.program_id(2)
    nblk = km_ref.shape[0]
    lane = _lane_ids()
    qn = q_ref[...]
    blk = lax.broadcasted_iota(I32, (nblk, MOBA_BLOCK), 0).astype(F32)
    first_blk = i.astype(F32)
    km = km_ref[...]
    krow = lax.broadcasted_iota(I32, (MOBA_BLOCK, MOBA_BLOCK), 0)
    qcol = lax.broadcasted_iota(I32, (MOBA_BLOCK, MOBA_BLOCK), 1)
    causal = krow <= qcol

    km_hi = km.astype(BF16)
    km_lo = (km - km_hi.astype(F32)).astype(BF16)
    acc_rows = [slice(h * MOBA_ACC_ROWS, (h + 1) * MOBA_ACC_ROWS) for h in range(HEADS_PER_TILE)]

    def keys(j):
        return k_ref[pl.ds(pl.multiple_of(j * MOBA_BLOCK, MOBA_BLOCK), MOBA_BLOCK), :]

    heads = range(HEADS_PER_TILE)
    qh = [jnp.where(lane // HEAD_DIM == h, qn, jnp.zeros_like(qn)) for h in heads]
    for h in heads:
        qh_s[h] = qh[h]
    gates = [lax.dot_general(km_hi, qh[h], _NT, preferred_element_type=F32)
             + lax.dot_general(km_lo, qh[h], _NT, preferred_element_type=F32) for h in heads]
    k_own = keys(i)
    own = [lax.dot_general(k_own, qh[h], _NT, preferred_element_type=F32) for h in heads]
    k_first = keys(0)
    for h in heads:
        sa_s[h] = lax.dot_general(k_first, qh[h], _NT, preferred_element_type=F32)

    for h in heads:
        gate = jnp.where(blk < first_blk, gates[h], NEG_INF)
        sel = jnp.zeros((nblk, MOBA_BLOCK), F32)
        for _ in range(MOBA_TOPK):
            top = jnp.max(gate, axis=0, keepdims=True)
            idx = jnp.min(jnp.where(gate == top, blk, float(nblk)), axis=0, keepdims=True)
            hit = blk == idx
            sel = jnp.where(hit & (top > NEG_INF), 1.0, sel)
            gate = jnp.where(hit, NEG_INF, gate)
        sel_s[h] = sel

    for h in heads:
        s = jnp.where(causal, own[h], NEG_INF)
        m0 = jnp.max(s, axis=0, keepdims=True)
        m_s[h] = m0
        out_s[acc_rows[h], :] = jnp.dot(vt_ref[i, acc_rows[h], :], jnp.exp2(s - m0).astype(BF16),
                                        preferred_element_type=F32)

    def stage(j_next, buf_next, j, buf):
        k_next = keys(jnp.minimum(j_next, nblk - 1))
        jc = jnp.minimum(j, nblk - 1)
        bar = jnp.where(j < i, 0.0, 2.0)
        for h in heads:
            buf_next[h] = lax.dot_general(k_next, qh_s[h], _NT, preferred_element_type=F32)
        for h in heads:
            bias = jnp.where(sel_s[h, pl.ds(jc, 1), :] > bar, 0.0, NEG_INF)
            s = buf[h]
            m = m_s[h]
            m_new = jnp.maximum(m, jnp.max(s, axis=0, keepdims=True) + bias)
            p = jnp.exp2(s + (bias - m_new)).astype(BF16)
            m_s[h] = m_new
            out_s[acc_rows[h], :] = (jnp.exp2(m - m_new) * out_s[acc_rows[h], :]
                                     + jnp.dot(vt_ref[jc, acc_rows[h], :], p, preferred_element_type=F32))

    def step(t, carry):
        stage(2 * t + 1, sb_s, 2 * t, sa_s)
        stage(2 * t + 2, sa_s, 2 * t + 1, sb_s)
        return carry

    lax.fori_loop(0, (i + 1) // 2, step, 0)

    outs = []
    for h in heads:
        acc = out_s[acc_rows[h], :]
        outs.append(acc[:HEAD_DIM, :] / acc[HEAD_DIM:HEAD_DIM + 1, :])
    o_ref[...] = jnp.concatenate(outs, axis=0).T.astype(o_ref.dtype)


def _moba(qp, kp, vtp, kmean):
    b, tiles, _, s, _ = qp.shape
    nblk = s // MOBA_BLOCK
    acc_w = HEADS_PER_TILE * MOBA_ACC_ROWS
    return pl.pallas_call(
        _moba_body,
        out_shape=jax.ShapeDtypeStruct((b, s, MOBA_WIDTH), BF16),
        grid=(b, tiles, nblk),
        in_specs=[
            pl.BlockSpec((None, None, None, MOBA_BLOCK, LANE_TILE), lambda bi, hg, i: (bi, hg, 0, i, 0)),
            pl.BlockSpec((None, None, None, s, LANE_TILE), lambda bi, hg, i: (bi, hg, 0, 0, 0)),
            pl.BlockSpec((None, None, None, nblk, acc_w, MOBA_BLOCK), lambda bi, hg, i: (bi, hg, 0, 0, 0, 0)),
            pl.BlockSpec((None, None, nblk, LANE_TILE), lambda bi, hg, i: (bi, hg, 0, 0)),
        ],
        out_specs=pl.BlockSpec((None, MOBA_BLOCK, LANE_TILE), lambda bi, hg, i: (bi, i, hg)),
        scratch_shapes=[
            pltpu.VMEM((HEADS_PER_TILE, nblk, MOBA_BLOCK), F32),
            pltpu.VMEM((HEADS_PER_TILE, MOBA_BLOCK, LANE_TILE), BF16),
            pltpu.VMEM((HEADS_PER_TILE, MOBA_BLOCK, MOBA_BLOCK), F32),
            pltpu.VMEM((HEADS_PER_TILE, MOBA_BLOCK, MOBA_BLOCK), F32),
            pltpu.VMEM((HEADS_PER_TILE, 1, MOBA_BLOCK), F32),
            pltpu.VMEM((HEADS_PER_TILE * MOBA_ACC_ROWS, MOBA_BLOCK), F32),
        ],
        compiler_params=_params("parallel", "parallel", "arbitrary"),
        name="moba_attn",
    )(qp, kp, vtp, kmean)


def _conv_body(bg_ref, cg_ref, xz_ref, w_ref, o_ref, *, ts):
    halo = 16
    w = w_ref[...]
    for c in range(o_ref.shape[0] // ts):
        lo = c * ts
        if c == 0:
            z = cg_ref[0:ts, :].astype(F32) * xz_ref[0:ts, :].astype(F32)
            zp = jnp.concatenate([jnp.zeros((halo, CONV_WIDTH), F32), z], axis=0)
        else:
            zp = cg_ref[lo - halo:lo + ts, :].astype(F32) * xz_ref[lo - halo:lo + ts, :].astype(F32)
        z1 = pltpu.roll(zp, 1, 0)[halo:]
        z2 = pltpu.roll(zp, 2, 0)[halo:]
        y = w[2:3, :] * zp[halo:] + w[1:2, :] * z1 + w[0:1, :] * z2
        o_ref[lo:lo + ts, :] = (bg_ref[lo:lo + ts, :].astype(F32) * y).astype(o_ref.dtype)


def _short_conv(p3, conv_w, *, col0, ts=512):
    b, s, _ = p3.shape
    taps = conv_w.shape[0]

    def col(c):
        return pl.BlockSpec((None, s, CONV_WIDTH), lambda bi: (bi, 0, col0 + c))

    return pl.pallas_call(
        functools.partial(_conv_body, ts=ts),
        out_shape=jax.ShapeDtypeStruct((b, s, CONV_WIDTH), BF16),
        grid=(b,),
        in_specs=[col(0), col(1), col(2), pl.BlockSpec((taps, CONV_WIDTH), lambda bi: (0, 0))],
        out_specs=pl.BlockSpec((None, s, CONV_WIDTH), lambda bi: (bi, 0, 0)),
        compiler_params=_params("parallel"),
        name="short_conv",
    )(p3, p3, p3, conv_w)


def _odd_out_body(x_ref, c_ref, d_ref, wc_ref, wd_ref, gain_ref, rw_ref, xo_ref, h_ref, route_ref):
    x = x_ref[...] + jnp.dot(c_ref[...], wc_ref[...], preferred_element_type=F32)
    x = x + jnp.dot(d_ref[...], wd_ref[...], preferred_element_type=F32)
    xo_ref[...] = x
    h = _row_norm(x, gain_ref[...])
    _store_token_tiles(h_ref, h)
    rw = rw_ref[...]
    h_hi = h.astype(BF16)
    h_lo = (h - h_hi.astype(F32)).astype(BF16)
    rw_hi = rw.astype(BF16)
    rw_lo = (rw - rw_hi.astype(F32)).astype(BF16)
    logits = (jnp.dot(h_hi, rw_hi, preferred_element_type=F32) + jnp.dot(h_lo, rw_hi, preferred_element_type=F32)
              + jnp.dot(h_hi, rw_lo, preferred_element_type=F32))
    lane = lax.broadcasted_iota(I32, logits.shape, 1)
    lane_f = lane.astype(F32)
    logits = jnp.where(lane < N_EXPERTS, logits, NEG_INF)
    v1 = jnp.max(logits, axis=-1, keepdims=True)
    i1 = jnp.min(jnp.where(logits == v1, lane_f, float(ROUTE_LANES)), axis=-1, keepdims=True)
    rest = jnp.where(lane_f == i1, NEG_INF, logits)
    v2 = jnp.max(rest, axis=-1, keepdims=True)
    i2 = jnp.min(jnp.where(rest == v2, lane_f, float(ROUTE_LANES)), axis=-1, keepdims=True)
    e = jnp.exp(v2 - v1)
    g1 = 1.0 / (1.0 + e)
    g2 = e / (1.0 + e)
    route = jnp.where(lane == 0, i1, 0.0)
    route = jnp.where(lane == 1, i2, route)
    route = jnp.where(lane == 2, g1, route)
    route_ref[...] = jnp.where(lane == 3, g2, route)


def _odd_out(x2, c2, d2, wc, wd, gain, router_pad, *, tm=512):
    t, d = x2.shape
    row = lambda w: pl.BlockSpec((tm, w), lambda i: (i, 0))
    full = lambda a: pl.BlockSpec(a.shape, lambda i: (0, 0))
    return pl.pallas_call(
        _odd_out_body,
        out_shape=[jax.ShapeDtypeStruct((t, d), F32), jax.ShapeDtypeStruct((t * ROW_SUBLANES, ROW_LANES), F32),
                   jax.ShapeDtypeStruct((t, ROUTE_LANES), F32)],
        grid=(t // tm,),
        in_specs=[row(d), row(MOBA_WIDTH), row(CONV_WIDTH), full(wc), full(wd),
                  pl.BlockSpec((1, d), lambda i: (0, 0)), full(router_pad)],
        out_specs=[row(d), pl.BlockSpec((tm * ROW_SUBLANES, ROW_LANES), lambda i: (i, 0)), row(ROUTE_LANES)],
        compiler_params=_params("parallel"),
        name="odd_out_proj_router",
    )(x2, c2, d2, wc, wd, gain.reshape(1, d), router_pad)


def _token_rows(token, count=1):
    return pl.ds(pl.multiple_of(token * ROW_SUBLANES, ROW_SUBLANES), count * ROW_SUBLANES)


def _row_copy(src_ref, src_row, dst_ref, dst_row, sem):
    return pltpu.make_async_copy(src_ref.at[_token_rows(src_row)], dst_ref.at[_token_rows(dst_row)], sem)


def _dispatch_body(pos_ref, pad_ref, h_ref, xs_ref, zeros_s, sem, pad_sem):
    td = h_ref.shape[0] // ROW_SUBLANES

    @pl.when(pl.program_id(0) == 0)
    def _():
        zeros_s[...] = jnp.zeros_like(zeros_s)
        tm = zeros_s.shape[0] // ROW_SUBLANES
        n_tiles = xs_ref.shape[0] // zeros_s.shape[0]
        n_live = pad_ref[N_EXPERTS]

        def fills():
            for e in range(N_EXPERTS):
                start = jnp.maximum(pad_ref[e], 0)
                yield pad_ref[e] >= 0, pltpu.make_async_copy(zeros_s, xs_ref.at[_token_rows(start, tm)], pad_sem)
            for c in range(N_EXPERTS):
                tile = n_tiles - 1 - c
                yield tile >= n_live, pltpu.make_async_copy(zeros_s, xs_ref.at[_token_rows(tile * tm, tm)], pad_sem)

        for has_tile, cp in fills():
            pl.when(has_tile)(cp.start)
        for has_tile, cp in fills():
            pl.when(has_tile)(cp.wait)

    def copies(r):
        return [_row_copy(h_ref, r, xs_ref, pos_ref[0, 0, k * td + r], sem) for k in range(2)]

    def start(r, c):
        for k, cp in enumerate(copies(r)):
            cp.start(priority=k)
        return c

    lax.fori_loop(0, td, start, 0, unroll=DMA_UNROLL)
    for _ in range(2):
        pltpu.make_async_copy(h_ref, xs_ref.at[_token_rows(0, td)], sem).wait()


def _dispatch(h3, pos3, pad_tiles, n_rows, *, td, tm):
    t = h3.shape[0] // ROW_SUBLANES
    return pl.pallas_call(
        _dispatch_body,
        out_shape=jax.ShapeDtypeStruct((n_rows * ROW_SUBLANES, ROW_LANES), F32),
        grid=(t // td,),
        in_specs=[
            pl.BlockSpec((1, 1, 2 * td), lambda i: (i, 0, 0), memory_space=pltpu.SMEM),
            pl.BlockSpec(memory_space=pltpu.SMEM),
            pl.BlockSpec((td * ROW_SUBLANES, ROW_LANES), lambda i: (i, 0)),
        ],
        out_specs=pl.BlockSpec(memory_space=pl.ANY),
        scratch_shapes=[pltpu.VMEM((tm * ROW_SUBLANES, ROW_LANES), F32), pltpu.SemaphoreType.DMA(()),
                        pltpu.SemaphoreType.DMA(())],
        compiler_params=_params("arbitrary"),
        name="expert_dispatch",
    )(pos3, pad_tiles, h3)


def _combine_body(pos_ref, x_ref, route_ref, y_ref, o_ref, buf, sems):
    i = pl.program_id(0)
    n = pl.num_programs(0)
    td = x_ref.shape[0]
    t = n * td

    def issue(tile, slot):
        def start(r, c):
            for k in range(2):
                _row_copy(y_ref, pos_ref[k * t + tile * td + r], buf.at[slot, k], r,
                          sems.at[slot]).start(priority=k)
            return c

        lax.fori_loop(0, td, start, 0, unroll=DMA_UNROLL)

    @pl.when(i == 0)
    def _():
        issue(0, 0)

    slot = i % 2

    @pl.when(i + 1 < n)
    def _():
        issue(i + 1, 1 - slot)

    for k in range(2):
        pltpu.make_async_copy(y_ref.at[_token_rows(0, td)], buf.at[slot, k], sems.at[slot]).wait()
    route = route_ref[...]
    o_ref[...] = (x_ref[...] + route[:, 2:3] * _load_token_tiles(buf.at[slot, 0])
                  + route[:, 3:4] * _load_token_tiles(buf.at[slot, 1]))


def _combine(x2, route, ys, pos, *, td):
    t, d = x2.shape
    return pl.pallas_call(
        _combine_body,
        out_shape=jax.ShapeDtypeStruct((t, d), F32),
        grid_spec=pltpu.PrefetchScalarGridSpec(
            num_scalar_prefetch=1,
            grid=(t // td,),
            in_specs=[
                pl.BlockSpec((td, d), lambda i, pos: (i, 0)),
                pl.BlockSpec((td, ROUTE_LANES), lambda i, pos: (i, 0)),
                pl.BlockSpec(memory_space=pl.ANY),
            ],
            out_specs=pl.BlockSpec((td, d), lambda i, pos: (i, 0)),
            scratch_shapes=[pltpu.VMEM((2, 2, td * ROW_SUBLANES, ROW_LANES), F32), pltpu.SemaphoreType.DMA((2,))],
        ),
        compiler_params=_params("arbitrary"),
        name="expert_combine",
    )(pos, x2, route, ys)


def _route_plan(route, *, tm, td):
    t = route.shape[0]
    experts = jnp.concatenate([route[:, 0], route[:, 1]]).astype(I32)
    onehot = (experts[:, None] == jnp.arange(N_EXPERTS, dtype=I32)[None, :]).astype(I32)
    running = jnp.cumsum(onehot, axis=0)
    counts = running[-1]
    padded = ((counts + tm - 1) // tm) * tm
    ends = jnp.cumsum(padded)
    starts = ends - padded
    pos = jnp.sum(onehot * (starts[None, :] + running - 1), axis=1)
    n_tiles = (2 * t) // tm + N_EXPERTS
    tile_lo = jnp.arange(n_tiles, dtype=I32) * tm
    tile_expert = jnp.sum((tile_lo[:, None] >= ends[None, :]).astype(I32), axis=1)
    n_live = (ends[-1] // tm).astype(I32)
    last = tile_expert[jnp.maximum(n_live - 1, 0)]
    live = jnp.arange(n_tiles) < n_live
    tile_expert = jnp.where(live, tile_expert, last)
    prev = jnp.concatenate([jnp.full((1,), -1, I32), tile_expert[:-1]])
    tile_first = (live & (tile_expert != prev)).astype(I32)
    pos3 = pos.reshape(2, t // td, td).transpose(1, 0, 2).reshape(t // td, 1, 2 * td)
    pad_tiles = jnp.concatenate([jnp.where(padded > 0, ends - tm, -1), n_live.reshape(1)]).astype(I32)
    return pos, pos3, pad_tiles, tile_expert, tile_first, n_live.reshape(1), n_tiles * tm


def _rope_tables(seq):
    inv = 1.0 / (ROPE_THETA ** (jnp.arange(0, HEAD_DIM, 2, dtype=F32) / HEAD_DIM))
    ang = jnp.arange(seq, dtype=F32)[:, None] * inv[None, :]
    ang = jnp.concatenate([ang, ang], axis=-1)
    tile = lambda a: jnp.tile(a, (1, HEADS_PER_TILE))
    return tile(jnp.cos(ang)), tile(jnp.sin(ang))


def _even_layer(x2, b, s, cos_t, sin_t, norm_mix, norm_ffn, w_in, v_gain, ws, bs, gq, gk, w_out, wg, wu, wd):
    p = _norm_proj(x2, norm_mix, w_in.astype(BF16))
    p3 = p.reshape(b, s, p.shape[1])
    a = _gmlp(p3, v_gain, ws, bs)
    q0 = 2 * GMLP_WIDTH // LANE_TILE
    n_groups = len(DIL_CONFIGS)
    accs = []
    for group, (window, dil) in enumerate(DIL_CONFIGS):
        assert window // dil == BAND
        qg, kg, vtg = _qkv_prep(p3, cos_t, sin_t, gq, gk, q_col=q0 + group,
                                k_col=q0 + n_groups + group, v_col=q0 + 2 * n_groups + group, n_ht=1, dil=dil,
                                tt=max(4, dil) * BAND, vblock=BAND, head_rows=BAND_ACC_ROWS, with_kmean=False)
        accs.append(_band_attn(qg, kg, vtg))
    w_out = w_out.astype(BF16)
    x3 = _even_out(x2.reshape(b, s, -1), a, accs, w_out[:GMLP_WIDTH], w_out[GMLP_WIDTH:])
    return _ffn(x3.reshape(b * s, -1), norm_ffn, wg.astype(BF16), wu.astype(BF16), wd.astype(BF16))


def _odd_layer(x2, b, s, cos_t, sin_t, norm_mix, norm_ffn, w_in, gq, gk, conv_w, w_out, router_w, wg, wu, wd,
               *, layer, tm=512, td_dispatch=512, td_combine=256):
    p = _norm_proj(x2, norm_mix, w_in.astype(BF16))
    p3 = p.reshape(b, s, p.shape[1])
    tiles = MOBA_WIDTH // LANE_TILE
    qp, kp, vtp, kmean = _qkv_prep(p3, cos_t, sin_t, gq, gk, q_col=0, k_col=tiles,
                                   v_col=2 * tiles, n_ht=tiles, dil=1, tt=8 * MOBA_BLOCK, vblock=MOBA_BLOCK,
                                   head_rows=MOBA_ACC_ROWS, with_kmean=True)
    c = _moba(qp, kp, vtp, kmean)
    dconv = _short_conv(p3, conv_w, col0=3 * MOBA_WIDTH // CONV_WIDTH)
    w_out = w_out.astype(BF16)
    router_pad = jnp.pad(router_w, ((0, 0), (0, ROUTE_LANES - N_EXPERTS)))
    x2, h2, route = _odd_out(x2, c.reshape(b * s, MOBA_WIDTH), dconv.reshape(b * s, CONV_WIDTH),
                             w_out[:MOBA_WIDTH], w_out[MOBA_WIDTH:], norm_ffn, router_pad)
    pos, pos3, pad_tiles, tile_expert, tile_first, n_live, n_rows = _route_plan(route, tm=tm, td=td_dispatch)
    xs = _dispatch(h2, pos3, pad_tiles, n_rows, td=td_dispatch, tm=tm)
    ys = _expert_ffn(xs, tile_expert, tile_first, n_live, wg, wu, wd, layer=layer, tm=tm)
    return _combine(x2, route, ys, pos, td=td_combine)


def kernel(x, norm_mix, norm_ffn, w_in_ab, gmlp_v_gain, gmlp_ws, gmlp_bs, dil_q_gain, dil_k_gain, w_out_ab,
           ffn_w_gate, ffn_w_up, ffn_w_down, w_in_cd, moba_q_gain, moba_k_gain, conv_w, w_out_cd, router_w,
           moe_w_gate, moe_w_up, moe_w_down):
    b, s, d = x.shape
    depth = norm_mix.shape[0]
    cos_t, sin_t = _rope_tables(s)
    x2 = x.reshape(b * s, d)
    for layer in range(depth):
        i = layer // 2
        if layer % 2 == 0:
            x2 = _even_layer(x2, b, s, cos_t, sin_t, norm_mix[layer], norm_ffn[layer], w_in_ab[i], gmlp_v_gain[i],
                             gmlp_ws[i], gmlp_bs[i], dil_q_gain[i], dil_k_gain[i], w_out_ab[i], ffn_w_gate[i],
                             ffn_w_up[i], ffn_w_down[i])
        else:
            x2 = _odd_layer(x2, b, s, cos_t, sin_t, norm_mix[layer], norm_ffn[layer], w_in_cd[i], moba_q_gain[i],
                            moba_k_gain[i], conv_w[i], w_out_cd[i], router_w[i], moe_w_gate, moe_w_up,
                            moe_w_down, layer=i)
    return x2.reshape(b, s, d)
```
